```python
import math
import jax
import jax.numpy as jnp
from jax import lax
import numpy as np

D_MODEL = 1024
BATCH = 32
SEQ = 256
DEPTH = 2
DEC_BATCH = 2
DEC_SEQ = 2048
PAST_LEN = 512

GRID_W = 64
N_ATTN_LAYERS = (DEPTH + 1) // 2
N_CONV_LAYERS = DEPTH // 2
N_MOD = 9
D_FF = 2816
EPS = 1e-6
MLA_HEADS = 8
Q_LORA = 384
KV_LORA = 256
NOPE_DIM = 64
ROPE_DIM = 32
V_DIM = 64
QK_DIM = NOPE_DIM + ROPE_DIM
ROPE_BASE = 10000.0
Q_BLOCK = 128
S5_WIDTH = D_MODEL // 2
S5_GROUP = 16
S5_GROUPS = S5_WIDTH // S5_GROUP
S5_STATE = 64
S5_DIRS = 2
DT_MIN = 0.001
DT_MAX = 0.1
LAMBDA_RE_MAX = -1e-4
AB_IN = Q_LORA + KV_LORA + ROPE_DIM + S5_WIDTH
AB_OUT = MLA_HEADS * V_DIM + S5_WIDTH
CONV_WIDTH = D_MODEL
CONV_K = 3

kernel_name = 'hybrid_diffusion_mla_s5_shortconv_step'


def rms_norm(x, g):
    xf = x.astype(jnp.float32)
    y = xf * lax.rsqrt(jnp.mean(xf * xf, axis=-1, keepdims=True) + EPS)
    return (y * g.astype(jnp.float32)).astype(x.dtype)


def modulate(x, g, shift, scale):
    return rms_norm(x, g) * (1 + scale) + shift


def swiglu(h, w_gate, w_up, w_down):
    return (jax.nn.silu(h @ w_gate) * (h @ w_up)) @ w_down


def grid_positions(length):
    rows = length // GRID_W
    row = jnp.repeat(jnp.arange(rows, dtype=jnp.float32), GRID_W)
    col = (jnp.arange(rows * GRID_W) % GRID_W).astype(jnp.float32)
    return row, col


def axial_rope(x):
    row, col = grid_positions(x.shape[1])
    half = ROPE_DIM // 2
    quarter = half // 2
    inv_freq = ROPE_BASE ** (-jnp.arange(quarter, dtype=jnp.float32) / quarter)

    def rotate(xa, pos):
        ang = pos[:, None] * inv_freq[None, :]
        cos = jnp.cos(ang)[None, :, None, :]
        sin = jnp.sin(ang)[None, :, None, :]
        xa = xa.astype(jnp.float32)
        x1, x2 = xa[..., :quarter], xa[..., quarter:]
        return jnp.concatenate([x1 * cos - x2 * sin, x1 * sin + x2 * cos], axis=-1)

    out = jnp.concatenate([rotate(x[..., :half], row), rotate(x[..., half:], col)], axis=-1)
    return out.astype(x.dtype)


def mla_queries(cq, g_q_lat, w_uq, g_qn, rotate):
    b, l, _ = cq.shape
    q = (rms_norm(cq, g_q_lat) @ w_uq).reshape(b, l, MLA_HEADS, QK_DIM)
    q = rms_norm(q, g_qn)
    if rotate:
        q = jnp.concatenate([q[..., :NOPE_DIM], axial_rope(q[..., NOPE_DIM:])], axis=-1)
    return q


def mla_keys_values(ckv, kr, w_ukv, g_kn, rotate):
    b, l, _ = ckv.shape
    kv = (ckv @ w_ukv).reshape(b, l, MLA_HEADS, NOPE_DIM + V_DIM)
    k_nope, v = kv[..., :NOPE_DIM], kv[..., NOPE_DIM:]
    k_rope = jnp.broadcast_to(kr[:, :, None, :], (b, l, MLA_HEADS, ROPE_DIM)).astype(k_nope.dtype)
    k = rms_norm(jnp.concatenate([k_nope, k_rope], axis=-1), g_kn)
    if rotate:
        k = jnp.concatenate([k[..., :NOPE_DIM], axial_rope(k[..., NOPE_DIM:])], axis=-1)
    return k, v


def blocked_attention(q, k, v):
    b, lq, h, dh = q.shape
    n_blocks = lq // Q_BLOCK
    qb = q.reshape(b, n_blocks, Q_BLOCK, h, dh).transpose(1, 0, 2, 3, 4)
    scale = dh ** -0.5

    def one_block(q_blk):
        s = jnp.einsum('bqhd,bkhd->bhqk', q_blk, k).astype(jnp.float32) * scale
        p = jax.nn.softmax(s, axis=-1).astype(v.dtype)
        return jnp.einsum('bhqk,bkhd->bqhd', p, v)

    o = lax.map(one_block, qb)
    return o.transpose(1, 0, 2, 3, 4).reshape(b, lq, h, v.shape[-1])


def complex_linear_combine(e1, e2):
    a1r, a1i, b1r, b1i = e1
    a2r, a2i, b2r, b2i = e2
    ar = a2r * a1r - a2i * a1i
    ai = a2r * a1i + a2i * a1r
    br = a2r * b1r - a2i * b1i + b2r
    bi = a2r * b1i + a2i * b1r + b2i
    return ar, ai, br, bi


def s5_direction(u, lam_re, lam_im, log_dt, b_re, b_im, c_re, c_im, h0, reverse):
    dt = jnp.exp(log_dt)[:, None]
    lr = jnp.minimum(lam_re, LAMBDA_RE_MAX)
    li = lam_im
    mag = jnp.exp(lr * dt)
    ang = li * dt
    ab_re = mag * jnp.cos(ang)
    ab_im = mag * jnp.sin(ang)
    den = lr * lr + li * li
    nr = ab_re - 1.0
    ni = ab_im
    co_re = (nr * lr + ni * li) / den
    co_im = (ni * lr - nr * li) / den
    bb_re = co_re[..., None] * b_re - co_im[..., None] * b_im
    bb_im = co_re[..., None] * b_im + co_im[..., None] * b_re
    if reverse:
        u = jnp.flip(u, axis=1)
    bu_re = jnp.einsum('blgi,gpi->blgp', u, bb_re)
    bu_im = jnp.einsum('blgi,gpi->blgp', u, bb_im)
    if h0 is not None:
        h_re, h_im = h0
        bu_re = bu_re.at[:, 0].add(ab_re * h_re - ab_im * h_im)
        bu_im = bu_im.at[:, 0].add(ab_re * h_im + ab_im * h_re)
    a_re = jnp.broadcast_to(ab_re, bu_re.shape)
    a_im = jnp.broadcast_to(ab_im, bu_im.shape)
    _, _, s_re, s_im = lax.associative_scan(complex_linear_combine, (a_re, a_im, bu_re, bu_im), axis=1)
    y = jnp.einsum('blgp,gip->blgi', s_re, c_re) - jnp.einsum('blgp,gip->blgi', s_im, c_im)
    if reverse:
        y = jnp.flip(y, axis=1)
    return y, s_re[:, -1], s_im[:, -1]


def s5_layer(u, lam_re, lam_im, log_dt, b_re, b_im, c_re, c_im, d_skip, w_glu, b_glu, h0_re, h0_im):
    b, l, _ = u.shape
    f32 = jnp.float32
    uf = u.astype(f32)
    ug = uf.reshape(b, l, S5_GROUPS, S5_GROUP)
    y = d_skip.astype(f32) * uf
    fin_re, fin_im = [], []
    for dr in range(S5_DIRS):
        h0 = None if h0_re is None else (h0_re[:, dr].astype(f32), h0_im[:, dr].astype(f32))
        y_dir, f_re, f_im = s5_direction(ug, lam_re[dr].astype(f32), lam_im[dr].astype(f32), log_dt[dr].astype(f32),
                                         b_re[dr].astype(f32), b_im[dr].astype(f32), c_re[dr].astype(f32),
                                         c_im[dr].astype(f32), h0, dr == 1)
        y = y + y_dir.reshape(b, l, S5_WIDTH)
        fin_re.append(f_re)
        fin_im.append(f_im)
    z = jax.nn.gelu(y)
    out = z * jax.nn.sigmoid(z @ w_glu.astype(f32) + b_glu.astype(f32))
    return out.astype(u.dtype), jnp.stack(fin_re, axis=1).astype(u.dtype), jnp.stack(fin_im, axis=1).astype(u.dtype)


def ab_mixer(hn, ctx_ckv, ctx_kr, h0_re, h0_im, w_in, g_q_lat, g_kv_lat, w_uq, w_ukv, g_qn, g_kn,
             lam_re, lam_im, log_dt, b_re, b_im, c_re, c_im, d_skip, w_glu, b_glu, w_out):
    latent = ctx_ckv is not None
    b, l, _ = hn.shape
    proj = hn @ w_in
    o1 = Q_LORA
    o2 = o1 + KV_LORA
    o3 = o2 + ROPE_DIM
    cq, ckv_raw, kr, u = proj[..., :o1], proj[..., o1:o2], proj[..., o2:o3], proj[..., o3:]
    ckv = rms_norm(ckv_raw, g_kv_lat)
    q = mla_queries(cq, g_q_lat, w_uq, g_qn, latent)
    k, v = mla_keys_values(ckv, kr, w_ukv, g_kn, latent)
    if latent:
        k_c, v_c = mla_keys_values(ctx_ckv, ctx_kr, w_ukv, g_kn, False)
        k = jnp.concatenate([k_c.astype(k.dtype), k], axis=1)
        v = jnp.concatenate([v_c.astype(v.dtype), v], axis=1)
    attn = blocked_attention(q, k, v).reshape(b, l, MLA_HEADS * V_DIM)
    s5_out, fin_re, fin_im = s5_layer(u, lam_re, lam_im, log_dt, b_re, b_im, c_re, c_im, d_skip, w_glu, b_glu,
                                      h0_re, h0_im)
    out = jnp.concatenate([attn, s5_out.astype(attn.dtype)], axis=-1) @ w_out
    return out, ckv, kr, fin_re, fin_im


def conv_mixer(hn, w_in, conv_w, w_out):
    gate_b, gate_c, h = jnp.split(hn @ w_in, 3, axis=-1)
    z = gate_c * h
    rhs = conv_w.T[:, None, :].astype(z.dtype)
    zc = lax.conv_general_dilated(z, rhs, window_strides=(1,), padding=[(CONV_K // 2, CONV_K // 2)],
                                  dimension_numbers=('NWC', 'WIO', 'NWC'), feature_group_count=CONV_WIDTH)
    return (gate_b * zc) @ w_out


def setup_inputs(seed: int = 0) -> dict:
    key = jax.random.key(seed)
    ks = iter(jax.random.split(key, 40))
    f32 = jnp.float32

    def nrm(shape, scale):
        return jax.random.normal(next(ks), shape, f32) * scale

    na, nc = N_ATTN_LAYERS, N_CONV_LAYERS
    lam_im_init = jnp.pi * jnp.arange(S5_STATE, dtype=f32)
    return {
        'x_prompt': nrm((BATCH, SEQ, D_MODEL), 1.0),
        'x_sample': nrm((DEC_BATCH, DEC_SEQ, D_MODEL), 1.0),
        'cache_ckv': nrm((DEC_BATCH, na, PAST_LEN, KV_LORA), 1.0),
        'cache_krope': nrm((DEC_BATCH, na, PAST_LEN, ROPE_DIM), 1.0),
        'state_ssm_re': nrm((DEC_BATCH, na, S5_DIRS, S5_GROUPS, S5_STATE), 1.0),
        'state_ssm_im': nrm((DEC_BATCH, na, S5_DIRS, S5_GROUPS, S5_STATE), 1.0),
        'c': nrm((DEC_BATCH, D_MODEL), 1.0),
        'c_ctx': nrm((D_MODEL,), 1.0),
        'w_ada': nrm((DEPTH, D_MODEL, N_MOD * D_MODEL), 0.5 * D_MODEL ** -0.5),
        'b_ada': nrm((DEPTH, N_MOD * D_MODEL), 0.02),
        'norm_g': 1.0 + nrm((DEPTH, 3, D_MODEL), 0.01),
        'ffn_w_gate': nrm((DEPTH, 2, D_MODEL, D_FF), D_MODEL ** -0.5),
        'ffn_w_up': nrm((DEPTH, 2, D_MODEL, D_FF), D_MODEL ** -0.5),
        'ffn_w_down': nrm((DEPTH, 2, D_FF, D_MODEL), D_FF ** -0.5),
        'ab_w_in': nrm((na, D_MODEL, AB_IN), D_MODEL ** -0.5),
        'mla_g_q_lat': 1.0 + nrm((na, Q_LORA), 0.01),
        'mla_g_kv_lat': 1.0 + nrm((na, KV_LORA), 0.01),
        'mla_w_uq': nrm((na, Q_LORA, MLA_HEADS * QK_DIM), Q_LORA ** -0.5),
        'mla_w_ukv': nrm((na, KV_LORA, MLA_HEADS * (NOPE_DIM + V_DIM)), KV_LORA ** -0.5),
        'mla_g_qnorm': 1.0 + nrm((na, QK_DIM), 0.01),
        'mla_g_knorm': 1.0 + nrm((na, QK_DIM), 0.01),
        's5_lam_re': -0.5 + nrm((na, S5_DIRS, S5_GROUPS, S5_STATE), 0.01),
        's5_lam_im': lam_im_init + nrm((na, S5_DIRS, S5_GROUPS, S5_STATE), 0.01),
        's5_log_dt': jax.random.uniform(next(ks), (na, S5_DIRS, S5_GROUPS), f32,
                                        minval=math.log(DT_MIN), maxval=math.log(DT_MAX)),
        's5_b_re': nrm((na, S5_DIRS, S5_GROUPS, S5_STATE, S5_GROUP), (0.5 / S5_GROUP) ** 0.5),
        's5_b_im': nrm((na, S5_DIRS, S5_GROUPS, S5_STATE, S5_GROUP), (0.5 / S5_GROUP) ** 0.5),
        's5_c_re': nrm((na, S5_DIRS, S5_GROUPS, S5_GROUP, S5_STATE), (0.5 / S5_STATE) ** 0.5),
        's5_c_im': nrm((na, S5_DIRS, S5_GROUPS, S5_GROUP, S5_STATE), (0.5 / S5_STATE) ** 0.5),
        's5_d': nrm((na, S5_WIDTH), 0.5),
        's5_w_glu': nrm((na, S5_WIDTH, S5_WIDTH), S5_WIDTH ** -0.5),
        's5_b_glu': nrm((na, S5_WIDTH), 0.02),
        'ab_w_out': nrm((na, AB_OUT, D_MODEL), AB_OUT ** -0.5),
        'conv_w_in': nrm((nc, D_MODEL, 3 * CONV_WIDTH), D_MODEL ** -0.5),
        'conv_w': nrm((nc, CONV_WIDTH, CONV_K), CONV_K ** -0.5),
        'conv_w_out': nrm((nc, CONV_WIDTH, D_MODEL), CONV_WIDTH ** -0.5),
    }


def reference(x_prompt, x_sample, cache_ckv, cache_krope, state_ssm_re, state_ssm_im, c, c_ctx,
              w_ada, b_ada, norm_g, ffn_w_gate, ffn_w_up, ffn_w_down,
              ab_w_in, mla_g_q_lat, mla_g_kv_lat, mla_w_uq, mla_w_ukv, mla_g_qnorm, mla_g_knorm,
              s5_lam_re, s5_lam_im, s5_log_dt, s5_b_re, s5_b_im, s5_c_re, s5_c_im, s5_d, s5_w_glu, s5_b_glu,
              ab_w_out, conv_w_in, conv_w, conv_w_out):

    def trunk(x, cond, use_cache):
        ckv_list, kr_list, sre_list, sim_list = [], [], [], []
        for l in range(DEPTH):
            mods = (jax.nn.silu(cond) @ w_ada[l] + b_ada[l]).reshape(cond.shape[0], N_MOD, 1, D_MODEL)
            h = modulate(x, norm_g[l, 0], mods[:, 0], mods[:, 1])
            x = x + 0.5 * mods[:, 2] * swiglu(h, ffn_w_gate[l, 0], ffn_w_up[l, 0], ffn_w_down[l, 0])
            h = modulate(x, norm_g[l, 1], mods[:, 3], mods[:, 4])
            if l % 2 == 0:
                a = l // 2
                if use_cache:
                    ctx_ckv, ctx_kr = cache_ckv[:, a], cache_krope[:, a]
                    h0_re, h0_im = state_ssm_re[:, a], state_ssm_im[:, a]
                else:
                    ctx_ckv, ctx_kr, h0_re, h0_im = None, None, None, None
                mix, ckv, kr, f_re, f_im = ab_mixer(
                    h, ctx_ckv, ctx_kr, h0_re, h0_im, ab_w_in[a], mla_g_q_lat[a], mla_g_kv_lat[a], mla_w_uq[a],
                    mla_w_ukv[a], mla_g_qnorm[a], mla_g_knorm[a], s5_lam_re[a], s5_lam_im[a], s5_log_dt[a],
                    s5_b_re[a], s5_b_im[a], s5_c_re[a], s5_c_im[a], s5_d[a], s5_w_glu[a], s5_b_glu[a], ab_w_out[a])
                if not use_cache:
                    ckv_list.append(ckv)
                    kr_list.append(kr)
                    sre_list.append(f_re)
                    sim_list.append(f_im)
            else:
                m = l // 2
                mix = conv_mixer(h, conv_w_in[m], conv_w[m], conv_w_out[m])
            x = x + mods[:, 5] * mix
            h = modulate(x, norm_g[l, 2], mods[:, 6], mods[:, 7])
            x = x + 0.5 * mods[:, 8] * swiglu(h, ffn_w_gate[l, 1], ffn_w_up[l, 1], ffn_w_down[l, 1])
        return x, ckv_list, kr_list, sre_list, sim_list

    y_prompt, ckv_l, kr_l, sre_l, sim_l = trunk(x_prompt, c_ctx[None, :], False)
    new_ckv = jnp.stack(ckv_l, axis=1)
    new_krope = jnp.stack(kr_l, axis=1)
    new_ssm_re = jnp.stack(sre_l, axis=1)
    new_ssm_im = jnp.stack(sim_l, axis=1)
    y_sample, _, _, _, _ = trunk(x_sample, c, True)
    return (y_prompt, y_sample, new_ckv, new_krope, new_ssm_re, new_ssm_im)
```

```python
import functools
import math

import jax
import jax.numpy as jnp
import numpy as np
from jax import lax
from jax.experimental import pallas as pl
from jax.experimental.pallas import tpu as pltpu

F32 = jnp.float32
BF16 = jnp.bfloat16

LANES = 128
SUBLANES = 8
VMEM_LIMIT_BYTES = 60 * 1024 * 1024

D_MODEL = 1024
D_FF = 2816
N_MOD = 9
EPS = 1e-6
HEADS = 8
Q_LORA = 384
KV_LORA = 256
NOPE = 64
ROPE = 32
V_DIM = 64
QK_DIM = NOPE + ROPE
HEAD_PAD = LANES
ROPE_BASE = 10000.0
GRID_W = 64
S5_WIDTH = 512
S5_GROUP = 16
S5_GROUPS = 32
S5_STATE = 64
S5_LANES = S5_GROUPS * S5_STATE
S5_QUARTERS = 4
S5_QLANES = S5_LANES // S5_QUARTERS
S5_SLABS = S5_QLANES // LANES
S5_TC = 256
S5_PITCH = S5_TC + 4
LAMBDA_RE_MAX = -1e-4
CONV_K = 3

TM = 512
FF_CHUNKS = tuple((c, min(512, D_FF - c)) for c in range(0, D_FF, 512))

W_IN_COLS = Q_LORA + KV_LORA + S5_WIDTH + LANES
O_CKV = Q_LORA
O_U = Q_LORA + KV_LORA
O_KR = O_U + S5_WIDTH


def _dot(a, b):
    return jnp.dot(a, b, preferred_element_type=F32)


def _sigmoid(x):
    return 1.0 / (1.0 + jnp.exp(-x))


def _rms(x, g, n):
    ms = jnp.sum(x * x, axis=-1, keepdims=True) * (1.0 / n)
    return x * lax.rsqrt(ms + EPS) * g


def _modulate(x, g, shift, scale):
    return _rms(x, g, D_MODEL) * (1.0 + scale) + shift


def _ffn(x, shift, scale, gate, g, wg_ref, wu_ref, wd_ref):
    h = _modulate(x, g, shift, scale).astype(BF16)
    acc = jnp.zeros(x.shape, F32)
    for c0, cs in FF_CHUNKS:
        gt = _dot(h, wg_ref[:, c0:c0 + cs])
        up = _dot(h, wu_ref[:, c0:c0 + cs])
        act = ((gt * _sigmoid(gt)) * up).astype(BF16)
        acc = acc + _dot(act, wd_ref[c0:c0 + cs, :])
    return x + (0.5 * gate) * acc


def _head_norm(xh, g):
    ms = jnp.sum(xh * xh, axis=-1, keepdims=True) * (1.0 / QK_DIM)
    return xh * lax.rsqrt(ms + EPS) * g


def _rope(xh, cos, sin_lo, sin_hi):
    return xh * cos + pltpu.roll(xh, LANES - 8, 1) * sin_lo + pltpu.roll(xh, 8, 1) * sin_hi


def _gelu_tanh(x):
    return x * (0.5 * (1.0 + jnp.tanh(math.sqrt(2.0 / math.pi) * (x + 0.044715 * (x * x * x)))))


def _const_spec(shape, index):
    return pl.BlockSpec(shape, lambda *_: index, pipeline_mode=pl.Buffered(1))


def _params(n_axes):
    return pltpu.CompilerParams(dimension_semantics=("arbitrary",) * n_axes, vmem_limit_bytes=VMEM_LIMIT_BYTES)


def _ffn_specs(layer, sub):
    w = _const_spec((None, None, D_MODEL, D_FF), (layer, sub, 0, 0))
    return [w, w, _const_spec((None, None, D_FF, D_MODEL), (layer, sub, 0, 0))]


def _mods_spec(layer, cond_of_tile):
    return pl.BlockSpec((None, None, N_MOD, D_MODEL), lambda i: (layer, cond_of_tile(i), 0, 0))


def _normg_spec(layer):
    return _const_spec((None, 3, D_MODEL), (layer, 0, 0))


def _row_spec(width, rows=TM):
    return pl.BlockSpec((rows, width), lambda i: (i, 0))


def _ada_body(c_ref, w_ref, b_ref, o_ref):
    c = c_ref[...]
    s = (c * _sigmoid(c)).astype(BF16)
    o_ref[...] = _dot(s, w_ref[...].astype(BF16)) + b_ref[...]


def _ada(cond, w_ada, b_ada):
    depth, _, n = w_ada.shape
    tn = 1152
    return pl.pallas_call(
        _ada_body,
        out_shape=jax.ShapeDtypeStruct((depth, SUBLANES, n), F32),
        grid=(depth, n // tn),
        in_specs=[pl.BlockSpec((SUBLANES, D_MODEL), lambda l, j: (0, 0)),
                  pl.BlockSpec((None, D_MODEL, tn), lambda l, j: (l, 0, j)),
                  pl.BlockSpec((None, 1, tn), lambda l, j: (l, 0, j))],
        out_specs=pl.BlockSpec((None, SUBLANES, tn), lambda l, j: (l, 0, j)),
        compiler_params=_params(2),
        name="ada",
    )(cond, w_ada, b_ada.reshape(depth, 1, n))


def _mixin_body(rotate, want_cache, *refs):
    (x_ref, mods_ref, g_ref, wg_ref, wu_ref, wd_ref, win_ref, gql_ref, gkv_ref, wuq_ref, wukv_ref,
     gq_ref, gk_ref) = refs[:13]
    refs = refs[13:]
    if rotate:
        cos_ref, slo_ref, shi_ref = refs[:3]
        refs = refs[3:]
    x1_ref, q_ref, k_ref, v_ref, u_ref = refs[:5]
    mods = mods_ref[...]
    g3 = g_ref[...]
    x1 = _ffn(x_ref[...], mods[0:1], mods[1:2], mods[2:3], g3[0:1], wg_ref, wu_ref, wd_ref)
    x1_ref[...] = x1
    hn = _modulate(x1, g3[1:2], mods[3:4], mods[4:5]).astype(BF16)
    proj = _dot(hn, win_ref[...])
    u_ref[...] = proj[:, O_U:O_KR]
    ckv = _rms(proj[:, O_CKV:O_U], gkv_ref[...], KV_LORA)
    krg = proj[:, O_KR:O_KR + LANES]
    if want_cache:
        ckv_ref, kr_ref = refs[5:7]
        ckv_ref[...] = ckv
        kr_ref[...] = krg[:, 0:ROPE]
    qn = _rms(proj[:, 0:Q_LORA], gql_ref[...], Q_LORA).astype(BF16)
    qraw = _dot(qn, wuq_ref[...])
    kv = _dot(ckv.astype(BF16), wukv_ref[...])
    v_ref[...] = kv[:, HEADS * HEAD_PAD:].astype(BF16)
    lane = lax.broadcasted_iota(jnp.int32, (1, LANES), 1)
    kr_only = jnp.where((lane >= NOPE) & (lane < QK_DIM), krg, 0.0)
    gq = gq_ref[...]
    gk = gk_ref[...]
    for h in range(HEADS):
        sl = slice(h * HEAD_PAD, (h + 1) * HEAD_PAD)
        qh = _head_norm(qraw[:, sl], gq)
        kh = _head_norm(kv[:, sl] + kr_only, gk)
        if rotate:
            qh = _rope(qh, cos_ref[...], slo_ref[...], shi_ref[...])
            kh = _rope(kh, cos_ref[...], slo_ref[...], shi_ref[...])
        q_ref[:, sl] = qh.astype(BF16)
        k_ref[:, sl] = kh.astype(BF16)


def _mixin(x, mods, norm_g, wts, cond_of_tile, seq_len, rotate, want_cache):
    rows = x.shape[0]
    tiles_per_seq = seq_len // TM if seq_len >= TM else 1
    in_specs = [_row_spec(D_MODEL), _mods_spec(0, cond_of_tile), _normg_spec(0)] + _ffn_specs(0, 0) + [
        _const_spec((D_MODEL, W_IN_COLS), (0, 0)),
        _const_spec((1, Q_LORA), (0, 0)),
        _const_spec((1, KV_LORA), (0, 0)),
        _const_spec((Q_LORA, HEADS * HEAD_PAD), (0, 0)),
        _const_spec((KV_LORA, HEADS * HEAD_PAD + HEADS * V_DIM), (0, 0)),
        _const_spec((1, HEAD_PAD), (0, 0)),
        _const_spec((1, HEAD_PAD), (0, 0)),
    ]
    args = [x, mods, norm_g, wts["ffn_g"], wts["ffn_u"], wts["ffn_d"], wts["w_in"], wts["g_q_lat"], wts["g_kv_lat"],
            wts["w_uq"], wts["w_ukv"], wts["g_qn"], wts["g_kn"]]
    if rotate:
        in_specs += [pl.BlockSpec((TM, LANES), lambda i: (i % tiles_per_seq, 0))] * 3
        args += list(wts["rope"])
    out_shape = [jax.ShapeDtypeStruct((rows, D_MODEL), F32),
                 jax.ShapeDtypeStruct((rows, HEADS * HEAD_PAD), BF16),
                 jax.ShapeDtypeStruct((rows, HEADS * HEAD_PAD), BF16),
                 jax.ShapeDtypeStruct((rows, HEADS * V_DIM), BF16),
                 jax.ShapeDtypeStruct((rows, S5_WIDTH), F32)]
    out_specs = [_row_spec(D_MODEL), _row_spec(HEADS * HEAD_PAD), _row_spec(HEADS * HEAD_PAD),
                 _row_spec(HEADS * V_DIM), _row_spec(S5_WIDTH)]
    if want_cache:
        out_shape += [jax.ShapeDtypeStruct((rows, KV_LORA), F32), jax.ShapeDtypeStruct((rows, ROPE), F32)]
        out_specs += [_row_spec(KV_LORA), _row_spec(ROPE)]
    return pl.pallas_call(
        functools.partial(_mixin_body, rotate, want_cache),
        out_shape=out_shape, grid=(rows // TM,), in_specs=in_specs, out_specs=out_specs,
        compiler_params=_params(1), name="ffn_mixin",
    )(*args)


def _ctx_body(ckv_ref, krp_ref, wukv_ref, gk_ref, k_ref, v_ref):
    kv = _dot(ckv_ref[...].astype(BF16), wukv_ref[...])
    v_ref[...] = kv[:, HEADS * HEAD_PAD:].astype(BF16)
    gk = gk_ref[...]
    krp = krp_ref[...]
    for h in range(HEADS):
        sl = slice(h * HEAD_PAD, (h + 1) * HEAD_PAD)
        k_ref[:, sl] = _head_norm(kv[:, sl] + krp, gk).astype(BF16)


def _ctx_kv(ckv, kr_padded, wts):
    rows = ckv.shape[0]
    return pl.pallas_call(
        _ctx_body,
        out_shape=[jax.ShapeDtypeStruct((rows, HEADS * HEAD_PAD), BF16),
                   jax.ShapeDtypeStruct((rows, HEADS * V_DIM), BF16)],
        grid=(rows // TM,),
        in_specs=[_row_spec(KV_LORA), _row_spec(LANES),
                  _const_spec((KV_LORA, HEADS * HEAD_PAD + HEADS * V_DIM), (0, 0)),
                  _const_spec((1, HEAD_PAD), (0, 0))],
        out_specs=[_row_spec(HEADS * HEAD_PAD), _row_spec(HEADS * V_DIM)],
        compiler_params=_params(1), name="ctx_kv",
    )(ckv, kr_padded, wts["w_ukv"], wts["g_kn"])


def _attn_body(n_kv, *refs):
    q_ref = refs[0]
    k_refs = refs[1:1 + n_kv]
    v_refs = refs[1 + n_kv:1 + 2 * n_kv]
    o_ref = refs[1 + 2 * n_kv]
    scale = QK_DIM ** -0.5
    lane = lax.broadcasted_iota(jnp.int32, (1, LANES), 1)
    for pair in range(HEADS // 2):
        vsl = slice(pair * LANES, (pair + 1) * LANES)
        outs = []
        for h in (2 * pair, 2 * pair + 1):
            sl = slice(h * HEAD_PAD, (h + 1) * HEAD_PAD)
            q = q_ref[:, sl]
            s = [lax.dot_general(q, k_ref[:, sl], (((1,), (1,)), ((), ())), preferred_element_type=F32) * scale
                 for k_ref in k_refs]
            m = functools.reduce(jnp.maximum, [jnp.max(si, axis=-1, keepdims=True) for si in s])
            e = [jnp.exp(si - m) for si in s]
            r = 1.0 / functools.reduce(jnp.add, [jnp.sum(ei, axis=-1, keepdims=True) for ei in e])
            outs.append(functools.reduce(
                jnp.add, [_dot((ei * r).astype(BF16), v_ref[:, vsl]) for ei, v_ref in zip(e, v_refs)]))
        o_ref[:, vsl] = jnp.where(lane < V_DIM, outs[0], outs[1]).astype(BF16)


def _attention(q, k, v, seq_len, tq, ctx=None):
    rows = q.shape[0]
    q_tiles = seq_len // tq
    kv_spec = lambda w: pl.BlockSpec((seq_len, w), lambda b, t: (b, 0))
    in_specs = [pl.BlockSpec((tq, HEADS * HEAD_PAD), lambda b, t: (b * q_tiles + t, 0))]
    ks, vs = [kv_spec(HEADS * HEAD_PAD)], [kv_spec(HEADS * V_DIM)]
    kargs, vargs = [k], [v]
    if ctx is not None:
        k_c, v_c = ctx
        past = k_c.shape[0] // (rows // seq_len)
        ks.insert(0, pl.BlockSpec((past, HEADS * HEAD_PAD), lambda b, t: (b, 0)))
        vs.insert(0, pl.BlockSpec((past, HEADS * V_DIM), lambda b, t: (b, 0)))
        kargs.insert(0, k_c)
        vargs.insert(0, v_c)
    return pl.pallas_call(
        functools.partial(_attn_body, len(kargs)),
        out_shape=jax.ShapeDtypeStruct((rows, HEADS * V_DIM), BF16),
        grid=(rows // seq_len, q_tiles),
        in_specs=in_specs + ks + vs,
        out_specs=pl.BlockSpec((tq, HEADS * V_DIM), lambda b, t: (b * q_tiles + t, 0)),
        compiler_params=_params(2), name="attention",
    )(q, *kargs, *vargs)


def _s5_body(n_chunks, has_h0, want_fin, *refs):
    refs = list(refs)
    uf_ref = refs.pop(0)
    ub_ref = refs.pop(0) if n_chunks > 1 else uf_ref
    bm_ref, cm_ref, a_ref = refs[:3]
    refs = refs[3:]
    h0_ref = refs.pop(0) if has_h0 else None
    yf_ref, yb_ref = refs[:2]
    refs = refs[2:]
    fin_ref = refs.pop(0) if want_fin else None
    bu_ref, st_ref = refs
    c = pl.program_id(1)
    nrow = 2 * S5_QUARTERS

    @pl.when(c == 0)
    def _():
        st_ref[...] = h0_ref[...] if has_h0 else jnp.zeros(st_ref.shape, F32)

    for d, u_ref in ((0, uf_ref), (1, ub_ref)):
        for b in range(2):
            for q in range(S5_QUARTERS):
                r = b * S5_QUARTERS + q
                bu = _dot(u_ref[b, :, q * LANES:(q + 1) * LANES].astype(BF16), bm_ref[d, q])
                for j in range(2 * S5_SLABS):
                    bu_ref[d, j, pl.ds(r * S5_PITCH, S5_TC), :] = bu[:, j * LANES:(j + 1) * LANES]

    coef = [[[a_ref[d, p, :, j * LANES:(j + 1) * LANES] for j in range(S5_SLABS)] for p in range(2)]
            for d in range(2)]

    def step(i, carry):
        new = []
        for d, t in ((0, i), (1, S5_TC - 1 - i)):
            rows = pl.ds(t, nrow, stride=S5_PITCH)
            for j in range(S5_SLABS):
                sr, si = carry[(d * S5_SLABS + j) * 2], carry[(d * S5_SLABS + j) * 2 + 1]
                ar, ai = coef[d][0][j], coef[d][1][j]
                nr = ar * sr - ai * si + bu_ref[d, j, rows, :]
                ni = ar * si + ai * sr + bu_ref[d, S5_SLABS + j, rows, :]
                bu_ref[d, j, rows, :] = nr
                bu_ref[d, S5_SLABS + j, rows, :] = ni
                new += [nr, ni]
        return tuple(new)

    init = tuple(st_ref[d, p, :, j * LANES:(j + 1) * LANES]
                 for d in range(2) for j in range(S5_SLABS) for p in range(2))
    fin = lax.fori_loop(0, S5_TC, step, init, unroll=2)
    for d in range(2):
        for j in range(S5_SLABS):
            for p in range(2):
                st_ref[d, p, :, j * LANES:(j + 1) * LANES] = fin[(d * S5_SLABS + j) * 2 + p]

    for d, y_ref in ((0, yf_ref), (1, yb_ref)):
        for b in range(2):
            for q in range(S5_QUARTERS):
                r = b * S5_QUARTERS + q
                s = jnp.concatenate([bu_ref[d, j, pl.ds(r * S5_PITCH, S5_TC), :] for j in range(2 * S5_SLABS)],
                                    axis=-1).astype(BF16)
                y_ref[b, :, q * LANES:(q + 1) * LANES] = _dot(s, cm_ref[d, q])

    if want_fin:
        @pl.when(c == n_chunks - 1)
        def _():
            fin_ref[...] = st_ref[...]


def _s5(u, wts, seq_len, h0=None, want_fin=False):
    batch = u.shape[0] // seq_len
    n_chunks = seq_len // S5_TC
    u3 = u.reshape(batch, seq_len, S5_WIDTH)
    nrow = 2 * S5_QUARTERS
    blk = (2, S5_TC, S5_WIDTH)
    fwd = pl.BlockSpec(blk, lambda p, c: (p, c, 0))
    bwd = pl.BlockSpec(blk, lambda p, c: (p, n_chunks - 1 - c, 0))
    in_specs, args = [fwd], [u3]
    if n_chunks > 1:
        in_specs.append(bwd)
        args.append(u3)
    in_specs += [_const_spec((2, S5_QUARTERS, LANES, 2 * S5_QLANES), (0, 0, 0, 0)),
                 _const_spec((2, S5_QUARTERS, 2 * S5_QLANES, LANES), (0, 0, 0, 0)),
                 _const_spec((2, 2, nrow, S5_QLANES), (0, 0, 0, 0))]
    args += [wts["s5_b"], wts["s5_c"], wts["s5_a"]]
    state_blk = pl.BlockSpec((None, 2, 2, nrow, S5_QLANES), lambda p, c: (p, 0, 0, 0, 0))
    if h0 is not None:
        in_specs.append(state_blk)
        args.append(h0)
    y_shape = jax.ShapeDtypeStruct((batch, seq_len, S5_WIDTH), F32)
    out_shape, out_specs = [y_shape, y_shape], [fwd, bwd]
    if want_fin:
        out_shape.append(jax.ShapeDtypeStruct((batch // 2, 2, 2, nrow, S5_QLANES), F32))
        out_specs.append(state_blk)
    return pl.pallas_call(
        functools.partial(_s5_body, n_chunks, h0 is not None, want_fin),
        out_shape=out_shape, grid=(batch // 2, n_chunks), in_specs=in_specs, out_specs=out_specs,
        scratch_shapes=[pltpu.VMEM((2, 2 * S5_SLABS, nrow * S5_PITCH, LANES), F32),
                        pltpu.VMEM((2, 2, nrow, S5_QLANES), F32)],
        compiler_params=_params(2), name="s5",
    )(*args)


def _mixout_body(x_ref, attn_ref, u_ref, yf_ref, yb_ref, mods_ref, g_ref, dsk_ref, wglu_ref, bglu_ref, wout_ref,
                 wg_ref, wu_ref, wd_ref, o_ref):
    mods = mods_ref[...]
    g3 = g_ref[...]
    y = dsk_ref[...] * u_ref[...] + yf_ref[...] + yb_ref[...]
    z = _gelu_tanh(y)
    s5o = z * _sigmoid(_dot(z.astype(BF16), wglu_ref[...]) + bglu_ref[...])
    half = HEADS * V_DIM
    mix = _dot(attn_ref[...], wout_ref[0:half, :]) + _dot(s5o.astype(BF16), wout_ref[half:, :])
    x2 = x_ref[...] + mods[5:6] * mix
    o_ref[...] = _ffn(x2, mods[6:7], mods[7:8], mods[8:9], g3[2:3], wg_ref, wu_ref, wd_ref)


def _mixout(x, attn, u, yf, yb, mods, norm_g, wts, cond_of_tile):
    rows = x.shape[0]
    in_specs = [_row_spec(D_MODEL), _row_spec(HEADS * V_DIM), _row_spec(S5_WIDTH), _row_spec(S5_WIDTH),
                _row_spec(S5_WIDTH), _mods_spec(0, cond_of_tile), _normg_spec(0),
                _const_spec((1, S5_WIDTH), (0, 0)), _const_spec((S5_WIDTH, S5_WIDTH), (0, 0)),
                _const_spec((1, S5_WIDTH), (0, 0)), _const_spec((D_MODEL, D_MODEL), (0, 0))] + _ffn_specs(0, 1)
    return pl.pallas_call(
        _mixout_body,
        out_shape=jax.ShapeDtypeStruct((rows, D_MODEL), F32),
        grid=(rows // TM,), in_specs=in_specs, out_specs=_row_spec(D_MODEL),
        compiler_params=_params(1), name="mixout_ffn",
    )(x, attn, u, yf.reshape(rows, S5_WIDTH), yb.reshape(rows, S5_WIDTH), mods, norm_g, wts["s5_d"], wts["w_glu"],
      wts["b_glu"], wts["w_out"], wts["ffn_g"], wts["ffn_u"], wts["ffn_d"])


def _ffn_body(sub, x_ref, mods_ref, g_ref, wg_ref, wu_ref, wd_ref, o_ref):
    mods = mods_ref[...]
    g3 = g_ref[...]
    m = 6 if sub else 0
    n = 2 if sub else 0
    o_ref[...] = _ffn(x_ref[...], mods[m:m + 1], mods[m + 1:m + 2], mods[m + 2:m + 3], g3[n:n + 1],
                      wg_ref, wu_ref, wd_ref)


def _ffn_call(x, mods, norm_g, wts, cond_of_tile, layer, sub):
    rows = x.shape[0]
    return pl.pallas_call(
        functools.partial(_ffn_body, sub),
        out_shape=jax.ShapeDtypeStruct((rows, D_MODEL), F32),
        grid=(rows // TM,),
        in_specs=[_row_spec(D_MODEL), _mods_spec(layer, cond_of_tile), _normg_spec(layer)] + _ffn_specs(layer, sub),
        out_specs=_row_spec(D_MODEL),
        compiler_params=_params(1), name="ffn",
    )(x, mods, norm_g, wts["ffn_g"], wts["ffn_u"], wts["ffn_d"])


def _conv_body(seq_len, x_ref, xp_ref, xn_ref, mods_ref, g_ref, cwin_ref, cw_ref, cwout_ref,
               wg_ref, wu_ref, wd_ref, o_ref):
    i = pl.program_id(0)
    mods = mods_ref[...]
    g3 = g_ref[...]
    x = x_ref[...]
    n_ext = TM + 2 * SUBLANES
    xe = jnp.concatenate([xp_ref[...], x, xn_ref[...]], axis=0)
    hne = _modulate(xe, g3[1:2], mods[3:4], mods[4:5]).astype(BF16)
    pz = _dot(hne, cwin_ref[:, D_MODEL:3 * D_MODEL])
    z = pz[:, 0:D_MODEL] * pz[:, D_MODEL:2 * D_MODEL]
    gate_b = _dot(hne[SUBLANES:SUBLANES + TM], cwin_ref[:, 0:D_MODEL])
    main = slice(SUBLANES, SUBLANES + TM)
    pos = (i * TM + lax.broadcasted_iota(jnp.int32, (TM, 1), 0)) % seq_len
    z_prev = jnp.where(pos == 0, 0.0, pltpu.roll(z, 1, 0)[main])
    z_next = jnp.where(pos == seq_len - 1, 0.0, pltpu.roll(z, n_ext - 1, 0)[main])
    cw = cw_ref[...]
    zc = z_prev * cw[0:1] + z[main] * cw[1:2] + z_next * cw[2:3]
    mix = _dot((gate_b * zc).astype(BF16), cwout_ref[...])
    x2 = x + mods[5:6] * mix
    o_ref[...] = _ffn(x2, mods[6:7], mods[7:8], mods[8:9], g3[2:3], wg_ref, wu_ref, wd_ref)


def _conv_mixer(x, mods, norm_g, wts, cond_of_tile, seq_len):
    rows = x.shape[0]
    per = TM // SUBLANES
    last = rows // SUBLANES - 1
    halo = (SUBLANES, D_MODEL)
    in_specs = [_row_spec(D_MODEL),
                pl.BlockSpec(halo, lambda i: (jnp.maximum(i * per - 1, 0), 0)),
                pl.BlockSpec(halo, lambda i: (jnp.minimum((i + 1) * per, last), 0)),
                _mods_spec(1, cond_of_tile), _normg_spec(1),
                _const_spec((D_MODEL, 3 * D_MODEL), (0, 0)), _const_spec((CONV_K, D_MODEL), (0, 0)),
                _const_spec((D_MODEL, D_MODEL), (0, 0))] + _ffn_specs(1, 1)
    return pl.pallas_call(
        functools.partial(_conv_body, seq_len),
        out_shape=jax.ShapeDtypeStruct((rows, D_MODEL), F32),
        grid=(rows // TM,), in_specs=in_specs, out_specs=_row_spec(D_MODEL),
        compiler_params=_params(1), name="conv_ffn",
    )(x, x, x, mods, norm_g, wts["conv_w_in"], wts["conv_w"], wts["conv_w_out"],
      wts["ffn_g"], wts["ffn_u"], wts["ffn_d"])


def _rope_tables(seq_len):
    t = np.arange(seq_len)
    quarter = ROPE // 4
    inv_freq = ROPE_BASE ** (-np.arange(quarter, dtype=np.float64) / quarter)
    cos = np.ones((seq_len, LANES))
    s_lo = np.zeros((seq_len, LANES))
    s_hi = np.zeros((seq_len, LANES))
    for base, pos in ((NOPE, t // GRID_W), (NOPE + ROPE // 2, t % GRID_W)):
        ang = pos[:, None].astype(np.float64) * inv_freq[None, :]
        cos[:, base:base + quarter] = np.cos(ang)
        cos[:, base + quarter:base + 2 * quarter] = np.cos(ang)
        s_lo[:, base:base + quarter] = -np.sin(ang)
        s_hi[:, base + quarter:base + 2 * quarter] = np.sin(ang)
    return tuple(jnp.asarray(a, F32) for a in (cos, s_lo, s_hi))


def _s5_params(lam_re, lam_im, log_dt, b_re, b_im, c_re, c_im):
    dt = jnp.exp(log_dt)[..., None]
    lr = jnp.minimum(lam_re, LAMBDA_RE_MAX)
    li = lam_im
    mag = jnp.exp(lr * dt)
    ang = li * dt
    ab_re = mag * jnp.cos(ang)
    ab_im = mag * jnp.sin(ang)
    den = lr * lr + li * li
    nr = ab_re - 1.0
    ni = ab_im
    co_re = (nr * lr + ni * li) / den
    co_im = (ni * lr - nr * li) / den
    bb_re = co_re[..., None] * b_re - co_im[..., None] * b_im
    bb_im = co_re[..., None] * b_im + co_im[..., None] * b_re
    per_q = S5_GROUPS // S5_QUARTERS
    eye = jnp.eye(per_q, dtype=F32)

    def in_mat(bb):
        bb = bb.reshape(2, S5_QUARTERS, per_q, S5_STATE, S5_GROUP)
        return jnp.einsum("dqgpi,gh->dqgihp", bb, eye).reshape(2, S5_QUARTERS, LANES, S5_QLANES)

    def out_mat(cc):
        cc = cc.reshape(2, S5_QUARTERS, per_q, S5_GROUP, S5_STATE)
        return jnp.einsum("dqgip,gh->dqgphi", cc, eye).reshape(2, S5_QUARTERS, S5_QLANES, LANES)

    bm = jnp.concatenate([in_mat(bb_re), in_mat(bb_im)], axis=-1).astype(BF16)
    cm = jnp.concatenate([out_mat(c_re), -out_mat(c_im)], axis=-2).astype(BF16)

    def rows(a):
        a = a.reshape(2, 1, S5_QUARTERS, S5_QLANES)
        return jnp.broadcast_to(a, (2, 2, S5_QUARTERS, S5_QLANES)).reshape(2, 2 * S5_QUARTERS, S5_QLANES)

    return bm, cm, jnp.stack([rows(ab_re), rows(ab_im)], axis=1)


def _state_rows(s):
    b = s.shape[0]
    s = s.reshape(b // 2, 2, 2, S5_QUARTERS, S5_QLANES)
    return s.transpose(0, 2, 1, 3, 4).reshape(b // 2, 2, 2 * S5_QUARTERS, S5_QLANES)


def _state_unrows(s, batch):
    s = s.reshape(batch // 2, 2, 2, S5_QUARTERS, S5_QLANES).transpose(0, 2, 1, 3, 4)
    return s.reshape(batch, 1, 2, S5_GROUPS, S5_STATE)


def kernel(x_prompt, x_sample, cache_ckv, cache_krope, state_ssm_re, state_ssm_im, c, c_ctx, w_ada, b_ada, norm_g, ffn_w_gate, ffn_w_up, ffn_w_down, ab_w_in, mla_g_q_lat, mla_g_kv_lat, mla_w_uq, mla_w_ukv, mla_g_qnorm, mla_g_knorm, s5_lam_re, s5_lam_im, s5_log_dt, s5_b_re, s5_b_im, s5_c_re, s5_c_im, s5_d, s5_w_glu, s5_b_glu, ab_w_out, conv_w_in, conv_w, conv_w_out):
    batch, seq, _ = x_prompt.shape
    dec_batch, dec_seq, _ = x_sample.shape
    past = cache_ckv.shape[2]

    w_in = ab_w_in[0]
    kr_cols = w_in[:, Q_LORA + KV_LORA:Q_LORA + KV_LORA + ROPE]
    zeros = jnp.zeros_like(kr_cols)
    w_uq = jnp.pad(mla_w_uq[0].reshape(Q_LORA, HEADS, QK_DIM), ((0, 0), (0, 0), (0, HEAD_PAD - QK_DIM)))
    w_ukv = mla_w_ukv[0].reshape(KV_LORA, HEADS, NOPE + V_DIM)
    w_uk = jnp.pad(w_ukv[:, :, :NOPE], ((0, 0), (0, 0), (0, HEAD_PAD - NOPE)))
    bm, cm, a_rows = _s5_params(s5_lam_re[0], s5_lam_im[0], s5_log_dt[0], s5_b_re[0], s5_b_im[0],
                                s5_c_re[0], s5_c_im[0])
    wts = {
        "ffn_g": ffn_w_gate.astype(BF16), "ffn_u": ffn_w_up.astype(BF16), "ffn_d": ffn_w_down.astype(BF16),
        "w_in": jnp.concatenate([w_in[:, :Q_LORA + KV_LORA], w_in[:, Q_LORA + KV_LORA + ROPE:],
                                 kr_cols, zeros, kr_cols, zeros], axis=1).astype(BF16),
        "g_q_lat": mla_g_q_lat, "g_kv_lat": mla_g_kv_lat,
        "w_uq": w_uq.reshape(Q_LORA, HEADS * HEAD_PAD).astype(BF16),
        "w_ukv": jnp.concatenate([w_uk.reshape(KV_LORA, HEADS * HEAD_PAD),
                                  w_ukv[:, :, NOPE:].reshape(KV_LORA, HEADS * V_DIM)], axis=1).astype(BF16),
        "g_qn": jnp.pad(mla_g_qnorm, ((0, 0), (0, HEAD_PAD - QK_DIM))),
        "g_kn": jnp.pad(mla_g_knorm, ((0, 0), (0, HEAD_PAD - QK_DIM))),
        "rope": _rope_tables(dec_seq),
        "s5_b": bm, "s5_c": cm, "s5_a": a_rows,
        "s5_d": s5_d, "w_glu": s5_w_glu[0].astype(BF16), "b_glu": s5_b_glu, "w_out": ab_w_out[0].astype(BF16),
        "conv_w_in": conv_w_in[0].astype(BF16), "conv_w": conv_w[0].T, "conv_w_out": conv_w_out[0].astype(BF16),
    }

    cond = jnp.zeros((SUBLANES, D_MODEL), F32).at[0].set(c_ctx).at[1:1 + dec_batch].set(c)
    mods = _ada(cond, w_ada, b_ada).reshape(w_ada.shape[0], SUBLANES, N_MOD, D_MODEL)

    def trunk(x, seq_len, cond_of_tile, latent):
        rows = x.shape[0]
        res = _mixin(x, mods, norm_g, wts, cond_of_tile, seq_len, rotate=latent, want_cache=not latent)
        x1, q, k, v, u = res[:5]
        if latent:
            kr_pad = jnp.pad(cache_krope[:, 0].reshape(dec_batch * past, ROPE), ((0, 0), (NOPE, LANES - QK_DIM)))
            ctx = _ctx_kv(cache_ckv[:, 0].reshape(dec_batch * past, KV_LORA), kr_pad, wts)
            h0 = jnp.stack([_state_rows(state_ssm_re[:, 0]), _state_rows(state_ssm_im[:, 0])], axis=2)
            attn = _attention(q, k, v, seq_len, 256, ctx)
            yf, yb = _s5(u, wts, seq_len, h0=h0)
            extra = ()
        else:
            attn = _attention(q, k, v, seq_len, seq_len)
            yf, yb, fin = _s5(u, wts, seq_len, want_fin=True)
            extra = (res[5], res[6], fin)
        x2 = _mixout(x1, attn, u, yf, yb, mods, norm_g, wts, cond_of_tile)
        x3 = _ffn_call(x2, mods, norm_g, wts, cond_of_tile, 1, 0)
        return _conv_mixer(x3, mods, norm_g, wts, cond_of_tile, seq_len), extra

    tiles_per_sample = dec_seq // TM
    y_p, (ckv_p, kr_p, fin) = trunk(x_prompt.reshape(batch * seq, D_MODEL), seq, lambda i: 0, False)
    y_s, _ = trunk(x_sample.reshape(dec_batch * dec_seq, D_MODEL), dec_seq, lambda i: 1 + i // tiles_per_sample, True)
    return (y_p.reshape(batch, seq, D_MODEL), y_s.reshape(dec_batch, dec_seq, D_MODEL),
            ckv_p.reshape(batch, 1, seq, KV_LORA), kr_p.reshape(batch, 1, seq, ROPE),
            _state_unrows(fin[:, :, 0], batch), _state_unrows(fin[:, :, 1], batch))
```

```python
import functools
import math

import jax
import jax.numpy as jnp
import numpy as np
from jax import lax
from jax.experimental import pallas as pl
from jax.experimental.pallas import tpu as pltpu

F32 = jnp.float32
BF16 = jnp.bfloat16

LANES = 128
SUBLANES = 8
VMEM_LIMIT_BYTES = 60 * 1024 * 1024

D_MODEL = 1024
D_FF = 2816
N_MOD = 9
EPS = 1e-6
HEADS = 8
Q_LORA = 384
KV_LORA = 256
NOPE = 64
ROPE = 32
V_DIM = 64
QK_DIM = NOPE + ROPE
HEAD_PAD = LANES
ROPE_BASE = 10000.0
GRID_W = 64
S5_WIDTH = 512
S5_GROUP = 16
S5_GROUPS = 32
S5_STATE = 64
S5_LANES = S5_GROUPS * S5_STATE
S5_QUARTERS = 4
S5_QLANES = S5_LANES // S5_QUARTERS
S5_SLABS = S5_QLANES // LANES
S5_TC = 256
S5_PITCH = S5_TC + 4
LAMBDA_RE_MAX = -1e-4
CONV_K = 3

ATT_ROWS = 2 * SUBLANES
TM = 512
FF_CHUNKS = tuple((c, min(512, D_FF - c)) for c in range(0, D_FF, 512))

O_CKV = Q_LORA
O_U = Q_LORA + KV_LORA
O_KR = O_U + S5_WIDTH


def _dot(a, b):
    return jnp.dot(a, b, preferred_element_type=F32)


def _sigmoid(x):
    return 1.0 / (1.0 + jnp.exp(-x))


def _rms(x, g, n):
    ms = jnp.sum(x * x, axis=-1, keepdims=True) * (1.0 / n)
    return x * lax.rsqrt(ms + EPS) * g


def _modulate(x, g, shift, scale):
    return _rms(x, g, D_MODEL) * (1.0 + scale) + shift


def _ffn(x, shift, scale, gate, g, wg_ref, wu_ref, wd_ref):
    h = _modulate(x, g, shift, scale).astype(BF16)
    acc = jnp.zeros(x.shape, F32)
    for c0, cs in FF_CHUNKS:
        gt = _dot(h, wg_ref[:, c0:c0 + cs])
        up = _dot(h, wu_ref[:, c0:c0 + cs])
        act = ((gt * _sigmoid(gt)) * up).astype(BF16)
        acc = acc + _dot(act, wd_ref[c0:c0 + cs, :])
    return x + (0.5 * gate) * acc


def _head_inv_rms(xh):
    return lax.rsqrt(jnp.sum(xh * xh, axis=-1, keepdims=True) * (1.0 / QK_DIM) + EPS)


def _head_norm(xh, g):
    return xh * _head_inv_rms(xh) * g


def _gelu_tanh(x):
    return x * (0.5 * (1.0 + jnp.tanh(math.sqrt(2.0 / math.pi) * (x + 0.044715 * (x * x * x)))))


def _const_spec(shape, index):
    return pl.BlockSpec(shape, lambda *_: index, pipeline_mode=pl.Buffered(1))


def _params(n_axes):
    return pltpu.CompilerParams(dimension_semantics=("arbitrary",) * n_axes, vmem_limit_bytes=VMEM_LIMIT_BYTES)


def _ffn_specs(layer, sub):
    w = _const_spec((None, None, D_MODEL, D_FF), (layer, sub, 0, 0))
    return [w, w, _const_spec((None, None, D_FF, D_MODEL), (layer, sub, 0, 0))]


def _mods_spec(layer, cond_of_tile):
    return pl.BlockSpec((None, None, N_MOD, D_MODEL), lambda i: (layer, cond_of_tile(i), 0, 0))


def _normg_spec(layer):
    return _const_spec((None, 3, D_MODEL), (layer, 0, 0))


def _row_spec(width, rows=TM):
    return pl.BlockSpec((rows, width), lambda i: (i, 0))


def _ada_body(c_ref, w_ref, b_ref, o_ref):
    c = c_ref[...]
    s = (c * _sigmoid(c)).astype(BF16)
    o_ref[...] = _dot(s, w_ref[...].astype(BF16)) + b_ref[...]


def _ada(cond, w_ada, b_ada):
    depth, _, n = w_ada.shape
    tn = 1152
    return pl.pallas_call(
        _ada_body,
        out_shape=jax.ShapeDtypeStruct((depth, SUBLANES, n), F32),
        grid=(depth, n // tn),
        in_specs=[pl.BlockSpec((SUBLANES, D_MODEL), lambda l, j: (0, 0)),
                  pl.BlockSpec((None, D_MODEL, tn), lambda l, j: (l, 0, j)),
                  pl.BlockSpec((None, 1, tn), lambda l, j: (l, 0, j))],
        out_specs=pl.BlockSpec((None, SUBLANES, tn), lambda l, j: (l, 0, j)),
        compiler_params=_params(2),
        name="ada",
    )(cond, w_ada, b_ada.reshape(depth, 1, n))


def _mixin_body(rotate, want_cache, *refs):
    (x_ref, mods_ref, g_ref, wg_ref, wu_ref, wd_ref, win_ref, gql_ref, gkv_ref, wuq_ref, wukv_ref,
     gq_ref, gk_ref) = refs[:13]
    refs = refs[13:]
    if rotate:
        qc_ref, qs_ref, kc_ref, ks_ref = refs[:4]
        refs = refs[4:]
    x1_ref, q_ref, k_ref, v_ref, u_ref = refs[:5]
    mods = mods_ref[...]
    g3 = g_ref[...]
    x1 = _ffn(x_ref[...], mods[0:1], mods[1:2], mods[2:3], g3[0:1], wg_ref, wu_ref, wd_ref)
    x1_ref[...] = x1
    hn = _modulate(x1, g3[1:2], mods[3:4], mods[4:5]).astype(BF16)
    proj = _dot(hn, win_ref[...])
    u_ref[...] = proj[:, O_U:O_KR]
    ckv = _rms(proj[:, O_CKV:O_U], gkv_ref[...], KV_LORA)
    krg = proj[:, O_KR:O_KR + LANES]
    if want_cache:
        ckv_ref, kr_ref = refs[5:7]
        ckv_ref[...] = ckv
        kr_ref[...] = krg[:, 0:ROPE]
    qn = _rms(proj[:, 0:Q_LORA], gql_ref[...], Q_LORA).astype(BF16)
    qraw = _dot(qn, wuq_ref[...])
    kv = _dot(ckv.astype(BF16), wukv_ref[...])
    v_ref[...] = kv[:, HEADS * HEAD_PAD:].astype(BF16)
    lane = lax.broadcasted_iota(jnp.int32, (1, LANES), 1)
    kr_only = jnp.where((lane >= NOPE) & (lane < QK_DIM), krg, 0.0)
    gq = gq_ref[...]
    gk = gk_ref[...]
    if rotate:
        kr_partner = proj[:, O_KR + LANES:O_KR + 2 * LANES]
        k_rot = kr_only * kc_ref[...] + kr_partner * ks_ref[...]
    for h in range(HEADS):
        sl = slice(h * HEAD_PAD, (h + 1) * HEAD_PAD)
        if rotate:
            psl = slice((HEADS + h) * HEAD_PAD, (HEADS + h + 1) * HEAD_PAD)
            qa = qraw[:, sl]
            qh = (qa * qc_ref[...] + qraw[:, psl] * qs_ref[...]) * _head_inv_rms(qa)
            kh = (kv[:, sl] * gk + k_rot) * _head_inv_rms(kv[:, sl] + kr_only)
        else:
            qh = _head_norm(qraw[:, sl], gq)
            kh = _head_norm(kv[:, sl] + kr_only, gk)
        q_ref[:, sl] = qh.astype(BF16)
        k_ref[:, sl] = kh.astype(BF16)


def _mixin(x, mods, norm_g, wts, cond_of_tile, seq_len, rotate, want_cache):
    rows = x.shape[0]
    tiles_per_seq = seq_len // TM if seq_len >= TM else 1
    w_in, w_uq = (wts["w_in_rot"], wts["w_uq_rot"]) if rotate else (wts["w_in"], wts["w_uq"])
    in_specs = [_row_spec(D_MODEL), _mods_spec(0, cond_of_tile), _normg_spec(0)] + _ffn_specs(0, 0) + [
        _const_spec(w_in.shape, (0, 0)),
        _const_spec((1, Q_LORA), (0, 0)),
        _const_spec((1, KV_LORA), (0, 0)),
        _const_spec(w_uq.shape, (0, 0)),
        _const_spec((KV_LORA, HEADS * HEAD_PAD + HEADS * V_DIM), (0, 0)),
        _const_spec((1, HEAD_PAD), (0, 0)),
        _const_spec((1, HEAD_PAD), (0, 0)),
    ]
    args = [x, mods, norm_g, wts["ffn_g"], wts["ffn_u"], wts["ffn_d"], w_in, wts["g_q_lat"], wts["g_kv_lat"],
            w_uq, wts["w_ukv"], wts["g_qn"], wts["g_kn"]]
    if rotate:
        in_specs += [pl.BlockSpec((TM, LANES), lambda i: (i % tiles_per_seq, 0))] * 4
        args += list(wts["rope"])
    out_shape = [jax.ShapeDtypeStruct((rows, D_MODEL), F32),
                 jax.ShapeDtypeStruct((rows, HEADS * HEAD_PAD), BF16),
                 jax.ShapeDtypeStruct((rows, HEADS * HEAD_PAD), BF16),
                 jax.ShapeDtypeStruct((rows, HEADS * V_DIM), BF16),
                 jax.ShapeDtypeStruct((rows, S5_WIDTH), F32)]
    out_specs = [_row_spec(D_MODEL), _row_spec(HEADS * HEAD_PAD), _row_spec(HEADS * HEAD_PAD),
                 _row_spec(HEADS * V_DIM), _row_spec(S5_WIDTH)]
    if want_cache:
        out_shape += [jax.ShapeDtypeStruct((rows, KV_LORA), F32), jax.ShapeDtypeStruct((rows, ROPE), F32)]
        out_specs += [_row_spec(KV_LORA), _row_spec(ROPE)]
    return pl.pallas_call(
        functools.partial(_mixin_body, rotate, want_cache),
        out_shape=out_shape, grid=(rows // TM,), in_specs=in_specs, out_specs=out_specs,
        compiler_params=_params(1), name="ffn_mixin",
    )(*args)


def _ctx_body(ckv_ref, krp_ref, wukv_ref, gk_ref, k_ref, v_ref):
    kv = _dot(ckv_ref[...].astype(BF16), wukv_ref[...])
    v_ref[...] = kv[:, HEADS * HEAD_PAD:].astype(BF16)
    gk = gk_ref[...]
    krp = krp_ref[...]
    for h in range(HEADS):
        sl = slice(h * HEAD_PAD, (h + 1) * HEAD_PAD)
        k_ref[:, sl] = _head_norm(kv[:, sl] + krp, gk).astype(BF16)


def _ctx_kv(ckv, kr_padded, wts):
    rows = ckv.shape[0]
    return pl.pallas_call(
        _ctx_body,
        out_shape=[jax.ShapeDtypeStruct((rows, HEADS * HEAD_PAD), BF16),
                   jax.ShapeDtypeStruct((rows, HEADS * V_DIM), BF16)],
        grid=(rows // TM,),
        in_specs=[_row_spec(KV_LORA), _row_spec(LANES),
                  _const_spec((KV_LORA, HEADS * HEAD_PAD + HEADS * V_DIM), (0, 0)),
                  _const_spec((1, HEAD_PAD), (0, 0))],
        out_specs=[_row_spec(HEADS * HEAD_PAD), _row_spec(HEADS * V_DIM)],
        compiler_params=_params(1), name="ctx_kv",
    )(ckv, kr_padded, wts["w_ukv"], wts["g_kn"])


def _attn_body(n_kv, *refs):
    q_ref = refs[0]
    k_refs = refs[1:1 + n_kv]
    v_refs = refs[1 + n_kv:1 + 2 * n_kv]
    o_ref, s_ref, p_ref, r_ref = refs[1 + 2 * n_kv:]
    tq = q_ref.shape[0]
    c = (QK_DIM ** -0.5) * math.log2(math.e)
    lane = lax.broadcasted_iota(jnp.int32, (1, LANES), 1)
    spans = []
    for k_ref in k_refs:
        spans.append(slice(spans[-1].stop if spans else 0, (spans[-1].stop if spans else 0) + k_ref.shape[0]))
    for pair in range(HEADS // 2):
        vsl = slice(pair * LANES, (pair + 1) * LANES)
        outs = []
        for h in (2 * pair, 2 * pair + 1):
            slot = h % 2
            sl = slice(h * HEAD_PAD, (h + 1) * HEAD_PAD)
            q = q_ref[:, sl]
            for k_ref, span in zip(k_refs, spans):
                s_ref[slot, :, span] = lax.dot_general(q, k_ref[:, sl], (((1,), (1,)), ((), ())),
                                                       preferred_element_type=F32)
            for r0 in range(0, tq, ATT_ROWS):
                rows = slice(r0, r0 + ATT_ROWS)
                s = s_ref[slot, rows, :]
                e = jnp.exp2((s - jnp.max(s, axis=-1, keepdims=True)) * c)
                r_ref[slot, rows, :] = jnp.broadcast_to(1.0 / jnp.sum(e, axis=-1, keepdims=True), (ATT_ROWS, LANES))
                p_ref[slot, rows, :] = e.astype(BF16)
            o = functools.reduce(jnp.add, [_dot(p_ref[slot, :, span], v_ref[:, vsl])
                                           for v_ref, span in zip(v_refs, spans)])
            outs.append(o * r_ref[slot])
        o_ref[:, vsl] = jnp.where(lane < V_DIM, outs[0], outs[1]).astype(BF16)


def _attention(q, k, v, seq_len, tq, ctx=None):
    rows = q.shape[0]
    q_tiles = seq_len // tq
    kv_spec = lambda w: pl.BlockSpec((seq_len, w), lambda b, t: (b, 0))
    in_specs = [pl.BlockSpec((tq, HEADS * HEAD_PAD), lambda b, t: (b * q_tiles + t, 0))]
    ks, vs = [kv_spec(HEADS * HEAD_PAD)], [kv_spec(HEADS * V_DIM)]
    kargs, vargs = [k], [v]
    if ctx is not None:
        k_c, v_c = ctx
        past = k_c.shape[0] // (rows // seq_len)
        ks.insert(0, pl.BlockSpec((past, HEADS * HEAD_PAD), lambda b, t: (b, 0)))
        vs.insert(0, pl.BlockSpec((past, HEADS * V_DIM), lambda b, t: (b, 0)))
        kargs.insert(0, k_c)
        vargs.insert(0, v_c)
    n_keys = sum(a.shape[0] for a in kargs) // (rows // seq_len)
    return pl.pallas_call(
        functools.partial(_attn_body, len(kargs)),
        out_shape=jax.ShapeDtypeStruct((rows, HEADS * V_DIM), BF16),
        grid=(rows // seq_len, q_tiles),
        in_specs=in_specs + ks + vs,
        out_specs=pl.BlockSpec((tq, HEADS * V_DIM), lambda b, t: (b * q_tiles + t, 0)),
        scratch_shapes=[pltpu.VMEM((2, tq, n_keys), F32), pltpu.VMEM((2, tq, n_keys), BF16),
                        pltpu.VMEM((2, tq, LANES), F32)],
        compiler_params=_params(2), name="attention",
    )(q, *kargs, *vargs)


def _s5_body(n_chunks, has_h0, want_fin, *refs):
    refs = list(refs)
    uf_ref = refs.pop(0)
    ub_ref = refs.pop(0) if n_chunks > 1 else uf_ref
    bm_ref, cm_ref, a_ref = refs[:3]
    refs = refs[3:]
    h0_ref = refs.pop(0) if has_h0 else None
    yf_ref, yb_ref = refs[:2]
    refs = refs[2:]
    fin_ref = refs.pop(0) if want_fin else None
    bu_ref, st_ref = refs
    c = pl.program_id(1)
    nrow = 2 * S5_QUARTERS

    @pl.when(c == 0)
    def _():
        st_ref[...] = h0_ref[...] if has_h0 else jnp.zeros(st_ref.shape, F32)

    for d, u_ref in ((0, uf_ref), (1, ub_ref)):
        for b in range(2):
            for q in range(S5_QUARTERS):
                r = b * S5_QUARTERS + q
                bu = _dot(u_ref[b, :, q * LANES:(q + 1) * LANES].astype(BF16), bm_ref[d, q])
                for j in range(2 * S5_SLABS):
                    bu_ref[d, j, pl.ds(r * S5_PITCH, S5_TC), :] = bu[:, j * LANES:(j + 1) * LANES]

    coef = [[[a_ref[d, p, :, j * LANES:(j + 1) * LANES] for j in range(S5_SLABS)] for p in range(2)]
            for d in range(2)]

    def step(i, carry):
        new = []
        for d, t in ((0, i), (1, S5_TC - 1 - i)):
            rows = pl.ds(t, nrow, stride=S5_PITCH)
            for j in range(S5_SLABS):
                sr, si = carry[(d * S5_SLABS + j) * 2], carry[(d * S5_SLABS + j) * 2 + 1]
                ar, ai = coef[d][0][j], coef[d][1][j]
                nr = ar * sr - ai * si + bu_ref[d, j, rows, :]
                ni = ar * si + ai * sr + bu_ref[d, S5_SLABS + j, rows, :]
                bu_ref[d, j, rows, :] = nr
                bu_ref[d, S5_SLABS + j, rows, :] = ni
                new += [nr, ni]
        return tuple(new)

    init = tuple(st_ref[d, p, :, j * LANES:(j + 1) * LANES]
                 for d in range(2) for j in range(S5_SLABS) for p in range(2))
    fin = lax.fori_loop(0, S5_TC, step, init, unroll=2)
    for d in range(2):
        for j in range(S5_SLABS):
            for p in range(2):
                st_ref[d, p, :, j * LANES:(j + 1) * LANES] = fin[(d * S5_SLABS + j) * 2 + p]

    for d, y_ref in ((0, yf_ref), (1, yb_ref)):
        for b in range(2):
            for q in range(S5_QUARTERS):
                r = b * S5_QUARTERS + q
                s = jnp.concatenate([bu_ref[d, j, pl.ds(r * S5_PITCH, S5_TC), :] for j in range(2 * S5_SLABS)],
                                    axis=-1).astype(BF16)
                y_ref[b, :, q * LANES:(q + 1) * LANES] = _dot(s, cm_ref[d, q])

    if want_fin:
        @pl.when(c == n_chunks - 1)
        def _():
            fin_ref[...] = st_ref[...]


def _s5(u, wts, seq_len, h0=None, want_fin=False):
    batch = u.shape[0] // seq_len
    n_chunks = seq_len // S5_TC
    u3 = u.reshape(batch, seq_len, S5_WIDTH)
    nrow = 2 * S5_QUARTERS
    blk = (2, S5_TC, S5_WIDTH)
    fwd = pl.BlockSpec(blk, lambda p, c: (p, c, 0))
    bwd = pl.BlockSpec(blk, lambda p, c: (p, n_chunks - 1 - c, 0))
    in_specs, args = [fwd], [u3]
    if n_chunks > 1:
        in_specs.append(bwd)
        args.append(u3)
    in_specs += [_const_spec((2, S5_QUARTERS, LANES, 2 * S5_QLANES), (0, 0, 0, 0)),
                 _const_spec((2, S5_QUARTERS, 2 * S5_QLANES, LANES), (0, 0, 0, 0)),
                 _const_spec((2, 2, nrow, S5_QLANES), (0, 0, 0, 0))]
    args += [wts["s5_b"], wts["s5_c"], wts["s5_a"]]
    state_blk = pl.BlockSpec((None, 2, 2, nrow, S5_QLANES), lambda p, c: (p, 0, 0, 0, 0))
    if h0 is not None:
        in_specs.append(state_blk)
        args.append(h0)
    y_shape = jax.ShapeDtypeStruct((batch, seq_len, S5_WIDTH), F32)
    out_shape, out_specs = [y_shape, y_shape], [fwd, bwd]
    if want_fin:
        out_shape.append(jax.ShapeDtypeStruct((batch // 2, 2, 2, nrow, S5_QLANES), F32))
        out_specs.append(state_blk)
    return pl.pallas_call(
        functools.partial(_s5_body, n_chunks, h0 is not None, want_fin),
        out_shape=out_shape, grid=(batch // 2, n_chunks), in_specs=in_specs, out_specs=out_specs,
        scratch_shapes=[pltpu.VMEM((2, 2 * S5_SLABS, nrow * S5_PITCH, LANES), F32),
                        pltpu.VMEM((2, 2, nrow, S5_QLANES), F32)],
        compiler_params=_params(2), name="s5",
    )(*args)


def _mixout_body(x_ref, attn_ref, u_ref, yf_ref, yb_ref, mods_ref, g_ref, dsk_ref, wglu_ref, bglu_ref, wout_ref,
                 wg_ref, wu_ref, wd_ref, o_ref):
    mods = mods_ref[...]
    g3 = g_ref[...]
    y = dsk_ref[...] * u_ref[...] + yf_ref[...] + yb_ref[...]
    z = _gelu_tanh(y)
    s5o = z * _sigmoid(_dot(z.astype(BF16), wglu_ref[...]) + bglu_ref[...])
    half = HEADS * V_DIM
    mix = _dot(attn_ref[...], wout_ref[0:half, :]) + _dot(s5o.astype(BF16), wout_ref[half:, :])
    x2 = x_ref[...] + mods[5:6] * mix
    o_ref[...] = _ffn(x2, mods[6:7], mods[7:8], mods[8:9], g3[2:3], wg_ref, wu_ref, wd_ref)


def _mixout(x, attn, u, yf, yb, mods, norm_g, wts, cond_of_tile):
    rows = x.shape[0]
    in_specs = [_row_spec(D_MODEL), _row_spec(HEADS * V_DIM), _row_spec(S5_WIDTH), _row_spec(S5_WIDTH),
                _row_spec(S5_WIDTH), _mods_spec(0, cond_of_tile), _normg_spec(0),
                _const_spec((1, S5_WIDTH), (0, 0)), _const_spec((S5_WIDTH, S5_WIDTH), (0, 0)),
                _const_spec((1, S5_WIDTH), (0, 0)), _const_spec((D_MODEL, D_MODEL), (0, 0))] + _ffn_specs(0, 1)
    return pl.pallas_call(
        _mixout_body,
        out_shape=jax.ShapeDtypeStruct((rows, D_MODEL), F32),
        grid=(rows // TM,), in_specs=in_specs, out_specs=_row_spec(D_MODEL),
        compiler_params=_params(1), name="mixout_ffn",
    )(x, attn, u, yf.reshape(rows, S5_WIDTH), yb.reshape(rows, S5_WIDTH), mods, norm_g, wts["s5_d"], wts["w_glu"],
      wts["b_glu"], wts["w_out"], wts["ffn_g"], wts["ffn_u"], wts["ffn_d"])


def _ffn_body(sub, x_ref, mods_ref, g_ref, wg_ref, wu_ref, wd_ref, o_ref):
    mods = mods_ref[...]
    g3 = g_ref[...]
    m = 6 if sub else 0
    n = 2 if sub else 0
    o_ref[...] = _ffn(x_ref[...], mods[m:m + 1], mods[m + 1:m + 2], mods[m + 2:m + 3], g3[n:n + 1],
                      wg_ref, wu_ref, wd_ref)


def _ffn_call(x, mods, norm_g, wts, cond_of_tile, layer, sub):
    rows = x.shape[0]
    return pl.pallas_call(
        functools.partial(_ffn_body, sub),
        out_shape=jax.ShapeDtypeStruct((rows, D_MODEL), F32),
        grid=(rows // TM,),
        in_specs=[_row_spec(D_MODEL), _mods_spec(layer, cond_of_tile), _normg_spec(layer)] + _ffn_specs(layer, sub),
        out_specs=_row_spec(D_MODEL),
        compiler_params=_params(1), name="ffn",
    )(x, mods, norm_g, wts["ffn_g"], wts["ffn_u"], wts["ffn_d"])


def _conv_body(seq_len, x_ref, xp_ref, xn_ref, mods_ref, g_ref, cwin_ref, cw_ref, cwout_ref,
               wg_ref, wu_ref, wd_ref, o_ref):
    i = pl.program_id(0)
    mods = mods_ref[...]
    g3 = g_ref[...]
    x = x_ref[...]
    n_ext = TM + 2 * SUBLANES
    xe = jnp.concatenate([xp_ref[...], x, xn_ref[...]], axis=0)
    hne = _modulate(xe, g3[1:2], mods[3:4], mods[4:5]).astype(BF16)
    pz = _dot(hne, cwin_ref[:, D_MODEL:3 * D_MODEL])
    z = pz[:, 0:D_MODEL] * pz[:, D_MODEL:2 * D_MODEL]
    gate_b = _dot(hne[SUBLANES:SUBLANES + TM], cwin_ref[:, 0:D_MODEL])
    main = slice(SUBLANES, SUBLANES + TM)
    pos = (i * TM + lax.broadcasted_iota(jnp.int32, (TM, 1), 0)) % seq_len
    z_prev = jnp.where(pos == 0, 0.0, pltpu.roll(z, 1, 0)[main])
    z_next = jnp.where(pos == seq_len - 1, 0.0, pltpu.roll(z, n_ext - 1, 0)[main])
    cw = cw_ref[...]
    zc = z_prev * cw[0:1] + z[main] * cw[1:2] + z_next * cw[2:3]
    mix = _dot((gate_b * zc).astype(BF16), cwout_ref[...])
    x2 = x + mods[5:6] * mix
    o_ref[...] = _ffn(x2, mods[6:7], mods[7:8], mods[8:9], g3[2:3], wg_ref, wu_ref, wd_ref)


def _conv_mixer(x, mods, norm_g, wts, cond_of_tile, seq_len):
    rows = x.shape[0]
    per = TM // SUBLANES
    last = rows // SUBLANES - 1
    halo = (SUBLANES, D_MODEL)
    in_specs = [_row_spec(D_MODEL),
                pl.BlockSpec(halo, lambda i: (jnp.maximum(i * per - 1, 0), 0)),
                pl.BlockSpec(halo, lambda i: (jnp.minimum((i + 1) * per, last), 0)),
                _mods_spec(1, cond_of_tile), _normg_spec(1),
                _const_spec((D_MODEL, 3 * D_MODEL), (0, 0)), _const_spec((CONV_K, D_MODEL), (0, 0)),
                _const_spec((D_MODEL, D_MODEL), (0, 0))] + _ffn_specs(1, 1)
    return pl.pallas_call(
        functools.partial(_conv_body, seq_len),
        out_shape=jax.ShapeDtypeStruct((rows, D_MODEL), F32),
        grid=(rows // TM,), in_specs=in_specs, out_specs=_row_spec(D_MODEL),
        compiler_params=_params(1), name="conv_ffn",
    )(x, x, x, mods, norm_g, wts["conv_w_in"], wts["conv_w"], wts["conv_w_out"],
      wts["ffn_g"], wts["ffn_u"], wts["ffn_d"])


ROPE_PARTNER = np.concatenate([np.arange(8, 16), np.arange(0, 8), np.arange(24, 32), np.arange(16, 24)])


def _rope_tables(seq_len, g_q, g_k):
    t = np.arange(seq_len)
    quarter = ROPE // 4
    inv_freq = ROPE_BASE ** (-np.arange(quarter, dtype=np.float64) / quarter)
    cos = np.ones((seq_len, LANES))
    sin = np.zeros((seq_len, LANES))
    for base, pos in ((NOPE, t // GRID_W), (NOPE + ROPE // 2, t % GRID_W)):
        ang = pos[:, None].astype(np.float64) * inv_freq[None, :]
        cos[:, base:base + quarter] = np.cos(ang)
        cos[:, base + quarter:base + 2 * quarter] = np.cos(ang)
        sin[:, base:base + quarter] = -np.sin(ang)
        sin[:, base + quarter:base + 2 * quarter] = np.sin(ang)
    cos, sin = jnp.asarray(cos, F32), jnp.asarray(sin, F32)
    out = []
    for g in (g_q, g_k):
        g = g.reshape(QK_DIM)
        g_pad = jnp.pad(g, (0, HEAD_PAD - QK_DIM))
        g_partner = jnp.pad(g[NOPE:][ROPE_PARTNER], (NOPE, HEAD_PAD - QK_DIM))
        out += [g_pad[None, :] * cos, g_partner[None, :] * sin]
    return tuple(out)


def _s5_params(lam_re, lam_im, log_dt, b_re, b_im, c_re, c_im):
    dt = jnp.exp(log_dt)[..., None]
    lr = jnp.minimum(lam_re, LAMBDA_RE_MAX)
    li = lam_im
    mag = jnp.exp(lr * dt)
    ang = li * dt
    ab_re = mag * jnp.cos(ang)
    ab_im = mag * jnp.sin(ang)
    den = lr * lr + li * li
    nr = ab_re - 1.0
    ni = ab_im
    co_re = (nr * lr + ni * li) / den
    co_im = (ni * lr - nr * li) / den
    bb_re = co_re[..., None] * b_re - co_im[..., None] * b_im
    bb_im = co_re[..., None] * b_im + co_im[..., None] * b_re
    per_q = S5_GROUPS // S5_QUARTERS
    eye = jnp.eye(per_q, dtype=F32)

    def in_mat(bb):
        bb = bb.reshape(2, S5_QUARTERS, per_q, S5_STATE, S5_GROUP)
        return jnp.einsum("dqgpi,gh->dqgihp", bb, eye).reshape(2, S5_QUARTERS, LANES, S5_QLANES)

    def out_mat(cc):
        cc = cc.reshape(2, S5_QUARTERS, per_q, S5_GROUP, S5_STATE)
        return jnp.einsum("dqgip,gh->dqgphi", cc, eye).reshape(2, S5_QUARTERS, S5_QLANES, LANES)

    bm = jnp.concatenate([in_mat(bb_re), in_mat(bb_im)], axis=-1).astype(BF16)
    cm = jnp.concatenate([out_mat(c_re), -out_mat(c_im)], axis=-2).astype(BF16)

    def rows(a):
        a = a.reshape(2, 1, S5_QUARTERS, S5_QLANES)
        return jnp.broadcast_to(a, (2, 2, S5_QUARTERS, S5_QLANES)).reshape(2, 2 * S5_QUARTERS, S5_QLANES)

    return bm, cm, jnp.stack([rows(ab_re), rows(ab_im)], axis=1)


def _state_rows(s):
    b = s.shape[0]
    s = s.reshape(b // 2, 2, 2, S5_QUARTERS, S5_QLANES)
    return s.transpose(0, 2, 1, 3, 4).reshape(b // 2, 2, 2 * S5_QUARTERS, S5_QLANES)


def _state_unrows(s, batch):
    s = s.reshape(batch // 2, 2, 2, S5_QUARTERS, S5_QLANES).transpose(0, 2, 1, 3, 4)
    return s.reshape(batch, 1, 2, S5_GROUPS, S5_STATE)


def kernel(x_prompt, x_sample, cache_ckv, cache_krope, state_ssm_re, state_ssm_im, c, c_ctx, w_ada, b_ada, norm_g, ffn_w_gate, ffn_w_up, ffn_w_down, ab_w_in, mla_g_q_lat, mla_g_kv_lat, mla_w_uq, mla_w_ukv, mla_g_qnorm, mla_g_knorm, s5_lam_re, s5_lam_im, s5_log_dt, s5_b_re, s5_b_im, s5_c_re, s5_c_im, s5_d, s5_w_glu, s5_b_glu, ab_w_out, conv_w_in, conv_w, conv_w_out):
    batch, seq, _ = x_prompt.shape
    dec_batch, dec_seq, _ = x_sample.shape
    past = cache_ckv.shape[2]

    w_in = ab_w_in[0]
    kr_cols = w_in[:, Q_LORA + KV_LORA:Q_LORA + KV_LORA + ROPE]
    zeros = jnp.zeros_like(kr_cols)
    w_uq = jnp.pad(mla_w_uq[0].reshape(Q_LORA, HEADS, QK_DIM), ((0, 0), (0, 0), (0, HEAD_PAD - QK_DIM)))
    w_ukv = mla_w_ukv[0].reshape(KV_LORA, HEADS, NOPE + V_DIM)
    w_uk = jnp.pad(w_ukv[:, :, :NOPE], ((0, 0), (0, 0), (0, HEAD_PAD - NOPE)))
    bm, cm, a_rows = _s5_params(s5_lam_re[0], s5_lam_im[0], s5_log_dt[0], s5_b_re[0], s5_b_im[0],
                                s5_c_re[0], s5_c_im[0])
    w_in_cols = [w_in[:, :Q_LORA + KV_LORA], w_in[:, Q_LORA + KV_LORA + ROPE:], kr_cols, zeros, kr_cols, zeros]
    w_in_partner = [zeros, zeros, kr_cols[:, ROPE_PARTNER], zeros]
    w_uq3 = mla_w_uq[0].reshape(Q_LORA, HEADS, QK_DIM)
    w_uq_partner = jnp.pad(w_uq3[:, :, NOPE:][:, :, ROPE_PARTNER], ((0, 0), (0, 0), (NOPE, HEAD_PAD - QK_DIM)))
    wts = {
        "ffn_g": ffn_w_gate.astype(BF16), "ffn_u": ffn_w_up.astype(BF16), "ffn_d": ffn_w_down.astype(BF16),
        "w_in": jnp.concatenate(w_in_cols, axis=1).astype(BF16),
        "w_in_rot": jnp.concatenate(w_in_cols + w_in_partner, axis=1).astype(BF16),
        "g_q_lat": mla_g_q_lat, "g_kv_lat": mla_g_kv_lat,
        "w_uq": w_uq.reshape(Q_LORA, HEADS * HEAD_PAD).astype(BF16),
        "w_uq_rot": jnp.concatenate([w_uq, w_uq_partner], axis=1).reshape(Q_LORA, 2 * HEADS * HEAD_PAD).astype(BF16),
        "w_ukv": jnp.concatenate([w_uk.reshape(KV_LORA, HEADS * HEAD_PAD),
                                  w_ukv[:, :, NOPE:].reshape(KV_LORA, HEADS * V_DIM)], axis=1).astype(BF16),
        "g_qn": jnp.pad(mla_g_qnorm, ((0, 0), (0, HEAD_PAD - QK_DIM))),
        "g_kn": jnp.pad(mla_g_knorm, ((0, 0), (0, HEAD_PAD - QK_DIM))),
        "rope": _rope_tables(dec_seq, mla_g_qnorm, mla_g_knorm),
        "s5_b": bm, "s5_c": cm, "s5_a": a_rows,
        "s5_d": s5_d, "w_glu": s5_w_glu[0].astype(BF16), "b_glu": s5_b_glu, "w_out": ab_w_out[0].astype(BF16),
        "conv_w_in": conv_w_in[0].astype(BF16), "conv_w": conv_w[0].T, "conv_w_out": conv_w_out[0].astype(BF16),
    }

    cond = jnp.zeros((SUBLANES, D_MODEL), F32).at[0].set(c_ctx).at[1:1 + dec_batch].set(c)
    mods = _ada(cond, w_ada, b_ada).reshape(w_ada.shape[0], SUBLANES, N_MOD, D_MODEL)

    def trunk(x, seq_len, cond_of_tile, latent):
        rows = x.shape[0]
        res = _mixin(x, mods, norm_g, wts, cond_of_tile, seq_len, rotate=latent, want_cache=not latent)
        x1, q, k, v, u = res[:5]
        if latent:
            kr_pad = jnp.pad(cache_krope[:, 0].reshape(dec_batch * past, ROPE), ((0, 0), (NOPE, LANES - QK_DIM)))
            ctx = _ctx_kv(cache_ckv[:, 0].reshape(dec_batch * past, KV_LORA), kr_pad, wts)
            h0 = jnp.stack([_state_rows(state_ssm_re[:, 0]), _state_rows(state_ssm_im[:, 0])], axis=2)
            attn = _attention(q, k, v, seq_len, 256, ctx)
            yf, yb = _s5(u, wts, seq_len, h0=h0)
            extra = ()
        else:
            attn = _attention(q, k, v, seq_len, seq_len)
            yf, yb, fin = _s5(u, wts, seq_len, want_fin=True)
            extra = (res[5], res[6], fin)
        x2 = _mixout(x1, attn, u, yf, yb, mods, norm_g, wts, cond_of_tile)
        x3 = _ffn_call(x2, mods, norm_g, wts, cond_of_tile, 1, 0)
        return _conv_mixer(x3, mods, norm_g, wts, cond_of_tile, seq_len), extra

    tiles_per_sample = dec_seq // TM
    y_p, (ckv_p, kr_p, fin) = trunk(x_prompt.reshape(batch * seq, D_MODEL), seq, lambda i: 0, False)
    y_s, _ = trunk(x_sample.reshape(dec_batch * dec_seq, D_MODEL), dec_seq, lambda i: 1 + i // tiles_per_sample, True)
    return (y_p.reshape(batch, seq, D_MODEL), y_s.reshape(dec_batch, dec_seq, D_MODEL),
            ckv_p.reshape(batch, 1, seq, KV_LORA), kr_p.reshape(batch, 1, seq, ROPE),
            _state_unrows(fin[:, :, 0], batch), _state_unrows(fin[:, :, 1], batch))
```

```python
import functools
import math

import jax
import jax.numpy as jnp
import numpy as np
from jax import lax
from jax.experimental import pallas as pl
from jax.experimental.pallas import tpu as pltpu

F32 = jnp.float32
BF16 = jnp.bfloat16

LANES = 128
SUBLANES = 8
VMEM_LIMIT_BYTES = 60 * 1024 * 1024

D_MODEL = 1024
D_FF = 2816
N_MOD = 9
EPS = 1e-6
HEADS = 8
Q_LORA = 384
KV_LORA = 256
NOPE = 64
ROPE = 32
V_DIM = 64
QK_DIM = NOPE + ROPE
HEAD_PAD = LANES
ROPE_BASE = 10000.0
GRID_W = 64
S5_WIDTH = 512
S5_GROUP = 16
S5_GROUPS = 32
S5_STATE = 64
S5_LANES = S5_GROUPS * S5_STATE
S5_QUARTERS = 4
S5_QLANES = S5_LANES // S5_QUARTERS
S5_SLABS = S5_QLANES // LANES
S5_TC = 128
S5_PITCH = S5_TC + 4
S5_SEGMENTS = 1
LAMBDA_RE_MAX = -1e-4
CONV_K = 3

ATT_ROWS = 2 * SUBLANES
TM = 512
FF_CHUNKS = tuple((c, min(512, D_FF - c)) for c in range(0, D_FF, 512))

O_CKV = Q_LORA
O_U = Q_LORA + KV_LORA
O_KR = O_U + S5_WIDTH


def _dot(a, b):
    return jnp.dot(a, b, preferred_element_type=F32)


def _sigmoid(x):
    return 1.0 / (1.0 + jnp.exp(-x))


def _rms(x, g, n):
    ms = jnp.sum(x * x, axis=-1, keepdims=True) * (1.0 / n)
    return x * lax.rsqrt(ms + EPS) * g


def _modulate(x, g, shift, scale):
    return _rms(x, g, D_MODEL) * (1.0 + scale) + shift


def _ffn(x, shift, scale, gate, g, wg_ref, wu_ref, wd_ref):
    h = _modulate(x, g, shift, scale).astype(BF16)
    acc = jnp.zeros(x.shape, F32)
    for c0, cs in FF_CHUNKS:
        gt = _dot(h, wg_ref[:, c0:c0 + cs])
        up = _dot(h, wu_ref[:, c0:c0 + cs])
        act = ((gt * _sigmoid(gt)) * up).astype(BF16)
        acc = acc + _dot(act, wd_ref[c0:c0 + cs, :])
    return x + (0.5 * gate) * acc


def _head_inv_rms(xh):
    return lax.rsqrt(jnp.sum(xh * xh, axis=-1, keepdims=True) * (1.0 / QK_DIM) + EPS)


def _head_norm(xh, g):
    return xh * _head_inv_rms(xh) * g


def _gelu_tanh(x):
    return x * (0.5 * (1.0 + jnp.tanh(math.sqrt(2.0 / math.pi) * (x + 0.044715 * (x * x * x)))))


def _const_spec(shape, index):
    return pl.BlockSpec(shape, lambda *_: index, pipeline_mode=pl.Buffered(1))


def _params(n_axes):
    return pltpu.CompilerParams(dimension_semantics=("arbitrary",) * n_axes, vmem_limit_bytes=VMEM_LIMIT_BYTES)


def _ffn_specs(layer, sub):
    w = _const_spec((None, None, D_MODEL, D_FF), (layer, sub, 0, 0))
    return [w, w, _const_spec((None, None, D_FF, D_MODEL), (layer, sub, 0, 0))]


def _mods_spec(layer, cond_of_tile):
    return pl.BlockSpec((None, None, N_MOD, D_MODEL), lambda i: (layer, cond_of_tile(i), 0, 0))


def _normg_spec(layer):
    return _const_spec((None, 3, D_MODEL), (layer, 0, 0))


def _row_spec(width, rows=TM):
    return pl.BlockSpec((rows, width), lambda i: (i, 0))


def _ada_body(c_ref, w_ref, b_ref, o_ref):
    c = c_ref[...]
    s = (c * _sigmoid(c)).astype(BF16)
    o_ref[...] = _dot(s, w_ref[...].astype(BF16)) + b_ref[...]


def _ada(cond, w_ada, b_ada):
    depth, _, n = w_ada.shape
    tn = 1152
    return pl.pallas_call(
        _ada_body,
        out_shape=jax.ShapeDtypeStruct((depth, SUBLANES, n), F32),
        grid=(depth, n // tn),
        in_specs=[pl.BlockSpec((SUBLANES, D_MODEL), lambda l, j: (0, 0)),
                  pl.BlockSpec((None, D_MODEL, tn), lambda l, j: (l, 0, j)),
                  pl.BlockSpec((None, 1, tn), lambda l, j: (l, 0, j))],
        out_specs=pl.BlockSpec((None, SUBLANES, tn), lambda l, j: (l, 0, j)),
        compiler_params=_params(2),
        name="ada",
    )(cond, w_ada, b_ada.reshape(depth, 1, n))


def _mixin_body(rotate, want_cache, *refs):
    (x_ref, mods_ref, g_ref, wg_ref, wu_ref, wd_ref, win_ref, gql_ref, gkv_ref, wuq_ref, wukv_ref,
     gq_ref, gk_ref) = refs[:13]
    refs = refs[13:]
    if rotate:
        qc_ref, qs_ref, kc_ref, ks_ref = refs[:4]
        refs = refs[4:]
    x1_ref, q_ref, k_ref, v_ref, u_ref = refs[:5]
    mods = mods_ref[...]
    g3 = g_ref[...]
    x1 = _ffn(x_ref[...], mods[0:1], mods[1:2], mods[2:3], g3[0:1], wg_ref, wu_ref, wd_ref)
    x1_ref[...] = x1
    hn = _modulate(x1, g3[1:2], mods[3:4], mods[4:5]).astype(BF16)
    proj = _dot(hn, win_ref[...])
    u_ref[...] = proj[:, O_U:O_KR]
    ckv = _rms(proj[:, O_CKV:O_U], gkv_ref[...], KV_LORA)
    krg = proj[:, O_KR:O_KR + LANES]
    if want_cache:
        ckv_ref, kr_ref = refs[5:7]
        ckv_ref[...] = ckv
        kr_ref[...] = krg[:, 0:ROPE]
    qn = _rms(proj[:, 0:Q_LORA], gql_ref[...], Q_LORA).astype(BF16)
    qraw = _dot(qn, wuq_ref[...])
    kv = _dot(ckv.astype(BF16), wukv_ref[...])
    v_ref[...] = kv[:, HEADS * HEAD_PAD:].astype(BF16)
    lane = lax.broadcasted_iota(jnp.int32, (1, LANES), 1)
    kr_only = jnp.where((lane >= NOPE) & (lane < QK_DIM), krg, 0.0)
    gq = gq_ref[...]
    gk = gk_ref[...]
    if rotate:
        kr_partner = proj[:, O_KR + LANES:O_KR + 2 * LANES]
        k_rot = kr_only * kc_ref[...] + kr_partner * ks_ref[...]
    for h in range(HEADS):
        sl = slice(h * HEAD_PAD, (h + 1) * HEAD_PAD)
        if rotate:
            psl = slice((HEADS + h) * HEAD_PAD, (HEADS + h + 1) * HEAD_PAD)
            qa = qraw[:, sl]
            qh = (qa * qc_ref[...] + qraw[:, psl] * qs_ref[...]) * _head_inv_rms(qa)
            kh = (kv[:, sl] * gk + k_rot) * _head_inv_rms(kv[:, sl] + kr_only)
        else:
            qh = _head_norm(qraw[:, sl], gq)
            kh = _head_norm(kv[:, sl] + kr_only, gk)
        q_ref[:, sl] = qh.astype(BF16)
        k_ref[:, sl] = kh.astype(BF16)


def _mixin(x, mods, norm_g, wts, cond_of_tile, seq_len, rotate, want_cache):
    rows = x.shape[0]
    tiles_per_seq = seq_len // TM if seq_len >= TM else 1
    w_in, w_uq = (wts["w_in_rot"], wts["w_uq_rot"]) if rotate else (wts["w_in"], wts["w_uq"])
    in_specs = [_row_spec(D_MODEL), _mods_spec(0, cond_of_tile), _normg_spec(0)] + _ffn_specs(0, 0) + [
        _const_spec(w_in.shape, (0, 0)),
        _const_spec((1, Q_LORA), (0, 0)),
        _const_spec((1, KV_LORA), (0, 0)),
        _const_spec(w_uq.shape, (0, 0)),
        _const_spec((KV_LORA, HEADS * HEAD_PAD + HEADS * V_DIM), (0, 0)),
        _const_spec((1, HEAD_PAD), (0, 0)),
        _const_spec((1, HEAD_PAD), (0, 0)),
    ]
    args = [x, mods, norm_g, wts["ffn_g"], wts["ffn_u"], wts["ffn_d"], w_in, wts["g_q_lat"], wts["g_kv_lat"],
            w_uq, wts["w_ukv"], wts["g_qn"], wts["g_kn"]]
    if rotate:
        in_specs += [pl.BlockSpec((TM, LANES), lambda i: (i % tiles_per_seq, 0))] * 4
        args += list(wts["rope"])
    out_shape = [jax.ShapeDtypeStruct((rows, D_MODEL), F32),
                 jax.ShapeDtypeStruct((rows, HEADS * HEAD_PAD), BF16),
                 jax.ShapeDtypeStruct((rows, HEADS * HEAD_PAD), BF16),
                 jax.ShapeDtypeStruct((rows, HEADS * V_DIM), BF16),
                 jax.ShapeDtypeStruct((rows, S5_WIDTH), F32)]
    out_specs = [_row_spec(D_MODEL), _row_spec(HEADS * HEAD_PAD), _row_spec(HEADS * HEAD_PAD),
                 _row_spec(HEADS * V_DIM), _row_spec(S5_WIDTH)]
    if want_cache:
        out_shape += [jax.ShapeDtypeStruct((rows, KV_LORA), F32), jax.ShapeDtypeStruct((rows, ROPE), F32)]
        out_specs += [_row_spec(KV_LORA), _row_spec(ROPE)]
    return pl.pallas_call(
        functools.partial(_mixin_body, rotate, want_cache),
        out_shape=out_shape, grid=(rows // TM,), in_specs=in_specs, out_specs=out_specs,
        compiler_params=_params(1), name="ffn_mixin",
    )(*args)


def _ctx_body(ckv_ref, krp_ref, wukv_ref, gk_ref, k_ref, v_ref):
    kv = _dot(ckv_ref[...].astype(BF16), wukv_ref[...])
    v_ref[...] = kv[:, HEADS * HEAD_PAD:].astype(BF16)
    gk = gk_ref[...]
    krp = krp_ref[...]
    for h in range(HEADS):
        sl = slice(h * HEAD_PAD, (h + 1) * HEAD_PAD)
        k_ref[:, sl] = _head_norm(kv[:, sl] + krp, gk).astype(BF16)


def _ctx_kv(ckv, kr_padded, wts):
    rows = ckv.shape[0]
    return pl.pallas_call(
        _ctx_body,
        out_shape=[jax.ShapeDtypeStruct((rows, HEADS * HEAD_PAD), BF16),
                   jax.ShapeDtypeStruct((rows, HEADS * V_DIM), BF16)],
        grid=(rows // TM,),
        in_specs=[_row_spec(KV_LORA), _row_spec(LANES),
                  _const_spec((KV_LORA, HEADS * HEAD_PAD + HEADS * V_DIM), (0, 0)),
                  _const_spec((1, HEAD_PAD), (0, 0))],
        out_specs=[_row_spec(HEADS * HEAD_PAD), _row_spec(HEADS * V_DIM)],
        compiler_params=_params(1), name="ctx_kv",
    )(ckv, kr_padded, wts["w_ukv"], wts["g_kn"])


def _attn_body(n_kv, *refs):
    q_ref = refs[0]
    k_refs = refs[1:1 + n_kv]
    v_refs = refs[1 + n_kv:1 + 2 * n_kv]
    o_ref, s_ref, p_ref, r_ref = refs[1 + 2 * n_kv:]
    tq = q_ref.shape[0]
    c = (QK_DIM ** -0.5) * math.log2(math.e)
    lane = lax.broadcasted_iota(jnp.int32, (1, LANES), 1)
    spans = []
    for k_ref in k_refs:
        spans.append(slice(spans[-1].stop if spans else 0, (spans[-1].stop if spans else 0) + k_ref.shape[0]))
    for pair in range(HEADS // 2):
        vsl = slice(pair * LANES, (pair + 1) * LANES)
        outs = []
        for h in (2 * pair, 2 * pair + 1):
            slot = h % 2
            sl = slice(h * HEAD_PAD, (h + 1) * HEAD_PAD)
            q = q_ref[:, sl]
            for k_ref, span in zip(k_refs, spans):
                s_ref[slot, :, span] = lax.dot_general(q, k_ref[:, sl], (((1,), (1,)), ((), ())),
                                                       preferred_element_type=F32)
            for r0 in range(0, tq, ATT_ROWS):
                rows = slice(r0, r0 + ATT_ROWS)
                s = s_ref[slot, rows, :]
                e = jnp.exp2((s - jnp.max(s, axis=-1, keepdims=True)) * c)
                r_ref[slot, rows, :] = jnp.broadcast_to(1.0 / jnp.sum(e, axis=-1, keepdims=True), (ATT_ROWS, LANES))
                p_ref[slot, rows, :] = e.astype(BF16)
            o = functools.reduce(jnp.add, [_dot(p_ref[slot, :, span], v_ref[:, vsl])
                                           for v_ref, span in zip(v_refs, spans)])
            outs.append(o * r_ref[slot])
        o_ref[:, vsl] = jnp.where(lane < V_DIM, outs[0], outs[1]).astype(BF16)


def _attention(q, k, v, seq_len, tq, ctx=None):
    rows = q.shape[0]
    q_tiles = seq_len // tq
    kv_spec = lambda w: pl.BlockSpec((seq_len, w), lambda b, t: (b, 0))
    in_specs = [pl.BlockSpec((tq, HEADS * HEAD_PAD), lambda b, t: (b * q_tiles + t, 0))]
    ks, vs = [kv_spec(HEADS * HEAD_PAD)], [kv_spec(HEADS * V_DIM)]
    kargs, vargs = [k], [v]
    if ctx is not None:
        k_c, v_c = ctx
        past = k_c.shape[0] // (rows // seq_len)
        ks.insert(0, pl.BlockSpec((past, HEADS * HEAD_PAD), lambda b, t: (b, 0)))
        vs.insert(0, pl.BlockSpec((past, HEADS * V_DIM), lambda b, t: (b, 0)))
        kargs.insert(0, k_c)
        vargs.insert(0, v_c)
    n_keys = sum(a.shape[0] for a in kargs) // (rows // seq_len)
    return pl.pallas_call(
        functools.partial(_attn_body, len(kargs)),
        out_shape=jax.ShapeDtypeStruct((rows, HEADS * V_DIM), BF16),
        grid=(rows // seq_len, q_tiles),
        in_specs=in_specs + ks + vs,
        out_specs=pl.BlockSpec((tq, HEADS * V_DIM), lambda b, t: (b * q_tiles + t, 0)),
        scratch_shapes=[pltpu.VMEM((2, tq, n_keys), F32), pltpu.VMEM((2, tq, n_keys), BF16),
                        pltpu.VMEM((2, tq, LANES), F32)],
        compiler_params=_params(2), name="attention",
    )(q, *kargs, *vargs)


def _s5_project_in(u_ref, rows, bm_ref, d, q, bu_ref):
    qsl = slice(q * LANES, (q + 1) * LANES)
    lhs = jnp.concatenate([u_ref[0, rows, qsl], u_ref[1, rows, qsl]], axis=0).astype(BF16)
    bu = _dot(lhs, bm_ref[d, q])
    for b in range(2):
        r = b * S5_QUARTERS + q
        for j in range(2 * S5_SLABS):
            bu_ref[d, j, pl.ds(r * S5_PITCH, S5_TC), :] = bu[b * S5_TC:(b + 1) * S5_TC, j * LANES:(j + 1) * LANES]


def _s5_scan(bu_ref, st_ref, a_ref, carry_ref, steps):
    nrow = 2 * S5_QUARTERS
    order = [(d, j) for d in range(2) for j in range(S5_SLABS)]
    lanes = lambda j: slice(j * LANES, (j + 1) * LANES)
    carry = {(d, j): (carry_ref[d, 0, :, lanes(j)], carry_ref[d, 1, :, lanes(j)]) for d, j in order}
    for i in steps:
        for d, j in order:
            t = i if d == 0 else S5_TC - 1 - i
            rows = pl.ds(t, nrow, stride=S5_PITCH)
            sr, si = carry[d, j]
            ar, ai = a_ref[d, 0, :, lanes(j)], a_ref[d, 1, :, lanes(j)]
            nr = ar * sr - ai * si + bu_ref[d, j, rows, :]
            ni = ar * si + ai * sr + bu_ref[d, S5_SLABS + j, rows, :]
            st_ref[d, j, rows, :] = nr
            st_ref[d, S5_SLABS + j, rows, :] = ni
            carry[d, j] = (nr, ni)
    for d, j in order:
        carry_ref[d, 0, :, lanes(j)], carry_ref[d, 1, :, lanes(j)] = carry[d, j]


def _s5_project_out(st_ref, d, q, cm_ref, y_ref, rows):
    qsl = slice(q * LANES, (q + 1) * LANES)
    lhs = jnp.concatenate(
        [jnp.concatenate([st_ref[d, j, pl.ds((b * S5_QUARTERS + q) * S5_PITCH, S5_TC), :]
                          for j in range(2 * S5_SLABS)], axis=-1) for b in range(2)], axis=0).astype(BF16)
    y = _dot(lhs, cm_ref[d, q])
    y_ref[0, rows, qsl] = y[0:S5_TC]
    y_ref[1, rows, qsl] = y[S5_TC:2 * S5_TC]


def _s5_body(n_super, has_h0, want_fin, *refs):
    refs = list(refs)
    uf_ref = refs.pop(0)
    ub_ref = refs.pop(0) if n_super > 1 else uf_ref
    bm_ref, cm_ref, a_ref = refs[:3]
    refs = refs[3:]
    h0_ref = refs.pop(0) if has_h0 else None
    yf_ref, yb_ref = refs[:2]
    refs = refs[2:]
    fin_ref = refs.pop(0) if want_fin else None
    bu0_ref, bu1_ref, st0_ref, st1_ref, carry_ref = refs
    s = pl.program_id(0)
    lo, hi = slice(0, S5_TC), slice(S5_TC, 2 * S5_TC)
    n_seg = S5_SEGMENTS
    seg_steps = S5_TC // n_seg
    seg_pairs = 2 * S5_QUARTERS // n_seg

    @pl.when(s == 0)
    def _():
        bu1_ref[...] = jnp.zeros(bu1_ref.shape, F32)
        st0_ref[...] = jnp.zeros(st0_ref.shape, F32)
        carry_ref[...] = jnp.zeros(carry_ref.shape, F32)

    def half(bu_w, bu_r, st_w, st_r, rows_f, rows_b, tag):
        for k in range(n_seg):
            @pl.when(s > -(1 + tag * n_seg + k))
            def _():
                io = ((uf_ref, rows_f, yf_ref), (ub_ref, rows_b, yb_ref))
                sub = seg_steps // seg_pairs
                for m in range(k * seg_pairs, (k + 1) * seg_pairs):
                    d, q = divmod(m, S5_QUARTERS)
                    _s5_project_in(io[d][0], io[d][1], bm_ref, d, q, bu_w)
                    _s5_scan(bu_r, st_w, a_ref, carry_ref, range(m * sub, (m + 1) * sub))
                    _s5_project_out(st_r, d, q, cm_ref, io[d][2], io[d][1])

    half(bu0_ref, bu1_ref, st1_ref, st0_ref, lo, hi, 0)

    @pl.when(s > -(1 + 2 * n_seg))
    def _():
        if want_fin:
            fin_ref[...] = carry_ref[...]
        start = h0_ref[...] if has_h0 else jnp.zeros(carry_ref.shape, F32)
        if n_super > 1:
            start = jnp.where(s % n_super == 0, start, carry_ref[...])
        carry_ref[...] = start

    half(bu1_ref, bu0_ref, st0_ref, st1_ref, hi, lo, 1)


def _s5(u, wts, seq_len, h0=None, want_fin=False):
    batch = u.shape[0] // seq_len
    n_super = seq_len // (2 * S5_TC)
    n_steps = (batch // 2) * n_super
    assert not (want_fin and n_super > 1)
    u3 = u.reshape(batch, seq_len, S5_WIDTH)
    nrow = 2 * S5_QUARTERS
    blk = (2, 2 * S5_TC, S5_WIDTH)

    def chunk_spec(delay, reverse):
        def index(s):
            s = jnp.clip(s - delay, 0, n_steps - 1)
            c = s % n_super
            return (s // n_super, n_super - 1 - c if reverse else c, 0)
        return pl.BlockSpec(blk, index)

    def state_spec(delay):
        return pl.BlockSpec((None, 2, 2, nrow, S5_QLANES),
                            lambda s: (jnp.clip(s - delay, 0, n_steps - 1) // n_super, 0, 0, 0, 0))

    in_specs, args = [chunk_spec(0, False)], [u3]
    if n_super > 1:
        in_specs.append(chunk_spec(0, True))
        args.append(u3)
    in_specs += [_const_spec((2, S5_QUARTERS, LANES, 2 * S5_QLANES), (0, 0, 0, 0)),
                 _const_spec((2, S5_QUARTERS, 2 * S5_QLANES, LANES), (0, 0, 0, 0)),
                 _const_spec((2, 2, nrow, S5_QLANES), (0, 0, 0, 0))]
    args += [wts["s5_b"], wts["s5_c"], wts["s5_a"]]
    if h0 is not None:
        in_specs.append(state_spec(0))
        args.append(h0)
    y_shape = jax.ShapeDtypeStruct((batch, seq_len, S5_WIDTH), F32)
    out_shape, out_specs = [y_shape, y_shape], [chunk_spec(1, False), chunk_spec(1, True)]
    if want_fin:
        out_shape.append(jax.ShapeDtypeStruct((batch // 2, 2, 2, nrow, S5_QLANES), F32))
        out_specs.append(state_spec(1))
    work = pltpu.VMEM((2, 2 * S5_SLABS, nrow * S5_PITCH, LANES), F32)
    return pl.pallas_call(
        functools.partial(_s5_body, n_super, h0 is not None, want_fin),
        out_shape=out_shape, grid=(n_steps + 1,), in_specs=in_specs, out_specs=out_specs,
        scratch_shapes=[work, work, work, work, pltpu.VMEM((2, 2, nrow, S5_QLANES), F32)],
        compiler_params=_params(1), name="s5",
    )(*args)


def _mixout_body(x_ref, attn_ref, u_ref, yf_ref, yb_ref, mods_ref, g_ref, dsk_ref, wglu_ref, bglu_ref, wout_ref,
                 wg_ref, wu_ref, wd_ref, o_ref):
    mods = mods_ref[...]
    g3 = g_ref[...]
    y = dsk_ref[...] * u_ref[...] + yf_ref[...] + yb_ref[...]
    z = _gelu_tanh(y)
    s5o = z * _sigmoid(_dot(z.astype(BF16), wglu_ref[...]) + bglu_ref[...])
    half = HEADS * V_DIM
    mix = _dot(attn_ref[...], wout_ref[0:half, :]) + _dot(s5o.astype(BF16), wout_ref[half:, :])
    x2 = x_ref[...] + mods[5:6] * mix
    o_ref[...] = _ffn(x2, mods[6:7], mods[7:8], mods[8:9], g3[2:3], wg_ref, wu_ref, wd_ref)


def _mixout(x, attn, u, yf, yb, mods, norm_g, wts, cond_of_tile):
    rows = x.shape[0]
    in_specs = [_row_spec(D_MODEL), _row_spec(HEADS * V_DIM), _row_spec(S5_WIDTH), _row_spec(S5_WIDTH),
                _row_spec(S5_WIDTH), _mods_spec(0, cond_of_tile), _normg_spec(0),
                _const_spec((1, S5_WIDTH), (0, 0)), _const_spec((S5_WIDTH, S5_WIDTH), (0, 0)),
                _const_spec((1, S5_WIDTH), (0, 0)), _const_spec((D_MODEL, D_MODEL), (0, 0))] + _ffn_specs(0, 1)
    return pl.pallas_call(
        _mixout_body,
        out_shape=jax.ShapeDtypeStruct((rows, D_MODEL), F32),
        grid=(rows // TM,), in_specs=in_specs, out_specs=_row_spec(D_MODEL),
        compiler_params=_params(1), name="mixout_ffn",
    )(x, attn, u, yf.reshape(rows, S5_WIDTH), yb.reshape(rows, S5_WIDTH), mods, norm_g, wts["s5_d"], wts["w_glu"],
      wts["b_glu"], wts["w_out"], wts["ffn_g"], wts["ffn_u"], wts["ffn_d"])


def _ffn_body(sub, x_ref, mods_ref, g_ref, wg_ref, wu_ref, wd_ref, o_ref):
    mods = mods_ref[...]
    g3 = g_ref[...]
    m = 6 if sub else 0
    n = 2 if sub else 0
    o_ref[...] = _ffn(x_ref[...], mods[m:m + 1], mods[m + 1:m + 2], mods[m + 2:m + 3], g3[n:n + 1],
                      wg_ref, wu_ref, wd_ref)


def _ffn_call(x, mods, norm_g, wts, cond_of_tile, layer, sub):
    rows = x.shape[0]
    return pl.pallas_call(
        functools.partial(_ffn_body, sub),
        out_shape=jax.ShapeDtypeStruct((rows, D_MODEL), F32),
        grid=(rows // TM,),
        in_specs=[_row_spec(D_MODEL), _mods_spec(layer, cond_of_tile), _normg_spec(layer)] + _ffn_specs(layer, sub),
        out_specs=_row_spec(D_MODEL),
        compiler_params=_params(1), name="ffn",
    )(x, mods, norm_g, wts["ffn_g"], wts["ffn_u"], wts["ffn_d"])


def _conv_body(seq_len, x_ref, xp_ref, xn_ref, mods_ref, g_ref, cwin_ref, cw_ref, cwout_ref,
               wg_ref, wu_ref, wd_ref, o_ref):
    i = pl.program_id(0)
    mods = mods_ref[...]
    g3 = g_ref[...]
    x = x_ref[...]
    n_ext = TM + 2 * SUBLANES
    xe = jnp.concatenate([xp_ref[...], x, xn_ref[...]], axis=0)
    hne = _modulate(xe, g3[1:2], mods[3:4], mods[4:5]).astype(BF16)
    pz = _dot(hne, cwin_ref[:, D_MODEL:3 * D_MODEL])
    z = pz[:, 0:D_MODEL] * pz[:, D_MODEL:2 * D_MODEL]
    gate_b = _dot(hne[SUBLANES:SUBLANES + TM], cwin_ref[:, 0:D_MODEL])
    main = slice(SUBLANES, SUBLANES + TM)
    pos = (i * TM + lax.broadcasted_iota(jnp.int32, (TM, 1), 0)) % seq_len
    z_prev = jnp.where(pos == 0, 0.0, pltpu.roll(z, 1, 0)[main])
    z_next = jnp.where(pos == seq_len - 1, 0.0, pltpu.roll(z, n_ext - 1, 0)[main])
    cw = cw_ref[...]
    zc = z_prev * cw[0:1] + z[main] * cw[1:2] + z_next * cw[2:3]
    mix = _dot((gate_b * zc).astype(BF16), cwout_ref[...])
    x2 = x + mods[5:6] * mix
    o_ref[...] = _ffn(x2, mods[6:7], mods[7:8], mods[8:9], g3[2:3], wg_ref, wu_ref, wd_ref)


def _conv_mixer(x, mods, norm_g, wts, cond_of_tile, seq_len):
    rows = x.shape[0]
    per = TM // SUBLANES
    last = rows // SUBLANES - 1
    halo = (SUBLANES, D_MODEL)
    in_specs = [_row_spec(D_MODEL),
                pl.BlockSpec(halo, lambda i: (jnp.maximum(i * per - 1, 0), 0)),
                pl.BlockSpec(halo, lambda i: (jnp.minimum((i + 1) * per, last), 0)),
                _mods_spec(1, cond_of_tile), _normg_spec(1),
                _const_spec((D_MODEL, 3 * D_MODEL), (0, 0)), _const_spec((CONV_K, D_MODEL), (0, 0)),
                _const_spec((D_MODEL, D_MODEL), (0, 0))] + _ffn_specs(1, 1)
    return pl.pallas_call(
        functools.partial(_conv_body, seq_len),
        out_shape=jax.ShapeDtypeStruct((rows, D_MODEL), F32),
        grid=(rows // TM,), in_specs=in_specs, out_specs=_row_spec(D_MODEL),
        compiler_params=_params(1), name="conv_ffn",
    )(x, x, x, mods, norm_g, wts["conv_w_in"], wts["conv_w"], wts["conv_w_out"],
      wts["ffn_g"], wts["ffn_u"], wts["ffn_d"])


ROPE_PARTNER = np.concatenate([np.arange(8, 16), np.arange(0, 8), np.arange(24, 32), np.arange(16, 24)])


def _rope_tables(seq_len, g_q, g_k):
    t = np.arange(seq_len)
    quarter = ROPE // 4
    inv_freq = ROPE_BASE ** (-np.arange(quarter, dtype=np.float64) / quarter)
    cos = np.ones((seq_len, LANES))
    sin = np.zeros((seq_len, LANES))
    for base, pos in ((NOPE, t // GRID_W), (NOPE + ROPE // 2, t % GRID_W)):
        ang = pos[:, None].astype(np.float64) * inv_freq[None, :]
        cos[:, base:base + quarter] = np.cos(ang)
        cos[:, base + quarter:base + 2 * quarter] = np.cos(ang)
        sin[:, base:base + quarter] = -np.sin(ang)
        sin[:, base + quarter:base + 2 * quarter] = np.sin(ang)
    cos, sin = jnp.asarray(cos, F32), jnp.asarray(sin, F32)
    out = []
    for g in (g_q, g_k):
        g = g.reshape(QK_DIM)
        g_pad = jnp.pad(g, (0, HEAD_PAD - QK_DIM))
        g_partner = jnp.pad(g[NOPE:][ROPE_PARTNER], (NOPE, HEAD_PAD - QK_DIM))
        out += [g_pad[None, :] * cos, g_partner[None, :] * sin]
    return tuple(out)


def _s5_params(lam_re, lam_im, log_dt, b_re, b_im, c_re, c_im):
    dt = jnp.exp(log_dt)[..., None]
    lr = jnp.minimum(lam_re, LAMBDA_RE_MAX)
    li = lam_im
    mag = jnp.exp(lr * dt)
    ang = li * dt
    ab_re = mag * jnp.cos(ang)
    ab_im = mag * jnp.sin(ang)
    den = lr * lr + li * li
    nr = ab_re - 1.0
    ni = ab_im
    co_re = (nr * lr + ni * li) / den
    co_im = (ni * lr - nr * li) / den
    bb_re = co_re[..., None] * b_re - co_im[..., None] * b_im
    bb_im = co_re[..., None] * b_im + co_im[..., None] * b_re
    per_q = S5_GROUPS // S5_QUARTERS
    eye = jnp.eye(per_q, dtype=F32)

    def in_mat(bb):
        bb = bb.reshape(2, S5_QUARTERS, per_q, S5_STATE, S5_GROUP)
        return jnp.einsum("dqgpi,gh->dqgihp", bb, eye).reshape(2, S5_QUARTERS, LANES, S5_QLANES)

    def out_mat(cc):
        cc = cc.reshape(2, S5_QUARTERS, per_q, S5_GROUP, S5_STATE)
        return jnp.einsum("dqgip,gh->dqgphi", cc, eye).reshape(2, S5_QUARTERS, S5_QLANES, LANES)

    bm = jnp.concatenate([in_mat(bb_re), in_mat(bb_im)], axis=-1).astype(BF16)
    cm = jnp.concatenate([out_mat(c_re), -out_mat(c_im)], axis=-2).astype(BF16)

    def rows(a):
        a = a.reshape(2, 1, S5_QUARTERS, S5_QLANES)
        return jnp.broadcast_to(a, (2, 2, S5_QUARTERS, S5_QLANES)).reshape(2, 2 * S5_QUARTERS, S5_QLANES)

    return bm, cm, jnp.stack([rows(ab_re), rows(ab_im)], axis=1)


def _state_rows(s):
    b = s.shape[0]
    s = s.reshape(b // 2, 2, 2, S5_QUARTERS, S5_QLANES)
    return s.transpose(0, 2, 1, 3, 4).reshape(b // 2, 2, 2 * S5_QUARTERS, S5_QLANES)


def _state_unrows(s, batch):
    s = s.reshape(batch // 2, 2, 2, S5_QUARTERS, S5_QLANES).transpose(0, 2, 1, 3, 4)
    return s.reshape(batch, 1, 2, S5_GROUPS, S5_STATE)


def kernel(x_prompt, x_sample, cache_ckv, cache_krope, state_ssm_re, state_ssm_im, c, c_ctx, w_ada, b_ada, norm_g, ffn_w_gate, ffn_w_up, ffn_w_down, ab_w_in, mla_g_q_lat, mla_g_kv_lat, mla_w_uq, mla_w_ukv, mla_g_qnorm, mla_g_knorm, s5_lam_re, s5_lam_im, s5_log_dt, s5_b_re, s5_b_im, s5_c_re, s5_c_im, s5_d, s5_w_glu, s5_b_glu, ab_w_out, conv_w_in, conv_w, conv_w_out):
    batch, seq, _ = x_prompt.shape
    dec_batch, dec_seq, _ = x_sample.shape
    past = cache_ckv.shape[2]

    w_in = ab_w_in[0]
    kr_cols = w_in[:, Q_LORA + KV_LORA:Q_LORA + KV_LORA + ROPE]
    zeros = jnp.zeros_like(kr_cols)
    w_uq = jnp.pad(mla_w_uq[0].reshape(Q_LORA, HEADS, QK_DIM), ((0, 0), (0, 0), (0, HEAD_PAD - QK_DIM)))
    w_ukv = mla_w_ukv[0].reshape(KV_LORA, HEADS, NOPE + V_DIM)
    w_uk = jnp.pad(w_ukv[:, :, :NOPE], ((0, 0), (0, 0), (0, HEAD_PAD - NOPE)))
    bm, cm, a_rows = _s5_params(s5_lam_re[0], s5_lam_im[0], s5_log_dt[0], s5_b_re[0], s5_b_im[0],
                                s5_c_re[0], s5_c_im[0])
    w_in_cols = [w_in[:, :Q_LORA + KV_LORA], w_in[:, Q_LORA + KV_LORA + ROPE:], kr_cols, zeros, kr_cols, zeros]
    w_in_partner = [zeros, zeros, kr_cols[:, ROPE_PARTNER], zeros]
    w_uq3 = mla_w_uq[0].reshape(Q_LORA, HEADS, QK_DIM)
    w_uq_partner = jnp.pad(w_uq3[:, :, NOPE:][:, :, ROPE_PARTNER], ((0, 0), (0, 0), (NOPE, HEAD_PAD - QK_DIM)))
    wts = {
        "ffn_g": ffn_w_gate.astype(BF16), "ffn_u": ffn_w_up.astype(BF16), "ffn_d": ffn_w_down.astype(BF16),
        "w_in": jnp.concatenate(w_in_cols, axis=1).astype(BF16),
        "w_in_rot": jnp.concatenate(w_in_cols + w_in_partner, axis=1).astype(BF16),
        "g_q_lat": mla_g_q_lat, "g_kv_lat": mla_g_kv_lat,
        "w_uq": w_uq.reshape(Q_LORA, HEADS * HEAD_PAD).astype(BF16),
        "w_uq_rot": jnp.concatenate([w_uq, w_uq_partner], axis=1).reshape(Q_LORA, 2 * HEADS * HEAD_PAD).astype(BF16),
        "w_ukv": jnp.concatenate([w_uk.reshape(KV_LORA, HEADS * HEAD_PAD),
                                  w_ukv[:, :, NOPE:].reshape(KV_LORA, HEADS * V_DIM)], axis=1).astype(BF16),
        "g_qn": jnp.pad(mla_g_qnorm, ((0, 0), (0, HEAD_PAD - QK_DIM))),
        "g_kn": jnp.pad(mla_g_knorm, ((0, 0), (0, HEAD_PAD - QK_DIM))),
        "rope": _rope_tables(dec_seq, mla_g_qnorm, mla_g_knorm),
        "s5_b": bm, "s5_c": cm, "s5_a": a_rows,
        "s5_d": s5_d, "w_glu": s5_w_glu[0].astype(BF16), "b_glu": s5_b_glu, "w_out": ab_w_out[0].astype(BF16),
        "conv_w_in": conv_w_in[0].astype(BF16), "conv_w": conv_w[0].T, "conv_w_out": conv_w_out[0].astype(BF16),
    }

    cond = jnp.zeros((SUBLANES, D_MODEL), F32).at[0].set(c_ctx).at[1:1 + dec_batch].set(c)
    mods = _ada(cond, w_ada, b_ada).reshape(w_ada.shape[0], SUBLANES, N_MOD, D_MODEL)

    def trunk(x, seq_len, cond_of_tile, latent):
        rows = x.shape[0]
        res = _mixin(x, mods, norm_g, wts, cond_of_tile, seq_len, rotate=latent, want_cache=not latent)
        x1, q, k, v, u = res[:5]
        if latent:
            kr_pad = jnp.pad(cache_krope[:, 0].reshape(dec_batch * past, ROPE), ((0, 0), (NOPE, LANES - QK_DIM)))
            ctx = _ctx_kv(cache_ckv[:, 0].reshape(dec_batch * past, KV_LORA), kr_pad, wts)
            h0 = jnp.stack([_state_rows(state_ssm_re[:, 0]), _state_rows(state_ssm_im[:, 0])], axis=2)
            attn = _attention(q, k, v, seq_len, 256, ctx)
            yf, yb = _s5(u, wts, seq_len, h0=h0)
            extra = ()
        else:
            attn = _attention(q, k, v, seq_len, seq_len)
            yf, yb, fin = _s5(u, wts, seq_len, want_fin=True)
            extra = (res[5], res[6], fin)
        x2 = _mixout(x1, attn, u, yf, yb, mods, norm_g, wts, cond_of_tile)
        x3 = _ffn_call(x2, mods, norm_g, wts, cond_of_tile, 1, 0)
        return _conv_mixer(x3, mods, norm_g, wts, cond_of_tile, seq_len), extra

    tiles_per_sample = dec_seq // TM
    y_p, (ckv_p, kr_p, fin) = trunk(x_prompt.reshape(batch * seq, D_MODEL), seq, lambda i: 0, False)
    y_s, _ = trunk(x_sample.reshape(dec_batch * dec_seq, D_MODEL), dec_seq, lambda i: 1 + i // tiles_per_sample, True)
    return (y_p.reshape(batch, seq, D_MODEL), y_s.reshape(dec_batch, dec_seq, D_MODEL),
            ckv_p.reshape(batch, 1, seq, KV_LORA), kr_p.reshape(batch, 1, seq, ROPE),
            _state_unrows(fin[:, :, 0], batch), _state_unrows(fin[:, :, 1], batch))
```

```python
import functools
import math

import jax
import jax.numpy as jnp
import numpy as np
from jax import lax
from jax.experimental import pallas as pl
from jax.experimental.pallas import tpu as pltpu

F32 = jnp.float32
BF16 = jnp.bfloat16

LANES = 128
SUBLANES = 8
VMEM_LIMIT_BYTES = 60 * 1024 * 1024

D_MODEL = 1024
D_FF = 2816
N_MOD = 9
EPS = 1e-6
HEADS = 8
Q_LORA = 384
KV_LORA = 256
NOPE = 64
ROPE = 32
V_DIM = 64
QK_DIM = NOPE + ROPE
HEAD_PAD = LANES
ROPE_BASE = 10000.0
GRID_W = 64
S5_WIDTH = 512
S5_GROUP = 16
S5_GROUPS = 32
S5_STATE = 64
S5_LANES = S5_GROUPS * S5_STATE
S5_QUARTERS = 4
S5_QLANES = S5_LANES // S5_QUARTERS
S5_SLABS = S5_QLANES // LANES
S5_TC = 128
S5_PITCH = S5_TC + 4
S5_SEGMENTS = 1
LAMBDA_RE_MAX = -1e-4
CONV_K = 3

ATT_ROWS = 2 * SUBLANES
TM = 512
FF_CHUNKS = tuple((c, min(512, D_FF - c)) for c in range(0, D_FF, 512))

O_CKV = Q_LORA
O_U = Q_LORA + KV_LORA
O_KR = O_U + S5_WIDTH


def _dot(a, b):
    return jnp.dot(a, b, preferred_element_type=F32)


def _sigmoid(x):
    return 1.0 / (1.0 + jnp.exp(-x))


def _rms(x, g, n):
    ms = jnp.sum(x * x, axis=-1, keepdims=True) * (1.0 / n)
    return x * lax.rsqrt(ms + EPS) * g


def _modulate(x, g, shift, scale):
    return _rms(x, g, D_MODEL) * (1.0 + scale) + shift


def _ffn(x, shift, scale, gate, g, wg_ref, wu_ref, wd_ref):
    h = _modulate(x, g, shift, scale).astype(BF16)
    acc = jnp.zeros(x.shape, F32)
    for c0, cs in FF_CHUNKS:
        gt = _dot(h, wg_ref[:, c0:c0 + cs])
        up = _dot(h, wu_ref[:, c0:c0 + cs])
        act = ((gt * _sigmoid(gt)) * up).astype(BF16)
        acc = acc + _dot(act, wd_ref[c0:c0 + cs, :])
    return x + (0.5 * gate) * acc


def _head_inv_rms(xh):
    return lax.rsqrt(jnp.sum(xh * xh, axis=-1, keepdims=True) * (1.0 / QK_DIM) + EPS)


def _head_norm(xh, g):
    return xh * _head_inv_rms(xh) * g


def _gelu_tanh(x):
    return x * (0.5 * (1.0 + jnp.tanh(math.sqrt(2.0 / math.pi) * (x + 0.044715 * (x * x * x)))))


def _const_spec(shape, index):
    return pl.BlockSpec(shape, lambda *_: index, pipeline_mode=pl.Buffered(1))


def _params(n_axes):
    return pltpu.CompilerParams(dimension_semantics=("arbitrary",) * n_axes, vmem_limit_bytes=VMEM_LIMIT_BYTES)


def _ffn_specs(layer, sub):
    w = _const_spec((None, None, D_MODEL, D_FF), (layer, sub, 0, 0))
    return [w, w, _const_spec((None, None, D_FF, D_MODEL), (layer, sub, 0, 0))]


def _mods_spec(layer, cond_of_tile):
    return pl.BlockSpec((None, None, N_MOD, D_MODEL), lambda i: (layer, cond_of_tile(i), 0, 0))


def _normg_spec(layer):
    return _const_spec((None, 3, D_MODEL), (layer, 0, 0))


def _row_spec(width, rows=TM):
    return pl.BlockSpec((rows, width), lambda i: (i, 0))


class _Groups:
    def __init__(self, prompt_rows, prompt_seq, latent_rows, latent_seq):
        assert prompt_rows % TM == 0 and latent_seq % TM == 0 and TM % prompt_seq == 0
        assert prompt_rows % latent_seq == 0
        self.group_rows = (prompt_rows, latent_rows)
        self.seq = (prompt_seq, latent_seq)
        self.rows = prompt_rows + latent_rows
        self.prompt_tiles = prompt_rows // TM
        self.tiles = self.rows // TM
        self.latent_tiles_per_seq = latent_seq // TM

    def cond(self, i):
        return jnp.where(i < self.prompt_tiles, 0,
                         1 + jnp.maximum(i - self.prompt_tiles, 0) // self.latent_tiles_per_seq)

    def spec(self, width, group):
        n = self.prompt_tiles
        if group == 0:
            return pl.BlockSpec((TM, width), lambda i: (jnp.minimum(i, n - 1), 0))
        return pl.BlockSpec((TM, width), lambda i: (jnp.maximum(i - n, 0), 0))


def _ada_body(c_ref, w_ref, b_ref, o_ref):
    c = c_ref[...]
    s = (c * _sigmoid(c)).astype(BF16)
    o_ref[...] = _dot(s, w_ref[...].astype(BF16)) + b_ref[...]


def _ada(cond, w_ada, b_ada):
    depth, _, n = w_ada.shape
    tn = 1152
    return pl.pallas_call(
        _ada_body,
        out_shape=jax.ShapeDtypeStruct((depth, SUBLANES, n), F32),
        grid=(depth, n // tn),
        in_specs=[pl.BlockSpec((SUBLANES, D_MODEL), lambda l, j: (0, 0)),
                  pl.BlockSpec((None, D_MODEL, tn), lambda l, j: (l, 0, j)),
                  pl.BlockSpec((None, 1, tn), lambda l, j: (l, 0, j))],
        out_specs=pl.BlockSpec((None, SUBLANES, tn), lambda l, j: (l, 0, j)),
        compiler_params=_params(2),
        name="ada",
    )(cond, w_ada, b_ada.reshape(depth, 1, n))


def _mixin_body(n_prompt, *refs):
    (xp_ref, xs_ref, mods_ref, g_ref, wg_ref, wu_ref, wd_ref, win_ref, gql_ref, gkv_ref, wuq_ref, wukv_ref,
     gq_ref, gk_ref, rqc_ref, rqs_ref, rkc_ref, rks_ref,
     x1_ref, q_ref, k_ref, v_ref, up_ref, ckv_ref, kr_ref, ul_ref) = refs
    is_prompt = pl.program_id(0) < n_prompt
    mods = mods_ref[...]
    g3 = g_ref[...]
    x = jnp.where(is_prompt, xp_ref[...], xs_ref[...])
    x1 = _ffn(x, mods[0:1], mods[1:2], mods[2:3], g3[0:1], wg_ref, wu_ref, wd_ref)
    x1_ref[...] = x1
    hn = _modulate(x1, g3[1:2], mods[3:4], mods[4:5]).astype(BF16)
    proj = _dot(hn, win_ref[:, 0:O_KR + LANES])
    u = proj[:, O_U:O_KR]
    ckv = _rms(proj[:, O_CKV:O_U], gkv_ref[...], KV_LORA)
    krg = proj[:, O_KR:O_KR + LANES]
    qn = _rms(proj[:, 0:Q_LORA], gql_ref[...], Q_LORA).astype(BF16)
    kv = _dot(ckv.astype(BF16), wukv_ref[...])
    v_ref[...] = kv[:, HEADS * HEAD_PAD:].astype(BF16)
    lane = lax.broadcasted_iota(jnp.int32, (1, LANES), 1)
    kr_only = jnp.where((lane >= NOPE) & (lane < QK_DIM), krg, 0.0)
    gk = gk_ref[...]
    head = lambda h: slice(h * HEAD_PAD, (h + 1) * HEAD_PAD)

    @pl.when(is_prompt)
    def _():
        up_ref[...] = u
        ckv_ref[...] = ckv
        kr_ref[...] = krg[:, 0:ROPE]
        qraw = _dot(qn, wuq_ref[:, 0:HEADS * HEAD_PAD])
        gq = gq_ref[...]
        for h in range(HEADS):
            q_ref[:, head(h)] = _head_norm(qraw[:, head(h)], gq).astype(BF16)
            k_ref[:, head(h)] = _head_norm(kv[:, head(h)] + kr_only, gk).astype(BF16)

    @pl.when(jnp.logical_not(is_prompt))
    def _():
        ul_ref[...] = u
        qraw = _dot(qn, wuq_ref[...])
        kr_partner = _dot(hn, win_ref[:, O_KR + LANES:O_KR + 2 * LANES])
        k_rot = kr_only * rkc_ref[...] + kr_partner * rks_ref[...]
        for h in range(HEADS):
            qa = qraw[:, head(h)]
            qh = (qa * rqc_ref[...] + qraw[:, head(HEADS + h)] * rqs_ref[...]) * _head_inv_rms(qa)
            kh = (kv[:, head(h)] * gk + k_rot) * _head_inv_rms(kv[:, head(h)] + kr_only)
            q_ref[:, head(h)] = qh.astype(BF16)
            k_ref[:, head(h)] = kh.astype(BF16)


def _mixin(x_prompt, x_latent, mods, norm_g, wts, groups):
    n_prompt = groups.prompt_tiles
    w_in, w_uq = wts["w_in_rot"], wts["w_uq_rot"]
    in_specs = [groups.spec(D_MODEL, 0), groups.spec(D_MODEL, 1), _mods_spec(0, groups.cond), _normg_spec(0)]
    in_specs += _ffn_specs(0, 0) + [
        _const_spec(w_in.shape, (0, 0)),
        _const_spec((1, Q_LORA), (0, 0)),
        _const_spec((1, KV_LORA), (0, 0)),
        _const_spec(w_uq.shape, (0, 0)),
        _const_spec((KV_LORA, HEADS * HEAD_PAD + HEADS * V_DIM), (0, 0)),
        _const_spec((1, HEAD_PAD), (0, 0)),
        _const_spec((1, HEAD_PAD), (0, 0)),
    ] + [pl.BlockSpec((TM, LANES), lambda i: (jnp.maximum(i - n_prompt, 0) % groups.latent_tiles_per_seq, 0))] * 4
    args = [x_prompt, x_latent, mods, norm_g, wts["ffn_g"], wts["ffn_u"], wts["ffn_d"], w_in, wts["g_q_lat"],
            wts["g_kv_lat"], w_uq, wts["w_ukv"], wts["g_qn"], wts["g_kn"]] + list(wts["rope"])
    out_shape, out_specs = [], []
    for w, dt in ((D_MODEL, F32), (HEADS * HEAD_PAD, BF16), (HEADS * HEAD_PAD, BF16), (HEADS * V_DIM, BF16)):
        out_shape.append(jax.ShapeDtypeStruct((groups.rows, w), dt))
        out_specs.append(_row_spec(w))
    for g, w in ((0, S5_WIDTH), (0, KV_LORA), (0, ROPE), (1, S5_WIDTH)):
        out_shape.append(jax.ShapeDtypeStruct((groups.group_rows[g], w), F32))
        out_specs.append(groups.spec(w, g))
    return pl.pallas_call(
        functools.partial(_mixin_body, n_prompt),
        out_shape=out_shape, grid=(groups.tiles,), in_specs=in_specs, out_specs=out_specs,
        compiler_params=_params(1), name="ffn_mixin",
    )(*args)


def _ctx_body(ckv_ref, krp_ref, wukv_ref, gk_ref, k_ref, v_ref):
    kv = _dot(ckv_ref[...].astype(BF16), wukv_ref[...])
    v_ref[...] = kv[:, HEADS * HEAD_PAD:].astype(BF16)
    gk = gk_ref[...]
    krp = krp_ref[...]
    for h in range(HEADS):
        sl = slice(h * HEAD_PAD, (h + 1) * HEAD_PAD)
        k_ref[:, sl] = _head_norm(kv[:, sl] + krp, gk).astype(BF16)


def _ctx_kv(ckv, kr_padded, wts):
    rows = ckv.shape[0]
    return pl.pallas_call(
        _ctx_body,
        out_shape=[jax.ShapeDtypeStruct((rows, HEADS * HEAD_PAD), BF16),
                   jax.ShapeDtypeStruct((rows, HEADS * V_DIM), BF16)],
        grid=(rows // TM,),
        in_specs=[_row_spec(KV_LORA), _row_spec(LANES),
                  _const_spec((KV_LORA, HEADS * HEAD_PAD + HEADS * V_DIM), (0, 0)),
                  _const_spec((1, HEAD_PAD), (0, 0))],
        out_specs=[_row_spec(HEADS * HEAD_PAD), _row_spec(HEADS * V_DIM)],
        compiler_params=_params(1), name="ctx_kv",
    )(ckv, kr_padded, wts["w_ukv"], wts["g_kn"])


def _attn_body(n_kv, *refs):
    q_ref = refs[0]
    k_refs = refs[1:1 + n_kv]
    v_refs = refs[1 + n_kv:1 + 2 * n_kv]
    o_ref, s_ref, p_ref, r_ref = refs[1 + 2 * n_kv:]
    tq = q_ref.shape[0]
    c = (QK_DIM ** -0.5) * math.log2(math.e)
    lane = lax.broadcasted_iota(jnp.int32, (1, LANES), 1)
    spans = []
    for k_ref in k_refs:
        spans.append(slice(spans[-1].stop if spans else 0, (spans[-1].stop if spans else 0) + k_ref.shape[0]))
    for pair in range(HEADS // 2):
        vsl = slice(pair * LANES, (pair + 1) * LANES)
        outs = []
        for h in (2 * pair, 2 * pair + 1):
            slot = h % 2
            sl = slice(h * HEAD_PAD, (h + 1) * HEAD_PAD)
            q = q_ref[:, sl]
            for k_ref, span in zip(k_refs, spans):
                s_ref[slot, :, span] = lax.dot_general(q, k_ref[:, sl], (((1,), (1,)), ((), ())),
                                                       preferred_element_type=F32)
            for r0 in range(0, tq, ATT_ROWS):
                rows = slice(r0, r0 + ATT_ROWS)
                s = s_ref[slot, rows, :]
                e = jnp.exp2((s - jnp.max(s, axis=-1, keepdims=True)) * c)
                r_ref[slot, rows, :] = jnp.broadcast_to(1.0 / jnp.sum(e, axis=-1, keepdims=True), (ATT_ROWS, LANES))
                p_ref[slot, rows, :] = e.astype(BF16)
            o = functools.reduce(jnp.add, [_dot(p_ref[slot, :, span], v_ref[:, vsl])
                                           for v_ref, span in zip(v_refs, spans)])
            outs.append(o * r_ref[slot])
        o_ref[:, vsl] = jnp.where(lane < V_DIM, outs[0], outs[1]).astype(BF16)


def _attention(q, k, v, row0, rows, seq_len, tq, ctx=None):
    q_tiles = seq_len // tq
    q0, kv0 = row0 // tq, row0 // seq_len
    kv_spec = lambda w: pl.BlockSpec((seq_len, w), lambda b, t: (kv0 + b, 0))
    in_specs = [pl.BlockSpec((tq, HEADS * HEAD_PAD), lambda b, t: (q0 + b * q_tiles + t, 0))]
    ks, vs = [kv_spec(HEADS * HEAD_PAD)], [kv_spec(HEADS * V_DIM)]
    kargs, vargs = [k], [v]
    n_keys = seq_len
    if ctx is not None:
        k_c, v_c = ctx
        past = k_c.shape[0] // (rows // seq_len)
        ks.insert(0, pl.BlockSpec((past, HEADS * HEAD_PAD), lambda b, t: (b, 0)))
        vs.insert(0, pl.BlockSpec((past, HEADS * V_DIM), lambda b, t: (b, 0)))
        kargs.insert(0, k_c)
        vargs.insert(0, v_c)
        n_keys += past
    return pl.pallas_call(
        functools.partial(_attn_body, len(kargs)),
        out_shape=jax.ShapeDtypeStruct((rows, HEADS * V_DIM), BF16),
        grid=(rows // seq_len, q_tiles),
        in_specs=in_specs + ks + vs,
        out_specs=pl.BlockSpec((tq, HEADS * V_DIM), lambda b, t: (b * q_tiles + t, 0)),
        scratch_shapes=[pltpu.VMEM((2, tq, n_keys), F32), pltpu.VMEM((2, tq, n_keys), BF16),
                        pltpu.VMEM((2, tq, LANES), F32)],
        compiler_params=_params(2), name="attention",
    )(q, *kargs, *vargs)


def _s5_project_in(u_ref, rows, bm_ref, d, q, bu_ref):
    qsl = slice(q * LANES, (q + 1) * LANES)
    lhs = jnp.concatenate([u_ref[0, rows, qsl], u_ref[1, rows, qsl]], axis=0).astype(BF16)
    bu = _dot(lhs, bm_ref[d, q])
    for b in range(2):
        r = b * S5_QUARTERS + q
        for j in range(2 * S5_SLABS):
            bu_ref[d, j, pl.ds(r * S5_PITCH, S5_TC), :] = bu[b * S5_TC:(b + 1) * S5_TC, j * LANES:(j + 1) * LANES]


def _s5_scan(bu_ref, st_ref, a_ref, carry_ref, steps):
    nrow = 2 * S5_QUARTERS
    order = [(d, j) for d in range(2) for j in range(S5_SLABS)]
    lanes = lambda j: slice(j * LANES, (j + 1) * LANES)
    carry = {(d, j): (carry_ref[d, 0, :, lanes(j)], carry_ref[d, 1, :, lanes(j)]) for d, j in order}
    for i in steps:
        for d, j in order:
            t = i if d == 0 else S5_TC - 1 - i
            rows = pl.ds(t, nrow, stride=S5_PITCH)
            sr, si = carry[d, j]
            ar, ai = a_ref[d, 0, :, lanes(j)], a_ref[d, 1, :, lanes(j)]
            nr = ar * sr - ai * si + bu_ref[d, j, rows, :]
            ni = ar * si + ai * sr + bu_ref[d, S5_SLABS + j, rows, :]
            st_ref[d, j, rows, :] = nr
            st_ref[d, S5_SLABS + j, rows, :] = ni
            carry[d, j] = (nr, ni)
    for d, j in order:
        carry_ref[d, 0, :, lanes(j)], carry_ref[d, 1, :, lanes(j)] = carry[d, j]


def _s5_project_out(st_ref, d, q, cm_ref, y_ref, rows):
    qsl = slice(q * LANES, (q + 1) * LANES)
    lhs = jnp.concatenate(
        [jnp.concatenate([st_ref[d, j, pl.ds((b * S5_QUARTERS + q) * S5_PITCH, S5_TC), :]
                          for j in range(2 * S5_SLABS)], axis=-1) for b in range(2)], axis=0).astype(BF16)
    y = _dot(lhs, cm_ref[d, q])
    y_ref[0, rows, qsl] = y[0:S5_TC]
    y_ref[1, rows, qsl] = y[S5_TC:2 * S5_TC]


def _s5_body(n_super, has_h0, want_fin, *refs):
    refs = list(refs)
    uf_ref = refs.pop(0)
    ub_ref = refs.pop(0) if n_super > 1 else uf_ref
    bm_ref, cm_ref, a_ref = refs[:3]
    refs = refs[3:]
    h0_ref = refs.pop(0) if has_h0 else None
    yf_ref, yb_ref = refs[:2]
    refs = refs[2:]
    fin_ref = refs.pop(0) if want_fin else None
    bu0_ref, bu1_ref, st0_ref, st1_ref, carry_ref = refs
    s = pl.program_id(0)
    lo, hi = slice(0, S5_TC), slice(S5_TC, 2 * S5_TC)
    n_seg = S5_SEGMENTS
    seg_steps = S5_TC // n_seg
    seg_pairs = 2 * S5_QUARTERS // n_seg

    @pl.when(s == 0)
    def _():
        bu1_ref[...] = jnp.zeros(bu1_ref.shape, F32)
        st0_ref[...] = jnp.zeros(st0_ref.shape, F32)
        carry_ref[...] = jnp.zeros(carry_ref.shape, F32)

    def half(bu_w, bu_r, st_w, st_r, rows_f, rows_b, tag):
        for k in range(n_seg):
            @pl.when(s > -(1 + tag * n_seg + k))
            def _():
                io = ((uf_ref, rows_f, yf_ref), (ub_ref, rows_b, yb_ref))
                sub = seg_steps // seg_pairs
                for m in range(k * seg_pairs, (k + 1) * seg_pairs):
                    d, q = divmod(m, S5_QUARTERS)
                    _s5_project_in(io[d][0], io[d][1], bm_ref, d, q, bu_w)
                    _s5_scan(bu_r, st_w, a_ref, carry_ref, range(m * sub, (m + 1) * sub))
                    _s5_project_out(st_r, d, q, cm_ref, io[d][2], io[d][1])

    half(bu0_ref, bu1_ref, st1_ref, st0_ref, lo, hi, 0)

    @pl.when(s > -(1 + 2 * n_seg))
    def _():
        if want_fin:
            fin_ref[...] = carry_ref[...]
        start = h0_ref[...] if has_h0 else jnp.zeros(carry_ref.shape, F32)
        if n_super > 1:
            start = jnp.where(s % n_super == 0, start, carry_ref[...])
        carry_ref[...] = start

    half(bu1_ref, bu0_ref, st0_ref, st1_ref, hi, lo, 1)


def _s5(u, wts, seq_len, h0=None, want_fin=False):
    batch = u.shape[0] // seq_len
    n_super = seq_len // (2 * S5_TC)
    n_steps = (batch // 2) * n_super
    assert not (want_fin and n_super > 1)
    u3 = u.reshape(batch, seq_len, S5_WIDTH)
    nrow = 2 * S5_QUARTERS
    blk = (2, 2 * S5_TC, S5_WIDTH)

    def chunk_spec(delay, reverse):
        def index(s):
            s = jnp.clip(s - delay, 0, n_steps - 1)
            c = s % n_super
            return (s // n_super, n_super - 1 - c if reverse else c, 0)
        return pl.BlockSpec(blk, index)

    def state_spec(delay):
        return pl.BlockSpec((None, 2, 2, nrow, S5_QLANES),
                            lambda s: (jnp.clip(s - delay, 0, n_steps - 1) // n_super, 0, 0, 0, 0))

    in_specs, args = [chunk_spec(0, False)], [u3]
    if n_super > 1:
        in_specs.append(chunk_spec(0, True))
        args.append(u3)
    in_specs += [_const_spec((2, S5_QUARTERS, LANES, 2 * S5_QLANES), (0, 0, 0, 0)),
                 _const_spec((2, S5_QUARTERS, 2 * S5_QLANES, LANES), (0, 0, 0, 0)),
                 _const_spec((2, 2, nrow, S5_QLANES), (0, 0, 0, 0))]
    args += [wts["s5_b"], wts["s5_c"], wts["s5_a"]]
    if h0 is not None:
        in_specs.append(state_spec(0))
        args.append(h0)
    y_shape = jax.ShapeDtypeStruct((batch, seq_len, S5_WIDTH), F32)
    out_shape, out_specs = [y_shape, y_shape], [chunk_spec(1, False), chunk_spec(1, True)]
    if want_fin:
        out_shape.append(jax.ShapeDtypeStruct((batch // 2, 2, 2, nrow, S5_QLANES), F32))
        out_specs.append(state_spec(1))
    work = pltpu.VMEM((2, 2 * S5_SLABS, nrow * S5_PITCH, LANES), F32)
    return pl.pallas_call(
        functools.partial(_s5_body, n_super, h0 is not None, want_fin),
        out_shape=out_shape, grid=(n_steps + 1,), in_specs=in_specs, out_specs=out_specs,
        scratch_shapes=[work, work, work, work, pltpu.VMEM((2, 2, nrow, S5_QLANES), F32)],
        compiler_params=_params(1), name="s5",
    )(*args)


def _mixout_body(n_prompt, x_ref, attn_p, attn_l, u_p, u_l, yf_p, yf_l, yb_p, yb_l, mods_ref, g_ref, dsk_ref,
                 wglu_ref, bglu_ref, wout_ref, wg_ref, wu_ref, wd_ref, o_ref):
    is_prompt = pl.program_id(0) < n_prompt
    pick = lambda a, b: jnp.where(is_prompt, a[...], b[...])
    mods = mods_ref[...]
    g3 = g_ref[...]
    y = dsk_ref[...] * pick(u_p, u_l) + pick(yf_p, yf_l) + pick(yb_p, yb_l)
    z = _gelu_tanh(y)
    s5o = z * _sigmoid(_dot(z.astype(BF16), wglu_ref[...]) + bglu_ref[...])
    half = HEADS * V_DIM
    mix = _dot(pick(attn_p, attn_l), wout_ref[0:half, :]) + _dot(s5o.astype(BF16), wout_ref[half:, :])
    x2 = x_ref[...] + mods[5:6] * mix
    o_ref[...] = _ffn(x2, mods[6:7], mods[7:8], mods[8:9], g3[2:3], wg_ref, wu_ref, wd_ref)


def _mixout(x, attn, u, yf, yb, mods, norm_g, wts, groups):
    in_specs = [_row_spec(D_MODEL)]
    args = [x]
    for pair, w in ((attn, HEADS * V_DIM), (u, S5_WIDTH), (yf, S5_WIDTH), (yb, S5_WIDTH)):
        for g in range(2):
            in_specs.append(groups.spec(w, g))
            args.append(pair[g].reshape(groups.group_rows[g], w))
    in_specs += [_mods_spec(0, groups.cond), _normg_spec(0),
                 _const_spec((1, S5_WIDTH), (0, 0)), _const_spec((S5_WIDTH, S5_WIDTH), (0, 0)),
                 _const_spec((1, S5_WIDTH), (0, 0)), _const_spec((D_MODEL, D_MODEL), (0, 0))] + _ffn_specs(0, 1)
    args += [mods, norm_g, wts["s5_d"], wts["w_glu"], wts["b_glu"], wts["w_out"],
             wts["ffn_g"], wts["ffn_u"], wts["ffn_d"]]
    return pl.pallas_call(
        functools.partial(_mixout_body, groups.prompt_tiles),
        out_shape=jax.ShapeDtypeStruct((groups.rows, D_MODEL), F32),
        grid=(groups.tiles,), in_specs=in_specs, out_specs=_row_spec(D_MODEL),
        compiler_params=_params(1), name="mixout_ffn",
    )(*args)


def _ffn_body(sub, x_ref, mods_ref, g_ref, wg_ref, wu_ref, wd_ref, o_ref):
    mods = mods_ref[...]
    g3 = g_ref[...]
    m = 6 if sub else 0
    n = 2 if sub else 0
    o_ref[...] = _ffn(x_ref[...], mods[m:m + 1], mods[m + 1:m + 2], mods[m + 2:m + 3], g3[n:n + 1],
                      wg_ref, wu_ref, wd_ref)


def _ffn_call(x, mods, norm_g, wts, groups, layer, sub):
    return pl.pallas_call(
        functools.partial(_ffn_body, sub),
        out_shape=jax.ShapeDtypeStruct((groups.rows, D_MODEL), F32),
        grid=(groups.tiles,),
        in_specs=[_row_spec(D_MODEL), _mods_spec(layer, groups.cond), _normg_spec(layer)] + _ffn_specs(layer, sub),
        out_specs=_row_spec(D_MODEL),
        compiler_params=_params(1), name="ffn",
    )(x, mods, norm_g, wts["ffn_g"], wts["ffn_u"], wts["ffn_d"])


def _conv_body(n_prompt, seqs, x_ref, xp_ref, xn_ref, mods_ref, g_ref, cwin_ref, cw_ref, cwout_ref,
               wg_ref, wu_ref, wd_ref, op_ref, ol_ref):
    i = pl.program_id(0)
    is_prompt = i < n_prompt
    mods = mods_ref[...]
    g3 = g_ref[...]
    x = x_ref[...]
    n_ext = TM + 2 * SUBLANES
    xe = jnp.concatenate([xp_ref[...], x, xn_ref[...]], axis=0)
    hne = _modulate(xe, g3[1:2], mods[3:4], mods[4:5]).astype(BF16)
    pz = _dot(hne, cwin_ref[:, D_MODEL:3 * D_MODEL])
    z = pz[:, 0:D_MODEL] * pz[:, D_MODEL:2 * D_MODEL]
    gate_b = _dot(hne[SUBLANES:SUBLANES + TM], cwin_ref[:, 0:D_MODEL])
    main = slice(SUBLANES, SUBLANES + TM)
    row = i * TM + lax.broadcasted_iota(jnp.int32, (TM, 1), 0)
    pos = jnp.where(is_prompt, row % seqs[0], row % seqs[1])
    end = jnp.where(is_prompt, seqs[0] - 1, seqs[1] - 1)
    z_prev = jnp.where(pos == 0, 0.0, pltpu.roll(z, 1, 0)[main])
    z_next = jnp.where(pos == end, 0.0, pltpu.roll(z, n_ext - 1, 0)[main])
    cw = cw_ref[...]
    zc = z_prev * cw[0:1] + z[main] * cw[1:2] + z_next * cw[2:3]
    mix = _dot((gate_b * zc).astype(BF16), cwout_ref[...])
    x2 = x + mods[5:6] * mix
    out = _ffn(x2, mods[6:7], mods[7:8], mods[8:9], g3[2:3], wg_ref, wu_ref, wd_ref)

    @pl.when(is_prompt)
    def _():
        op_ref[...] = out

    @pl.when(jnp.logical_not(is_prompt))
    def _():
        ol_ref[...] = out


def _conv_mixer(x, mods, norm_g, wts, groups):
    per = TM // SUBLANES
    last = groups.rows // SUBLANES - 1
    halo = (SUBLANES, D_MODEL)
    in_specs = [_row_spec(D_MODEL),
                pl.BlockSpec(halo, lambda i: (jnp.maximum(i * per - 1, 0), 0)),
                pl.BlockSpec(halo, lambda i: (jnp.minimum((i + 1) * per, last), 0)),
                _mods_spec(1, groups.cond), _normg_spec(1),
                _const_spec((D_MODEL, 3 * D_MODEL), (0, 0)), _const_spec((CONV_K, D_MODEL), (0, 0)),
                _const_spec((D_MODEL, D_MODEL), (0, 0))] + _ffn_specs(1, 1)
    return pl.pallas_call(
        functools.partial(_conv_body, groups.prompt_tiles, groups.seq),
        out_shape=[jax.ShapeDtypeStruct((r, D_MODEL), F32) for r in groups.group_rows],
        grid=(groups.tiles,), in_specs=in_specs, out_specs=[groups.spec(D_MODEL, 0), groups.spec(D_MODEL, 1)],
        compiler_params=_params(1), name="conv_ffn",
    )(x, x, x, mods, norm_g, wts["conv_w_in"], wts["conv_w"], wts["conv_w_out"],
      wts["ffn_g"], wts["ffn_u"], wts["ffn_d"])


ROPE_PARTNER = np.concatenate([np.arange(8, 16), np.arange(0, 8), np.arange(24, 32), np.arange(16, 24)])


def _rope_tables(seq_len, g_q, g_k):
    t = np.arange(seq_len)
    quarter = ROPE // 4
    inv_freq = ROPE_BASE ** (-np.arange(quarter, dtype=np.float64) / quarter)
    cos = np.ones((seq_len, LANES))
    sin = np.zeros((seq_len, LANES))
    for base, pos in ((NOPE, t // GRID_W), (NOPE + ROPE // 2, t % GRID_W)):
        ang = pos[:, None].astype(np.float64) * inv_freq[None, :]
        cos[:, base:base + quarter] = np.cos(ang)
        cos[:, base + quarter:base + 2 * quarter] = np.cos(ang)
        sin[:, base:base + quarter] = -np.sin(ang)
        sin[:, base + quarter:base + 2 * quarter] = np.sin(ang)
    cos, sin = jnp.asarray(cos, F32), jnp.asarray(sin, F32)
    out = []
    for g in (g_q, g_k):
        g = g.reshape(QK_DIM)
        g_pad = jnp.pad(g, (0, HEAD_PAD - QK_DIM))
        g_partner = jnp.pad(g[NOPE:][ROPE_PARTNER], (NOPE, HEAD_PAD - QK_DIM))
        out += [g_pad[None, :] * cos, g_partner[None, :] * sin]
    return tuple(out)


def _s5_params(lam_re, lam_im, log_dt, b_re, b_im, c_re, c_im):
    dt = jnp.exp(log_dt)[..., None]
    lr = jnp.minimum(lam_re, LAMBDA_RE_MAX)
    li = lam_im
    mag = jnp.exp(lr * dt)
    ang = li * dt
    ab_re = mag * jnp.cos(ang)
    ab_im = mag * jnp.sin(ang)
    den = lr * lr + li * li
    nr = ab_re - 1.0
    ni = ab_im
    co_re = (nr * lr + ni * li) / den
    co_im = (ni * lr - nr * li) / den
    bb_re = co_re[..., None] * b_re - co_im[..., None] * b_im
    bb_im = co_re[..., None] * b_im + co_im[..., None] * b_re
    per_q = S5_GROUPS // S5_QUARTERS
    eye = jnp.eye(per_q, dtype=F32)

    def in_mat(bb):
        bb = bb.reshape(2, S5_QUARTERS, per_q, S5_STATE, S5_GROUP)
        return jnp.einsum("dqgpi,gh->dqgihp", bb, eye).reshape(2, S5_QUARTERS, LANES, S5_QLANES)

    def out_mat(cc):
        cc = cc.reshape(2, S5_QUARTERS, per_q, S5_GROUP, S5_STATE)
        return jnp.einsum("dqgip,gh->dqgphi", cc, eye).reshape(2, S5_QUARTERS, S5_QLANES, LANES)

    bm = jnp.concatenate([in_mat(bb_re), in_mat(bb_im)], axis=-1).astype(BF16)
    cm = jnp.concatenate([out_mat(c_re), -out_mat(c_im)], axis=-2).astype(BF16)

    def rows(a):
        a = a.reshape(2, 1, S5_QUARTERS, S5_QLANES)
        return jnp.broadcast_to(a, (2, 2, S5_QUARTERS, S5_QLANES)).reshape(2, 2 * S5_QUARTERS, S5_QLANES)

    return bm, cm, jnp.stack([rows(ab_re), rows(ab_im)], axis=1)


def _state_rows(s):
    b = s.shape[0]
    s = s.reshape(b // 2, 2, 2, S5_QUARTERS, S5_QLANES)
    return s.transpose(0, 2, 1, 3, 4).reshape(b // 2, 2, 2 * S5_QUARTERS, S5_QLANES)


def _state_unrows(s, batch):
    s = s.reshape(batch // 2, 2, 2, S5_QUARTERS, S5_QLANES).transpose(0, 2, 1, 3, 4)
    return s.reshape(batch, 1, 2, S5_GROUPS, S5_STATE)


def kernel(x_prompt, x_sample, cache_ckv, cache_krope, state_ssm_re, state_ssm_im, c, c_ctx, w_ada, b_ada, norm_g, ffn_w_gate, ffn_w_up, ffn_w_down, ab_w_in, mla_g_q_lat, mla_g_kv_lat, mla_w_uq, mla_w_ukv, mla_g_qnorm, mla_g_knorm, s5_lam_re, s5_lam_im, s5_log_dt, s5_b_re, s5_b_im, s5_c_re, s5_c_im, s5_d, s5_w_glu, s5_b_glu, ab_w_out, conv_w_in, conv_w, conv_w_out):
    batch, seq, _ = x_prompt.shape
    dec_batch, dec_seq, _ = x_sample.shape
    past = cache_ckv.shape[2]

    w_in = ab_w_in[0]
    kr_cols = w_in[:, Q_LORA + KV_LORA:Q_LORA + KV_LORA + ROPE]
    zeros = jnp.zeros_like(kr_cols)
    w_uq = jnp.pad(mla_w_uq[0].reshape(Q_LORA, HEADS, QK_DIM), ((0, 0), (0, 0), (0, HEAD_PAD - QK_DIM)))
    w_ukv = mla_w_ukv[0].reshape(KV_LORA, HEADS, NOPE + V_DIM)
    w_uk = jnp.pad(w_ukv[:, :, :NOPE], ((0, 0), (0, 0), (0, HEAD_PAD - NOPE)))
    bm, cm, a_rows = _s5_params(s5_lam_re[0], s5_lam_im[0], s5_log_dt[0], s5_b_re[0], s5_b_im[0],
                                s5_c_re[0], s5_c_im[0])
    w_in_cols = [w_in[:, :Q_LORA + KV_LORA], w_in[:, Q_LORA + KV_LORA + ROPE:], kr_cols, zeros, kr_cols, zeros]
    w_in_partner = [zeros, zeros, kr_cols[:, ROPE_PARTNER], zeros]
    w_uq3 = mla_w_uq[0].reshape(Q_LORA, HEADS, QK_DIM)
    w_uq_partner = jnp.pad(w_uq3[:, :, NOPE:][:, :, ROPE_PARTNER], ((0, 0), (0, 0), (NOPE, HEAD_PAD - QK_DIM)))
    wts = {
        "ffn_g": ffn_w_gate.astype(BF16), "ffn_u": ffn_w_up.astype(BF16), "ffn_d": ffn_w_down.astype(BF16),
        "w_in_rot": jnp.concatenate(w_in_cols + w_in_partner, axis=1).astype(BF16),
        "g_q_lat": mla_g_q_lat, "g_kv_lat": mla_g_kv_lat,
        "w_uq_rot": jnp.concatenate([w_uq, w_uq_partner], axis=1).reshape(Q_LORA, 2 * HEADS * HEAD_PAD).astype(BF16),
        "w_ukv": jnp.concatenate([w_uk.reshape(KV_LORA, HEADS * HEAD_PAD),
                                  w_ukv[:, :, NOPE:].reshape(KV_LORA, HEADS * V_DIM)], axis=1).astype(BF16),
        "g_qn": jnp.pad(mla_g_qnorm, ((0, 0), (0, HEAD_PAD - QK_DIM))),
        "g_kn": jnp.pad(mla_g_knorm, ((0, 0), (0, HEAD_PAD - QK_DIM))),
        "rope": _rope_tables(dec_seq, mla_g_qnorm, mla_g_knorm),
        "s5_b": bm, "s5_c": cm, "s5_a": a_rows,
        "s5_d": s5_d, "w_glu": s5_w_glu[0].astype(BF16), "b_glu": s5_b_glu, "w_out": ab_w_out[0].astype(BF16),
        "conv_w_in": conv_w_in[0].astype(BF16), "conv_w": conv_w[0].T, "conv_w_out": conv_w_out[0].astype(BF16),
    }

    cond = jnp.zeros((SUBLANES, D_MODEL), F32).at[0].set(c_ctx).at[1:1 + dec_batch].set(c)
    mods = _ada(cond, w_ada, b_ada).reshape(w_ada.shape[0], SUBLANES, N_MOD, D_MODEL)

    groups = _Groups(batch * seq, seq, dec_batch * dec_seq, dec_seq)
    x1, q, k, v, u_p, ckv_p, kr_p, u_l = _mixin(
        x_prompt.reshape(batch * seq, D_MODEL), x_sample.reshape(dec_batch * dec_seq, D_MODEL), mods, norm_g, wts,
        groups)
    kr_pad = jnp.pad(cache_krope[:, 0].reshape(dec_batch * past, ROPE), ((0, 0), (NOPE, LANES - QK_DIM)))
    ctx = _ctx_kv(cache_ckv[:, 0].reshape(dec_batch * past, KV_LORA), kr_pad, wts)
    h0 = jnp.stack([_state_rows(state_ssm_re[:, 0]), _state_rows(state_ssm_im[:, 0])], axis=2)
    attn_p = _attention(q, k, v, 0, batch * seq, seq, seq)
    attn_l = _attention(q, k, v, batch * seq, dec_batch * dec_seq, dec_seq, 256, ctx)
    yf_p, yb_p, fin = _s5(u_p, wts, seq, want_fin=True)
    yf_l, yb_l = _s5(u_l, wts, dec_seq, h0=h0)
    x2 = _mixout(x1, (attn_p, attn_l), (u_p, u_l), (yf_p, yf_l), (yb_p, yb_l), mods, norm_g, wts, groups)
    x3 = _ffn_call(x2, mods, norm_g, wts, groups, 1, 0)
    y_p, y_s = _conv_mixer(x3, mods, norm_g, wts, groups)
    return (y_p.reshape(batch, seq, D_MODEL), y_s.reshape(dec_batch, dec_seq, D_MODEL),
            ckv_p.reshape(batch, 1, seq, KV_LORA), kr_p.reshape(batch, 1, seq, ROPE),
            _state_unrows(fin[:, :, 0], batch), _state_unrows(fin[:, :, 1], batch))
```

```python
import functools
import math

import jax
import jax.numpy as jnp
import numpy as np
from jax import lax
from jax.experimental import pallas as pl
from jax.experimental.pallas import tpu as pltpu

F32 = jnp.float32
BF16 = jnp.bfloat16

LANES = 128
SUBLANES = 8
VMEM_LIMIT_BYTES = 60 * 1024 * 1024

D_MODEL = 1024
D_FF = 2816
N_MOD = 9
EPS = 1e-6
HEADS = 8
Q_LORA = 384
KV_LORA = 256
NOPE = 64
ROPE = 32
V_DIM = 64
QK_DIM = NOPE + ROPE
HEAD_PAD = LANES
ROPE_BASE = 10000.0
GRID_W = 64
S5_WIDTH = 512
S5_GROUP = 16
S5_GROUPS = 32
S5_STATE = 64
S5_LANES = S5_GROUPS * S5_STATE
S5_QUARTERS = 4
S5_QLANES = S5_LANES // S5_QUARTERS
S5_SLABS = S5_QLANES // LANES
S5_TC = 128
S5_PITCH = S5_TC + 4
S5_SEGMENTS = 1
LAMBDA_RE_MAX = -1e-4
CONV_K = 3

ATT_ROWS = 2 * SUBLANES
TM = 512
FF_CHUNKS = tuple((c, min(512, D_FF - c)) for c in range(0, D_FF, 512))

O_CKV = Q_LORA
O_U = Q_LORA + KV_LORA
O_KR = O_U + S5_WIDTH


def _dot(a, b):
    return jnp.dot(a, b, preferred_element_type=F32)


def _sigmoid(x):
    return 1.0 / (1.0 + jnp.exp(-x))


def _rms(x, g, n):
    ms = jnp.sum(x * x, axis=-1, keepdims=True) * (1.0 / n)
    return x * lax.rsqrt(ms + EPS) * g


def _modulate(x, g, shift, scale):
    return _rms(x, g, D_MODEL) * (1.0 + scale) + shift


def _ffn(x, shift, scale, gate, g, wg_ref, wu_ref, wd_ref):
    h = _modulate(x, g, shift, scale).astype(BF16)
    acc = jnp.zeros(x.shape, F32)
    for c0, cs in FF_CHUNKS:
        gt = _dot(h, wg_ref[:, c0:c0 + cs].astype(BF16))
        up = _dot(h, wu_ref[:, c0:c0 + cs].astype(BF16))
        act = ((gt * _sigmoid(gt)) * up).astype(BF16)
        acc = acc + _dot(act, wd_ref[c0:c0 + cs, :].astype(BF16))
    return x + (0.5 * gate) * acc


def _head_inv_rms(xh):
    return lax.rsqrt(jnp.sum(xh * xh, axis=-1, keepdims=True) * (1.0 / QK_DIM) + EPS)


def _head_norm(xh, g):
    return xh * _head_inv_rms(xh) * g


def _gelu_tanh(x):
    return x * (0.5 * (1.0 + jnp.tanh(math.sqrt(2.0 / math.pi) * (x + 0.044715 * (x * x * x)))))


def _const_spec(shape, index):
    return pl.BlockSpec(shape, lambda *_: index, pipeline_mode=pl.Buffered(1))


def _params(n_axes):
    return pltpu.CompilerParams(dimension_semantics=("arbitrary",) * n_axes, vmem_limit_bytes=VMEM_LIMIT_BYTES)


def _ffn_specs(weights, layer, sub):
    return [_const_spec(w.shape, (0, 0)) if w.ndim == 2 else
            _const_spec((None, None) + w.shape[2:], (layer, sub, 0, 0)) for w in weights]


def _mods_spec(layer, cond_of_tile):
    return pl.BlockSpec((None, None, N_MOD, D_MODEL), lambda i: (layer, cond_of_tile(i), 0, 0))


def _normg_spec(layer):
    return _const_spec((None, 3, D_MODEL), (layer, 0, 0))


def _row_spec(width, rows=TM):
    return pl.BlockSpec((rows, width), lambda i: (i, 0))


class _Groups:
    def __init__(self, prompt_rows, prompt_seq, latent_rows, latent_seq):
        assert prompt_rows % TM == 0 and latent_seq % TM == 0 and TM % prompt_seq == 0
        assert prompt_rows % latent_seq == 0
        self.group_rows = (prompt_rows, latent_rows)
        self.seq = (prompt_seq, latent_seq)
        self.rows = prompt_rows + latent_rows
        self.prompt_tiles = prompt_rows // TM
        self.tiles = self.rows // TM
        self.latent_tiles_per_seq = latent_seq // TM

    def cond(self, i):
        return jnp.where(i < self.prompt_tiles, 0,
                         1 + jnp.maximum(i - self.prompt_tiles, 0) // self.latent_tiles_per_seq)

    def spec(self, width, group):
        n = self.prompt_tiles
        if group == 0:
            return pl.BlockSpec((TM, width), lambda i: (jnp.minimum(i, n - 1), 0))
        return pl.BlockSpec((TM, width), lambda i: (jnp.maximum(i - n, 0), 0))


def _ada_body(c_ref, w_ref, b_ref, o_ref):
    c = c_ref[...]
    s = (c * _sigmoid(c)).astype(BF16)
    o_ref[...] = _dot(s, w_ref[...].astype(BF16)) + b_ref[...]


def _ada(cond, w_ada, b_ada):
    depth, _, n = w_ada.shape
    tn = 2304
    return pl.pallas_call(
        _ada_body,
        out_shape=jax.ShapeDtypeStruct((depth, SUBLANES, n), F32),
        grid=(depth, n // tn),
        in_specs=[pl.BlockSpec((SUBLANES, D_MODEL), lambda l, j: (0, 0)),
                  pl.BlockSpec((None, D_MODEL, tn), lambda l, j: (l, 0, j)),
                  pl.BlockSpec((None, 1, tn), lambda l, j: (l, 0, j))],
        out_specs=pl.BlockSpec((None, SUBLANES, tn), lambda l, j: (l, 0, j)),
        compiler_params=_params(2),
        name="ada",
    )(cond, w_ada, b_ada.reshape(depth, 1, n))


def _mixin_body(n_prompt, *refs):
    (xp_ref, xs_ref, mods_ref, g_ref, wg_ref, wu_ref, wd_ref, win_ref, gql_ref, gkv_ref, wuq_ref, wukv_ref,
     gq_ref, gk_ref, rqc_ref, rqs_ref, rkc_ref, rks_ref,
     x1_ref, q_ref, k_ref, v_ref, up_ref, ckv_ref, kr_ref, ul_ref) = refs
    is_prompt = pl.program_id(0) < n_prompt
    mods = mods_ref[...]
    g3 = g_ref[...]
    x = jnp.where(is_prompt, xp_ref[...], xs_ref[...])
    x1 = _ffn(x, mods[0:1], mods[1:2], mods[2:3], g3[0:1], wg_ref, wu_ref, wd_ref)
    x1_ref[...] = x1
    hn = _modulate(x1, g3[1:2], mods[3:4], mods[4:5]).astype(BF16)
    proj = _dot(hn, win_ref[:, 0:O_KR + LANES])
    u = proj[:, O_U:O_KR]
    ckv = _rms(proj[:, O_CKV:O_U], gkv_ref[...], KV_LORA)
    krg = proj[:, O_KR:O_KR + LANES]
    qn = _rms(proj[:, 0:Q_LORA], gql_ref[...], Q_LORA).astype(BF16)
    kv = _dot(ckv.astype(BF16), wukv_ref[...])
    v_ref[...] = kv[:, HEADS * HEAD_PAD:].astype(BF16)
    lane = lax.broadcasted_iota(jnp.int32, (1, LANES), 1)
    kr_only = jnp.where((lane >= NOPE) & (lane < QK_DIM), krg, 0.0)
    gk = gk_ref[...]
    head = lambda h: slice(h * HEAD_PAD, (h + 1) * HEAD_PAD)

    @pl.when(is_prompt)
    def _():
        up_ref[...] = u
        ckv_ref[...] = ckv
        kr_ref[...] = krg[:, 0:ROPE]
        qraw = _dot(qn, wuq_ref[:, 0:HEADS * HEAD_PAD])
        gq = gq_ref[...]
        for h in range(HEADS):
            q_ref[:, head(h)] = _head_norm(qraw[:, head(h)], gq).astype(BF16)
            k_ref[:, head(h)] = _head_norm(kv[:, head(h)] + kr_only, gk).astype(BF16)

    @pl.when(jnp.logical_not(is_prompt))
    def _():
        ul_ref[...] = u
        qraw = _dot(qn, wuq_ref[...])
        kr_partner = _dot(hn, win_ref[:, O_KR + LANES:O_KR + 2 * LANES])
        k_rot = kr_only * rkc_ref[...] + kr_partner * rks_ref[...]
        for h in range(HEADS):
            qa = qraw[:, head(h)]
            qh = (qa * rqc_ref[...] + qraw[:, head(HEADS + h)] * rqs_ref[...]) * _head_inv_rms(qa)
            kh = (kv[:, head(h)] * gk + k_rot) * _head_inv_rms(kv[:, head(h)] + kr_only)
            q_ref[:, head(h)] = qh.astype(BF16)
            k_ref[:, head(h)] = kh.astype(BF16)


def _mixin(x_prompt, x_latent, mods, norm_g, wts, groups):
    n_prompt = groups.prompt_tiles
    w_in, w_uq = wts["w_in_rot"], wts["w_uq_rot"]
    in_specs = [groups.spec(D_MODEL, 0), groups.spec(D_MODEL, 1), _mods_spec(0, groups.cond), _normg_spec(0)]
    in_specs += _ffn_specs(wts["ffn"][0, 0], 0, 0) + [
        _const_spec(w_in.shape, (0, 0)),
        _const_spec((1, Q_LORA), (0, 0)),
        _const_spec((1, KV_LORA), (0, 0)),
        _const_spec(w_uq.shape, (0, 0)),
        _const_spec((KV_LORA, HEADS * HEAD_PAD + HEADS * V_DIM), (0, 0)),
        _const_spec((1, HEAD_PAD), (0, 0)),
        _const_spec((1, HEAD_PAD), (0, 0)),
    ] + [pl.BlockSpec((TM, LANES), lambda i: (jnp.maximum(i - n_prompt, 0) % groups.latent_tiles_per_seq, 0))] * 4
    args = [x_prompt, x_latent, mods, norm_g, *wts["ffn"][0, 0], w_in, wts["g_q_lat"],
            wts["g_kv_lat"], w_uq, wts["w_ukv"], wts["g_qn"], wts["g_kn"]] + list(wts["rope"])
    out_shape, out_specs = [], []
    for w, dt in ((D_MODEL, F32), (HEADS * HEAD_PAD, BF16), (HEADS * HEAD_PAD, BF16), (HEADS * V_DIM, BF16)):
        out_shape.append(jax.ShapeDtypeStruct((groups.rows, w), dt))
        out_specs.append(_row_spec(w))
    for g, w in ((0, S5_WIDTH), (0, KV_LORA), (0, ROPE), (1, S5_WIDTH)):
        out_shape.append(jax.ShapeDtypeStruct((groups.group_rows[g], w), F32))
        out_specs.append(groups.spec(w, g))
    return pl.pallas_call(
        functools.partial(_mixin_body, n_prompt),
        out_shape=out_shape, grid=(groups.tiles,), in_specs=in_specs, out_specs=out_specs,
        compiler_params=_params(1), name="ffn_mixin",
    )(*args)


def _ctx_body(ckv_ref, krp_ref, wukv_ref, gk_ref, k_ref, v_ref):
    kv = _dot(ckv_ref[...].astype(BF16), wukv_ref[...])
    v_ref[...] = kv[:, HEADS * HEAD_PAD:].astype(BF16)
    gk = gk_ref[...]
    krp = krp_ref[...]
    for h in range(HEADS):
        sl = slice(h * HEAD_PAD, (h + 1) * HEAD_PAD)
        k_ref[:, sl] = _head_norm(kv[:, sl] + krp, gk).astype(BF16)


def _ctx_kv(ckv, kr_padded, wts):
    rows = ckv.shape[0]
    return pl.pallas_call(
        _ctx_body,
        out_shape=[jax.ShapeDtypeStruct((rows, HEADS * HEAD_PAD), BF16),
                   jax.ShapeDtypeStruct((rows, HEADS * V_DIM), BF16)],
        grid=(rows // TM,),
        in_specs=[_row_spec(KV_LORA), _row_spec(LANES),
                  _const_spec((KV_LORA, HEADS * HEAD_PAD + HEADS * V_DIM), (0, 0)),
                  _const_spec((1, HEAD_PAD), (0, 0))],
        out_specs=[_row_spec(HEADS * HEAD_PAD), _row_spec(HEADS * V_DIM)],
        compiler_params=_params(1), name="ctx_kv",
    )(ckv, kr_padded, wts["w_ukv"], wts["g_kn"])


def _attn_body(n_kv, n_seq, *refs):
    q_ref = refs[0]
    k_refs = refs[1:1 + n_kv]
    v_refs = refs[1 + n_kv:1 + 2 * n_kv]
    o_ref, s_ref, p_ref, r_ref = refs[1 + 2 * n_kv:]
    tq = q_ref.shape[0] // n_seq
    c = (QK_DIM ** -0.5) * math.log2(math.e)
    lane = lax.broadcasted_iota(jnp.int32, (1, LANES), 1)
    n_keys = [k_ref.shape[0] // n_seq for k_ref in k_refs]
    spans = [slice(sum(n_keys[:i]), sum(n_keys[:i + 1])) for i in range(n_kv)]
    for sq, pair in [(sq, pair) for sq in range(n_seq) for pair in range(HEADS // 2)]:
        qrows = slice(sq * tq, (sq + 1) * tq)
        krows = [slice(sq * n, (sq + 1) * n) for n in n_keys]
        wide = slice(pair // 2 * 2 * LANES, (pair // 2 + 1) * 2 * LANES)
        mine = slice(pair % 2 * LANES, (pair % 2 + 1) * LANES)
        vsl = slice(pair * LANES, (pair + 1) * LANES)
        outs = []
        for h in (2 * pair, 2 * pair + 1):
            slot = h % 2
            sl = slice(h * HEAD_PAD, (h + 1) * HEAD_PAD)
            q = q_ref[qrows, sl]
            for k_ref, kr, span in zip(k_refs, krows, spans):
                s_ref[slot, :, span] = lax.dot_general(q, k_ref[kr, sl], (((1,), (1,)), ((), ())),
                                                       preferred_element_type=F32)
            for r0 in range(0, tq, ATT_ROWS):
                rows = slice(r0, r0 + ATT_ROWS)
                s = s_ref[slot, rows, :]
                e = jnp.exp2((s - jnp.max(s, axis=-1, keepdims=True)) * c)
                r_ref[slot, rows, :] = jnp.broadcast_to(1.0 / jnp.sum(e, axis=-1, keepdims=True), (ATT_ROWS, LANES))
                p_ref[slot, rows, :] = e.astype(BF16)
            o = functools.reduce(jnp.add, [_dot(p_ref[slot, :, span], v_ref[kr, wide])
                                           for v_ref, kr, span in zip(v_refs, krows, spans)])
            outs.append(o[:, mine] * r_ref[slot])
        o_ref[qrows, vsl] = jnp.where(lane < V_DIM, outs[0], outs[1]).astype(BF16)


def _attention(q, k, v, row0, rows, seq_len, tq, ctx=None, n_seq=1):
    q_tiles = seq_len // tq
    n_seq = n_seq if q_tiles == 1 and ctx is None else 1
    q0, kv0 = row0 // (n_seq * tq), row0 // (n_seq * seq_len)
    kv_spec = lambda w: pl.BlockSpec((n_seq * seq_len, w), lambda b, t: (kv0 + b, 0))
    in_specs = [pl.BlockSpec((n_seq * tq, HEADS * HEAD_PAD), lambda b, t: (q0 + b * q_tiles + t, 0))]
    ks, vs = [kv_spec(HEADS * HEAD_PAD)], [kv_spec(HEADS * V_DIM)]
    kargs, vargs = [k], [v]
    n_keys = seq_len
    if ctx is not None:
        k_c, v_c = ctx
        past = k_c.shape[0] // (rows // seq_len)
        ks.insert(0, pl.BlockSpec((past, HEADS * HEAD_PAD), lambda b, t: (b, 0)))
        vs.insert(0, pl.BlockSpec((past, HEADS * V_DIM), lambda b, t: (b, 0)))
        kargs.insert(0, k_c)
        vargs.insert(0, v_c)
        n_keys += past
    return pl.pallas_call(
        functools.partial(_attn_body, len(kargs), n_seq),
        out_shape=jax.ShapeDtypeStruct((rows, HEADS * V_DIM), BF16),
        grid=(rows // (n_seq * seq_len), q_tiles),
        in_specs=in_specs + ks + vs,
        out_specs=pl.BlockSpec((n_seq * tq, HEADS * V_DIM), lambda b, t: (b * q_tiles + t, 0)),
        scratch_shapes=[pltpu.VMEM((2, tq, n_keys), F32), pltpu.VMEM((2, tq, n_keys), BF16),
                        pltpu.VMEM((2, tq, LANES), F32)],
        compiler_params=_params(2), name="attention",
    )(q, *kargs, *vargs)


def _s5_project_in(u_ref, rows, bm_ref, d, q, bu_ref):
    qsl = slice(q * LANES, (q + 1) * LANES)
    lhs = jnp.concatenate([u_ref[0, rows, qsl], u_ref[1, rows, qsl]], axis=0).astype(BF16)
    bu = _dot(lhs, bm_ref[d, q])
    for b in range(2):
        r = b * S5_QUARTERS + q
        for j in range(2 * S5_SLABS):
            bu_ref[d, j, pl.ds(r * S5_PITCH, S5_TC), :] = bu[b * S5_TC:(b + 1) * S5_TC, j * LANES:(j + 1) * LANES]


def _s5_scan(bu_ref, st_ref, a_ref, carry_ref, steps):
    nrow = 2 * S5_QUARTERS
    order = [(d, j) for d in range(2) for j in range(S5_SLABS)]
    lanes = lambda j: slice(j * LANES, (j + 1) * LANES)
    carry = {(d, j): (carry_ref[d, 0, :, lanes(j)], carry_ref[d, 1, :, lanes(j)]) for d, j in order}
    for i in steps:
        for d, j in order:
            t = i if d == 0 else S5_TC - 1 - i
            rows = pl.ds(t, nrow, stride=S5_PITCH)
            sr, si = carry[d, j]
            ar, ai = a_ref[d, 0, :, lanes(j)], a_ref[d, 1, :, lanes(j)]
            nr = ar * sr - ai * si + bu_ref[d, j, rows, :]
            ni = ar * si + ai * sr + bu_ref[d, S5_SLABS + j, rows, :]
            st_ref[d, j, rows, :] = nr
            st_ref[d, S5_SLABS + j, rows, :] = ni
            carry[d, j] = (nr, ni)
    for d, j in order:
        carry_ref[d, 0, :, lanes(j)], carry_ref[d, 1, :, lanes(j)] = carry[d, j]


def _s5_project_out(st_ref, d, q, cm_ref, y_ref, rows):
    qsl = slice(q * LANES, (q + 1) * LANES)
    lhs = jnp.concatenate(
        [jnp.concatenate([st_ref[d, j, pl.ds((b * S5_QUARTERS + q) * S5_PITCH, S5_TC), :]
                          for j in range(2 * S5_SLABS)], axis=-1) for b in range(2)], axis=0).astype(BF16)
    y = _dot(lhs, cm_ref[d, q])
    y_ref[0, rows, qsl] = y[0:S5_TC]
    y_ref[1, rows, qsl] = y[S5_TC:2 * S5_TC]


def _s5_body(n_super, has_h0, want_fin, *refs):
    refs = list(refs)
    uf_ref = refs.pop(0)
    ub_ref = refs.pop(0) if n_super > 1 else uf_ref
    bm_ref, cm_ref, a_ref = refs[:3]
    refs = refs[3:]
    h0_ref = refs.pop(0) if has_h0 else None
    yf_ref, yb_ref = refs[:2]
    refs = refs[2:]
    fin_ref = refs.pop(0) if want_fin else None
    bu0_ref, bu1_ref, st0_ref, st1_ref, carry_ref = refs
    s = pl.program_id(0)
    lo, hi = slice(0, S5_TC), slice(S5_TC, 2 * S5_TC)
    n_seg = S5_SEGMENTS
    seg_steps = S5_TC // n_seg
    seg_pairs = 2 * S5_QUARTERS // n_seg

    @pl.when(s == 0)
    def _():
        bu1_ref[...] = jnp.zeros(bu1_ref.shape, F32)
        st0_ref[...] = jnp.zeros(st0_ref.shape, F32)
        carry_ref[...] = jnp.zeros(carry_ref.shape, F32)

    def half(bu_w, bu_r, st_w, st_r, rows_f, rows_b, tag):
        for k in range(n_seg):
            @pl.when(s > -(1 + tag * n_seg + k))
            def _():
                io = ((uf_ref, rows_f, yf_ref), (ub_ref, rows_b, yb_ref))
                sub = seg_steps // seg_pairs
                for m in range(k * seg_pairs, (k + 1) * seg_pairs):
                    d, q = divmod(m, S5_QUARTERS)
                    _s5_project_in(io[d][0], io[d][1], bm_ref, d, q, bu_w)
                    _s5_scan(bu_r, st_w, a_ref, carry_ref, range(m * sub, (m + 1) * sub))
                    _s5_project_out(st_r, d, q, cm_ref, io[d][2], io[d][1])

    half(bu0_ref, bu1_ref, st1_ref, st0_ref, lo, hi, 0)

    @pl.when(s > -(1 + 2 * n_seg))
    def _():
        if want_fin:
            fin_ref[...] = carry_ref[...]
        start = h0_ref[...] if has_h0 else jnp.zeros(carry_ref.shape, F32)
        if n_super > 1:
            start = jnp.where(s % n_super == 0, start, carry_ref[...])
        carry_ref[...] = start

    half(bu1_ref, bu0_ref, st0_ref, st1_ref, hi, lo, 1)


def _s5(u, wts, seq_len, h0=None, want_fin=False):
    batch = u.shape[0] // seq_len
    n_super = seq_len // (2 * S5_TC)
    n_steps = (batch // 2) * n_super
    assert not (want_fin and n_super > 1)
    u3 = u.reshape(batch, seq_len, S5_WIDTH)
    nrow = 2 * S5_QUARTERS
    blk = (2, 2 * S5_TC, S5_WIDTH)

    def chunk_spec(delay, reverse):
        def index(s):
            s = jnp.clip(s - delay, 0, n_steps - 1)
            c = s % n_super
            return (s // n_super, n_super - 1 - c if reverse else c, 0)
        return pl.BlockSpec(blk, index)

    def state_spec(delay):
        return pl.BlockSpec((None, 2, 2, nrow, S5_QLANES),
                            lambda s: (jnp.clip(s - delay, 0, n_steps - 1) // n_super, 0, 0, 0, 0))

    in_specs, args = [chunk_spec(0, False)], [u3]
    if n_super > 1:
        in_specs.append(chunk_spec(0, True))
        args.append(u3)
    in_specs += [_const_spec((2, S5_QUARTERS, LANES, 2 * S5_QLANES), (0, 0, 0, 0)),
                 _const_spec((2, S5_QUARTERS, 2 * S5_QLANES, LANES), (0, 0, 0, 0)),
                 _const_spec((2, 2, nrow, S5_QLANES), (0, 0, 0, 0))]
    args += [wts["s5_b"], wts["s5_c"], wts["s5_a"]]
    if h0 is not None:
        in_specs.append(state_spec(0))
        args.append(h0)
    y_shape = jax.ShapeDtypeStruct((batch, seq_len, S5_WIDTH), F32)
    out_shape, out_specs = [y_shape, y_shape], [chunk_spec(1, False), chunk_spec(1, True)]
    if want_fin:
        out_shape.append(jax.ShapeDtypeStruct((batch // 2, 2, 2, nrow, S5_QLANES), F32))
        out_specs.append(state_spec(1))
    work = pltpu.VMEM((2, 2 * S5_SLABS, nrow * S5_PITCH, LANES), F32)
    return pl.pallas_call(
        functools.partial(_s5_body, n_super, h0 is not None, want_fin),
        out_shape=out_shape, grid=(n_steps + 1,), in_specs=in_specs, out_specs=out_specs,
        scratch_shapes=[work, work, work, work, pltpu.VMEM((2, 2, nrow, S5_QLANES), F32)],
        compiler_params=_params(1), name="s5",
    )(*args)


def _mixout_body(n_prompt, x_ref, attn_p, attn_l, u_p, u_l, yf_p, yf_l, yb_p, yb_l, mods_ref, g_ref, dsk_ref,
                 wglu_ref, bglu_ref, wout_ref, wg_ref, wu_ref, wd_ref, o_ref):
    is_prompt = pl.program_id(0) < n_prompt
    pick = lambda a, b: jnp.where(is_prompt, a[...], b[...])
    mods = mods_ref[...]
    g3 = g_ref[...]
    y = dsk_ref[...] * pick(u_p, u_l) + pick(yf_p, yf_l) + pick(yb_p, yb_l)
    z = _gelu_tanh(y)
    s5o = z * _sigmoid(_dot(z.astype(BF16), wglu_ref[...]) + bglu_ref[...])
    half = HEADS * V_DIM
    mix = _dot(pick(attn_p, attn_l), wout_ref[0:half, :]) + _dot(s5o.astype(BF16), wout_ref[half:, :])
    x2 = x_ref[...] + mods[5:6] * mix
    o_ref[...] = _ffn(x2, mods[6:7], mods[7:8], mods[8:9], g3[2:3], wg_ref, wu_ref, wd_ref)


def _mixout(x, attn, u, yf, yb, mods, norm_g, wts, groups):
    in_specs = [_row_spec(D_MODEL)]
    args = [x]
    for pair, w in ((attn, HEADS * V_DIM), (u, S5_WIDTH), (yf, S5_WIDTH), (yb, S5_WIDTH)):
        for g in range(2):
            in_specs.append(groups.spec(w, g))
            args.append(pair[g].reshape(groups.group_rows[g], w))
    in_specs += [_mods_spec(0, groups.cond), _normg_spec(0),
                 _const_spec((1, S5_WIDTH), (0, 0)), _const_spec((S5_WIDTH, S5_WIDTH), (0, 0)),
                 _const_spec((1, S5_WIDTH), (0, 0)), _const_spec((D_MODEL, D_MODEL), (0, 0))] + _ffn_specs(wts["ffn"][0, 1], 0, 1)
    args += [mods, norm_g, wts["s5_d"], wts["w_glu"], wts["b_glu"], wts["w_out"],
             *wts["ffn"][0, 1]]
    return pl.pallas_call(
        functools.partial(_mixout_body, groups.prompt_tiles),
        out_shape=jax.ShapeDtypeStruct((groups.rows, D_MODEL), F32),
        grid=(groups.tiles,), in_specs=in_specs, out_specs=_row_spec(D_MODEL),
        compiler_params=_params(1), name="mixout_ffn",
    )(*args)


def _ffn_body(sub, x_ref, mods_ref, g_ref, wg_ref, wu_ref, wd_ref, o_ref):
    mods = mods_ref[...]
    g3 = g_ref[...]
    m = 6 if sub else 0
    n = 2 if sub else 0
    o_ref[...] = _ffn(x_ref[...], mods[m:m + 1], mods[m + 1:m + 2], mods[m + 2:m + 3], g3[n:n + 1],
                      wg_ref, wu_ref, wd_ref)


def _ffn_call(x, mods, norm_g, wts, groups, layer, sub):
    return pl.pallas_call(
        functools.partial(_ffn_body, sub),
        out_shape=jax.ShapeDtypeStruct((groups.rows, D_MODEL), F32),
        grid=(groups.tiles,),
        in_specs=[_row_spec(D_MODEL), _mods_spec(layer, groups.cond), _normg_spec(layer)] + _ffn_specs(wts["ffn"][layer, sub], layer, sub),
        out_specs=_row_spec(D_MODEL),
        compiler_params=_params(1), name="ffn",
    )(x, mods, norm_g, *wts["ffn"][layer, sub])


def _conv_body(n_prompt, seqs, x_ref, xp_ref, xn_ref, mods_ref, g_ref, cwin_ref, cw_ref, cwout_ref,
               wg_ref, wu_ref, wd_ref, op_ref, ol_ref):
    i = pl.program_id(0)
    is_prompt = i < n_prompt
    mods = mods_ref[...]
    g3 = g_ref[...]
    x = x_ref[...]
    n_ext = TM + 2 * SUBLANES
    xe = jnp.concatenate([xp_ref[...], x, xn_ref[...]], axis=0)
    hne = _modulate(xe, g3[1:2], mods[3:4], mods[4:5]).astype(BF16)
    pz = _dot(hne, cwin_ref[:, D_MODEL:3 * D_MODEL])
    z = pz[:, 0:D_MODEL] * pz[:, D_MODEL:2 * D_MODEL]
    gate_b = _dot(hne[SUBLANES:SUBLANES + TM], cwin_ref[:, 0:D_MODEL])
    main = slice(SUBLANES, SUBLANES + TM)
    row = i * TM + lax.broadcasted_iota(jnp.int32, (TM, 1), 0)
    pos = jnp.where(is_prompt, row % seqs[0], row % seqs[1])
    end = jnp.where(is_prompt, seqs[0] - 1, seqs[1] - 1)
    z_prev = jnp.where(pos == 0, 0.0, pltpu.roll(z, 1, 0)[main])
    z_next = jnp.where(pos == end, 0.0, pltpu.roll(z, n_ext - 1, 0)[main])
    cw = cw_ref[...]
    zc = z_prev * cw[0:1] + z[main] * cw[1:2] + z_next * cw[2:3]
    mix = _dot((gate_b * zc).astype(BF16), cwout_ref[...])
    x2 = x + mods[5:6] * mix
    out = _ffn(x2, mods[6:7], mods[7:8], mods[8:9], g3[2:3], wg_ref, wu_ref, wd_ref)

    @pl.when(is_prompt)
    def _():
        op_ref[...] = out

    @pl.when(jnp.logical_not(is_prompt))
    def _():
        ol_ref[...] = out


def _conv_mixer(x, mods, norm_g, wts, groups):
    per = TM // SUBLANES
    last = groups.rows // SUBLANES - 1
    halo = (SUBLANES, D_MODEL)
    in_specs = [_row_spec(D_MODEL),
                pl.BlockSpec(halo, lambda i: (jnp.maximum(i * per - 1, 0), 0)),
                pl.BlockSpec(halo, lambda i: (jnp.minimum((i + 1) * per, last), 0)),
                _mods_spec(1, groups.cond), _normg_spec(1),
                _const_spec((D_MODEL, 3 * D_MODEL), (0, 0)), _const_spec((CONV_K, D_MODEL), (0, 0)),
                _const_spec((D_MODEL, D_MODEL), (0, 0))] + _ffn_specs(wts["ffn"][1, 1], 1, 1)
    return pl.pallas_call(
        functools.partial(_conv_body, groups.prompt_tiles, groups.seq),
        out_shape=[jax.ShapeDtypeStruct((r, D_MODEL), F32) for r in groups.group_rows],
        grid=(groups.tiles,), in_specs=in_specs, out_specs=[groups.spec(D_MODEL, 0), groups.spec(D_MODEL, 1)],
        compiler_params=_params(1), name="conv_ffn",
    )(x, x, x, mods, norm_g, wts["conv_w_in"], wts["conv_w"], wts["conv_w_out"],
      *wts["ffn"][1, 1])


ROPE_PARTNER = np.concatenate([np.arange(8, 16), np.arange(0, 8), np.arange(24, 32), np.arange(16, 24)])


def _rope_tables(seq_len, g_q, g_k):
    t = np.arange(seq_len)
    quarter = ROPE // 4
    inv_freq = ROPE_BASE ** (-np.arange(quarter, dtype=np.float64) / quarter)
    cos = np.ones((seq_len, LANES))
    sin = np.zeros((seq_len, LANES))
    for base, pos in ((NOPE, t // GRID_W), (NOPE + ROPE // 2, t % GRID_W)):
        ang = pos[:, None].astype(np.float64) * inv_freq[None, :]
        cos[:, base:base + quarter] = np.cos(ang)
        cos[:, base + quarter:base + 2 * quarter] = np.cos(ang)
        sin[:, base:base + quarter] = -np.sin(ang)
        sin[:, base + quarter:base + 2 * quarter] = np.sin(ang)
    cos, sin = jnp.asarray(cos, F32), jnp.asarray(sin, F32)
    out = []
    for g in (g_q, g_k):
        g = g.reshape(QK_DIM)
        g_pad = jnp.pad(g, (0, HEAD_PAD - QK_DIM))
        g_partner = jnp.pad(g[NOPE:][ROPE_PARTNER], (NOPE, HEAD_PAD - QK_DIM))
        out += [g_pad[None, :] * cos, g_partner[None, :] * sin]
    return tuple(out)


def _s5_params(lam_re, lam_im, log_dt, b_re, b_im, c_re, c_im):
    dt = jnp.exp(log_dt)[..., None]
    lr = jnp.minimum(lam_re, LAMBDA_RE_MAX)
    li = lam_im
    mag = jnp.exp(lr * dt)
    ang = li * dt
    ab_re = mag * jnp.cos(ang)
    ab_im = mag * jnp.sin(ang)
    den = lr * lr + li * li
    nr = ab_re - 1.0
    ni = ab_im
    co_re = (nr * lr + ni * li) / den
    co_im = (ni * lr - nr * li) / den
    bb_re = co_re[..., None] * b_re - co_im[..., None] * b_im
    bb_im = co_re[..., None] * b_im + co_im[..., None] * b_re
    per_q = S5_GROUPS // S5_QUARTERS
    eye = jnp.eye(per_q, dtype=F32)

    def in_mat(bb):
        bb = bb.reshape(2, S5_QUARTERS, per_q, S5_STATE, S5_GROUP)
        return jnp.einsum("dqgpi,gh->dqgihp", bb, eye).reshape(2, S5_QUARTERS, LANES, S5_QLANES)

    def out_mat(cc):
        cc = cc.reshape(2, S5_QUARTERS, per_q, S5_GROUP, S5_STATE)
        return jnp.einsum("dqgip,gh->dqgphi", cc, eye).reshape(2, S5_QUARTERS, S5_QLANES, LANES)

    bm = jnp.concatenate([in_mat(bb_re), in_mat(bb_im)], axis=-1).astype(BF16)
    cm = jnp.concatenate([out_mat(c_re), -out_mat(c_im)], axis=-2).astype(BF16)

    def rows(a):
        a = a.reshape(2, 1, S5_QUARTERS, S5_QLANES)
        return jnp.broadcast_to(a, (2, 2, S5_QUARTERS, S5_QLANES)).reshape(2, 2 * S5_QUARTERS, S5_QLANES)

    return bm, cm, jnp.stack([rows(ab_re), rows(ab_im)], axis=1)


def _state_rows(s):
    b = s.shape[0]
    s = s.reshape(b // 2, 2, 2, S5_QUARTERS, S5_QLANES)
    return s.transpose(0, 2, 1, 3, 4).reshape(b // 2, 2, 2 * S5_QUARTERS, S5_QLANES)


def _state_unrows(s, batch):
    s = s.reshape(batch // 2, 2, 2, S5_QUARTERS, S5_QLANES).transpose(0, 2, 1, 3, 4)
    return s.reshape(batch, 1, 2, S5_GROUPS, S5_STATE)


def kernel(x_prompt, x_sample, cache_ckv, cache_krope, state_ssm_re, state_ssm_im, c, c_ctx, w_ada, b_ada, norm_g, ffn_w_gate, ffn_w_up, ffn_w_down, ab_w_in, mla_g_q_lat, mla_g_kv_lat, mla_w_uq, mla_w_ukv, mla_g_qnorm, mla_g_knorm, s5_lam_re, s5_lam_im, s5_log_dt, s5_b_re, s5_b_im, s5_c_re, s5_c_im, s5_d, s5_w_glu, s5_b_glu, ab_w_out, conv_w_in, conv_w, conv_w_out):
    batch, seq, _ = x_prompt.shape
    dec_batch, dec_seq, _ = x_sample.shape
    past = cache_ckv.shape[2]

    w_in = ab_w_in[0]
    kr_cols = w_in[:, Q_LORA + KV_LORA:Q_LORA + KV_LORA + ROPE]
    zeros = jnp.zeros_like(kr_cols)
    w_uq = jnp.pad(mla_w_uq[0].reshape(Q_LORA, HEADS, QK_DIM), ((0, 0), (0, 0), (0, HEAD_PAD - QK_DIM)))
    w_ukv = mla_w_ukv[0].reshape(KV_LORA, HEADS, NOPE + V_DIM)
    w_uk = jnp.pad(w_ukv[:, :, :NOPE], ((0, 0), (0, 0), (0, HEAD_PAD - NOPE)))
    bm, cm, a_rows = _s5_params(s5_lam_re[0], s5_lam_im[0], s5_log_dt[0], s5_b_re[0], s5_b_im[0],
                                s5_c_re[0], s5_c_im[0])
    stacked = (ffn_w_gate, ffn_w_up, ffn_w_down)
    ffn = {(l, s): tuple(w[l, s].astype(BF16) for w in stacked) for l, s in ((0, 0), (0, 1), (1, 1))}
    ffn[1, 0] = stacked
    w_in_cols = [w_in[:, :Q_LORA + KV_LORA], w_in[:, Q_LORA + KV_LORA + ROPE:], kr_cols, zeros, kr_cols, zeros]
    w_in_partner = [zeros, zeros, kr_cols[:, ROPE_PARTNER], zeros]
    w_uq3 = mla_w_uq[0].reshape(Q_LORA, HEADS, QK_DIM)
    w_uq_partner = jnp.pad(w_uq3[:, :, NOPE:][:, :, ROPE_PARTNER], ((0, 0), (0, 0), (NOPE, HEAD_PAD - QK_DIM)))
    wts = {
        "ffn": ffn,
        "w_in_rot": jnp.concatenate(w_in_cols + w_in_partner, axis=1).astype(BF16),
        "g_q_lat": mla_g_q_lat, "g_kv_lat": mla_g_kv_lat,
        "w_uq_rot": jnp.concatenate([w_uq, w_uq_partner], axis=1).reshape(Q_LORA, 2 * HEADS * HEAD_PAD).astype(BF16),
        "w_ukv": jnp.concatenate([w_uk.reshape(KV_LORA, HEADS * HEAD_PAD),
                                  w_ukv[:, :, NOPE:].reshape(KV_LORA, HEADS * V_DIM)], axis=1).astype(BF16),
        "g_qn": jnp.pad(mla_g_qnorm, ((0, 0), (0, HEAD_PAD - QK_DIM))),
        "g_kn": jnp.pad(mla_g_knorm, ((0, 0), (0, HEAD_PAD - QK_DIM))),
        "rope": _rope_tables(dec_seq, mla_g_qnorm, mla_g_knorm),
        "s5_b": bm, "s5_c": cm, "s5_a": a_rows,
        "s5_d": s5_d, "w_glu": s5_w_glu[0].astype(BF16), "b_glu": s5_b_glu, "w_out": ab_w_out[0].astype(BF16),
        "conv_w_in": conv_w_in[0].astype(BF16), "conv_w": conv_w[0].T, "conv_w_out": conv_w_out[0].astype(BF16),
    }

    cond = jnp.zeros((SUBLANES, D_MODEL), F32).at[0].set(c_ctx).at[1:1 + dec_batch].set(c)
    mods = _ada(cond, w_ada, b_ada).reshape(w_ada.shape[0], SUBLANES, N_MOD, D_MODEL)

    groups = _Groups(batch * seq, seq, dec_batch * dec_seq, dec_seq)
    x1, q, k, v, u_p, ckv_p, kr_p, u_l = _mixin(
        x_prompt.reshape(batch * seq, D_MODEL), x_sample.reshape(dec_batch * dec_seq, D_MODEL), mods, norm_g, wts,
        groups)
    kr_pad = jnp.pad(cache_krope[:, 0].reshape(dec_batch * past, ROPE), ((0, 0), (NOPE, LANES - QK_DIM)))
    ctx = _ctx_kv(cache_ckv[:, 0].reshape(dec_batch * past, KV_LORA), kr_pad, wts)
    h0 = jnp.stack([_state_rows(state_ssm_re[:, 0]), _state_rows(state_ssm_im[:, 0])], axis=2)
    attn_p = _attention(q, k, v, 0, batch * seq, seq, seq, n_seq=2)
    attn_l = _attention(q, k, v, batch * seq, dec_batch * dec_seq, dec_seq, 256, ctx)
    yf_p, yb_p, fin = _s5(u_p, wts, seq, want_fin=True)
    yf_l, yb_l = _s5(u_l, wts, dec_seq, h0=h0)
    x2 = _mixout(x1, (attn_p, attn_l), (u_p, u_l), (yf_p, yf_l), (yb_p, yb_l), mods, norm_g, wts, groups)
    x3 = _ffn_call(x2, mods, norm_g, wts, groups, 1, 0)
    y_p, y_s = _conv_mixer(x3, mods, norm_g, wts, groups)
    return (y_p.reshape(batch, seq, D_MODEL), y_s.reshape(dec_batch, dec_seq, D_MODEL),
            ckv_p.reshape(batch, 1, seq, KV_LORA), kr_p.reshape(batch, 1, seq, ROPE),
            _state_unrows(fin[:, :, 0], batch), _state_unrows(fin[:, :, 1], batch))
```

```python
import functools
import math

import jax
import jax.numpy as jnp
import numpy as np
from jax import lax
from jax.experimental import pallas as pl
from jax.experimental.pallas import tpu as pltpu

F32 = jnp.float32
BF16 = jnp.bfloat16

LANES = 128
SUBLANES = 8
VMEM_LIMIT_BYTES = 60 * 1024 * 1024

D_MODEL = 1024
D_FF = 2816
N_MOD = 9
EPS = 1e-6
HEADS = 8
Q_LORA = 384
KV_LORA = 256
NOPE = 64
ROPE = 32
V_DIM = 64
QK_DIM = NOPE + ROPE
HEAD_PAD = LANES
ROPE_BASE = 10000.0
GRID_W = 64
S5_WIDTH = 512
S5_GROUP = 16
S5_GROUPS = 32
S5_STATE = 64
S5_LANES = S5_GROUPS * S5_STATE
S5_QUARTERS = 4
S5_QLANES = S5_LANES // S5_QUARTERS
S5_SLABS = S5_QLANES // LANES
S5_TC = 128
S5_PITCH = S5_TC + 4
S5_SEGMENTS = 1
LAMBDA_RE_MAX = -1e-4
CONV_K = 3

ATT_ROWS = 2 * SUBLANES
TM = 512
FF_CHUNKS = tuple((c, min(512, D_FF - c)) for c in range(0, D_FF, 512))

O_CKV = Q_LORA
O_U = Q_LORA + KV_LORA
O_KR = O_U + S5_WIDTH


def _dot(a, b):
    return jnp.dot(a, b, preferred_element_type=F32)


def _sigmoid(x):
    return 1.0 / (1.0 + jnp.exp(-x))


def _rms(x, g, n):
    ms = jnp.sum(x * x, axis=-1, keepdims=True) * (1.0 / n)
    return x * lax.rsqrt(ms + EPS) * g


def _modulate(x, g, shift, scale):
    return _rms(x, g, D_MODEL) * (1.0 + scale) + shift


def _ffn(x, shift, scale, gate, g, wg_ref, wu_ref, wd_ref):
    h = _modulate(x, g, shift, scale).astype(BF16)
    acc = jnp.zeros(x.shape, F32)
    for c0, cs in FF_CHUNKS:
        gt = _dot(h, wg_ref[:, c0:c0 + cs])
        up = _dot(h, wu_ref[:, c0:c0 + cs])
        act = ((gt * _sigmoid(gt)) * up).astype(BF16)
        acc = acc + _dot(act, wd_ref[c0:c0 + cs, :])
    return x + (0.5 * gate) * acc


def _head_inv_rms(xh):
    return lax.rsqrt(jnp.sum(xh * xh, axis=-1, keepdims=True) * (1.0 / QK_DIM) + EPS)


def _head_norm(xh, g):
    return xh * _head_inv_rms(xh) * g


def _gelu_tanh(x):
    return x * (0.5 * (1.0 + jnp.tanh(math.sqrt(2.0 / math.pi) * (x + 0.044715 * (x * x * x)))))


def _const_spec(shape, index):
    return pl.BlockSpec(shape, lambda *_: index, pipeline_mode=pl.Buffered(1))


def _params(n_axes):
    return pltpu.CompilerParams(dimension_semantics=("arbitrary",) * n_axes, vmem_limit_bytes=VMEM_LIMIT_BYTES)


def _ffn_specs(weights, layer, sub):
    return [_const_spec(w.shape, (0, 0)) if w.ndim == 2 else
            _const_spec((None, None) + w.shape[2:], (layer, sub, 0, 0)) for w in weights]


def _mods_spec(layer, cond_of_tile):
    return pl.BlockSpec((None, None, N_MOD, D_MODEL), lambda i: (layer, cond_of_tile(i), 0, 0))


def _normg_spec(layer):
    return _const_spec((None, 3, D_MODEL), (layer, 0, 0))


def _row_spec(width, rows=TM):
    return pl.BlockSpec((rows, width), lambda i: (i, 0))


class _Groups:
    def __init__(self, prompt_rows, prompt_seq, latent_rows, latent_seq):
        assert prompt_rows % TM == 0 and latent_seq % TM == 0 and TM % prompt_seq == 0
        assert prompt_rows % latent_seq == 0
        self.group_rows = (prompt_rows, latent_rows)
        self.seq = (prompt_seq, latent_seq)
        self.rows = prompt_rows + latent_rows
        self.prompt_tiles = prompt_rows // TM
        self.tiles = self.rows // TM
        self.latent_tiles_per_seq = latent_seq // TM

    def cond(self, i):
        return jnp.where(i < self.prompt_tiles, 0,
                         1 + jnp.maximum(i - self.prompt_tiles, 0) // self.latent_tiles_per_seq)

    def spec(self, width, group):
        n = self.prompt_tiles
        if group == 0:
            return pl.BlockSpec((TM, width), lambda i: (jnp.minimum(i, n - 1), 0))
        return pl.BlockSpec((TM, width), lambda i: (jnp.maximum(i - n, 0), 0))


def _ada_body(c_ref, w_ref, b_ref, o_ref):
    c = c_ref[...]
    s = (c * _sigmoid(c)).astype(BF16)
    o_ref[...] = _dot(s, w_ref[...].astype(BF16)) + b_ref[...]


def _ada(cond, w_ada, b_ada):
    depth, _, n = w_ada.shape
    tn = 2304
    return pl.pallas_call(
        _ada_body,
        out_shape=jax.ShapeDtypeStruct((depth, SUBLANES, n), F32),
        grid=(depth, n // tn),
        in_specs=[pl.BlockSpec((SUBLANES, D_MODEL), lambda l, j: (0, 0)),
                  pl.BlockSpec((None, D_MODEL, tn), lambda l, j: (l, 0, j)),
                  pl.BlockSpec((None, 1, tn), lambda l, j: (l, 0, j))],
        out_specs=pl.BlockSpec((None, SUBLANES, tn), lambda l, j: (l, 0, j)),
        compiler_params=_params(2),
        name="ada",
    )(cond, w_ada, b_ada.reshape(depth, 1, n))


def _mixin_body(n_prompt, *refs):
    (xp_ref, xs_ref, mods_ref, g_ref, wg_ref, wu_ref, wd_ref, win_ref, gql_ref, gkv_ref, wuq_ref, wukv_ref,
     gq_ref, gk_ref, rqc_ref, rqs_ref, rkc_ref, rks_ref,
     x1_ref, q_ref, k_ref, v_ref, up_ref, ckv_ref, kr_ref, ul_ref) = refs
    is_prompt = pl.program_id(0) < n_prompt
    mods = mods_ref[...]
    g3 = g_ref[...]
    x = jnp.where(is_prompt, xp_ref[...], xs_ref[...])
    x1 = _ffn(x, mods[0:1], mods[1:2], mods[2:3], g3[0:1], wg_ref, wu_ref, wd_ref)
    x1_ref[...] = x1
    hn = _modulate(x1, g3[1:2], mods[3:4], mods[4:5]).astype(BF16)
    proj = _dot(hn, win_ref[:, 0:O_KR + LANES])
    u = proj[:, O_U:O_KR]
    ckv = _rms(proj[:, O_CKV:O_U], gkv_ref[...], KV_LORA)
    krg = proj[:, O_KR:O_KR + LANES]
    qn = _rms(proj[:, 0:Q_LORA], gql_ref[...], Q_LORA).astype(BF16)
    kv = _dot(ckv.astype(BF16), wukv_ref[...])
    v_ref[...] = kv[:, HEADS * HEAD_PAD:].astype(BF16)
    lane = lax.broadcasted_iota(jnp.int32, (1, LANES), 1)
    kr_only = jnp.where((lane >= NOPE) & (lane < QK_DIM), krg, 0.0)
    gk = gk_ref[...]
    head = lambda h: slice(h * HEAD_PAD, (h + 1) * HEAD_PAD)

    @pl.when(is_prompt)
    def _():
        up_ref[...] = u
        ckv_ref[...] = ckv
        kr_ref[...] = krg[:, 0:ROPE]
        qraw = _dot(qn, wuq_ref[:, 0:HEADS * HEAD_PAD])
        gq = gq_ref[...]
        for h in range(HEADS):
            q_ref[:, head(h)] = _head_norm(qraw[:, head(h)], gq).astype(BF16)
            k_ref[:, head(h)] = _head_norm(kv[:, head(h)] + kr_only, gk).astype(BF16)

    @pl.when(jnp.logical_not(is_prompt))
    def _():
        ul_ref[...] = u
        qraw = _dot(qn, wuq_ref[...])
        kr_partner = _dot(hn, win_ref[:, O_KR + LANES:O_KR + 2 * LANES])
        k_rot = kr_only * rkc_ref[...] + kr_partner * rks_ref[...]
        for h in range(HEADS):
            qa = qraw[:, head(h)]
            qh = (qa * rqc_ref[...] + qraw[:, head(HEADS + h)] * rqs_ref[...]) * _head_inv_rms(qa)
            kh = (kv[:, head(h)] * gk + k_rot) * _head_inv_rms(kv[:, head(h)] + kr_only)
            q_ref[:, head(h)] = qh.astype(BF16)
            k_ref[:, head(h)] = kh.astype(BF16)


def _mixin(x_prompt, x_latent, mods, norm_g, wts, groups):
    n_prompt = groups.prompt_tiles
    w_in, w_uq = wts["w_in_rot"], wts["w_uq_rot"]
    in_specs = [groups.spec(D_MODEL, 0), groups.spec(D_MODEL, 1), _mods_spec(0, groups.cond), _normg_spec(0)]
    in_specs += _ffn_specs(wts["ffn"][0, 0], 0, 0) + [
        _const_spec(w_in.shape, (0, 0)),
        _const_spec((1, Q_LORA), (0, 0)),
        _const_spec((1, KV_LORA), (0, 0)),
        _const_spec(w_uq.shape, (0, 0)),
        _const_spec((KV_LORA, HEADS * HEAD_PAD + HEADS * V_DIM), (0, 0)),
        _const_spec((1, HEAD_PAD), (0, 0)),
        _const_spec((1, HEAD_PAD), (0, 0)),
    ] + [pl.BlockSpec((TM, LANES), lambda i: (jnp.maximum(i - n_prompt, 0) % groups.latent_tiles_per_seq, 0))] * 4
    args = [x_prompt, x_latent, mods, norm_g, *wts["ffn"][0, 0], w_in, wts["g_q_lat"],
            wts["g_kv_lat"], w_uq, wts["w_ukv"], wts["g_qn"], wts["g_kn"]] + list(wts["rope"])
    out_shape, out_specs = [], []
    for w, dt in ((D_MODEL, F32), (HEADS * HEAD_PAD, BF16), (HEADS * HEAD_PAD, BF16), (HEADS * V_DIM, BF16)):
        out_shape.append(jax.ShapeDtypeStruct((groups.rows, w), dt))
        out_specs.append(_row_spec(w))
    for g, w in ((0, S5_WIDTH), (0, KV_LORA), (0, ROPE), (1, S5_WIDTH)):
        out_shape.append(jax.ShapeDtypeStruct((groups.group_rows[g], w), F32))
        out_specs.append(groups.spec(w, g))
    return pl.pallas_call(
        functools.partial(_mixin_body, n_prompt),
        out_shape=out_shape, grid=(groups.tiles,), in_specs=in_specs, out_specs=out_specs,
        compiler_params=_params(1), name="ffn_mixin",
    )(*args)


def _ctx_body(ckv_ref, krp_ref, wukv_ref, gk_ref, k_ref, v_ref):
    kv = _dot(ckv_ref[...].astype(BF16), wukv_ref[...])
    v_ref[...] = kv[:, HEADS * HEAD_PAD:].astype(BF16)
    gk = gk_ref[...]
    krp = krp_ref[...]
    for h in range(HEADS):
        sl = slice(h * HEAD_PAD, (h + 1) * HEAD_PAD)
        k_ref[:, sl] = _head_norm(kv[:, sl] + krp, gk).astype(BF16)


def _ctx_kv(ckv, kr_padded, wts):
    rows = ckv.shape[0]
    return pl.pallas_call(
        _ctx_body,
        out_shape=[jax.ShapeDtypeStruct((rows, HEADS * HEAD_PAD), BF16),
                   jax.ShapeDtypeStruct((rows, HEADS * V_DIM), BF16)],
        grid=(rows // TM,),
        in_specs=[_row_spec(KV_LORA), _row_spec(LANES),
                  _const_spec((KV_LORA, HEADS * HEAD_PAD + HEADS * V_DIM), (0, 0)),
                  _const_spec((1, HEAD_PAD), (0, 0))],
        out_specs=[_row_spec(HEADS * HEAD_PAD), _row_spec(HEADS * V_DIM)],
        compiler_params=_params(1), name="ctx_kv",
    )(ckv, kr_padded, wts["w_ukv"], wts["g_kn"])


def _attn_body(n_kv, n_seq, *refs):
    q_ref = refs[0]
    k_refs = refs[1:1 + n_kv]
    v_refs = refs[1 + n_kv:1 + 2 * n_kv]
    o_ref, s_ref, p_ref, r_ref = refs[1 + 2 * n_kv:]
    tq = q_ref.shape[0] // n_seq
    c = (QK_DIM ** -0.5) * math.log2(math.e)
    lane = lax.broadcasted_iota(jnp.int32, (1, LANES), 1)
    n_keys = [k_ref.shape[0] // n_seq for k_ref in k_refs]
    spans = [slice(sum(n_keys[:i]), sum(n_keys[:i + 1])) for i in range(n_kv)]
    for sq, pair in [(sq, pair) for sq in range(n_seq) for pair in range(HEADS // 2)]:
        qrows = slice(sq * tq, (sq + 1) * tq)
        krows = [slice(sq * n, (sq + 1) * n) for n in n_keys]
        wide = slice(pair // 2 * 2 * LANES, (pair // 2 + 1) * 2 * LANES)
        mine = slice(pair % 2 * LANES, (pair % 2 + 1) * LANES)
        vsl = slice(pair * LANES, (pair + 1) * LANES)
        outs = []
        for h in (2 * pair, 2 * pair + 1):
            slot = h % 2
            sl = slice(h * HEAD_PAD, (h + 1) * HEAD_PAD)
            q = q_ref[qrows, sl]
            for k_ref, kr, span in zip(k_refs, krows, spans):
                s_ref[slot, :, span] = lax.dot_general(q, k_ref[kr, sl], (((1,), (1,)), ((), ())),
                                                       preferred_element_type=F32)
            for r0 in range(0, tq, ATT_ROWS):
                rows = slice(r0, r0 + ATT_ROWS)
                s = s_ref[slot, rows, :]
                e = jnp.exp2((s - jnp.max(s, axis=-1, keepdims=True)) * c)
                r_ref[slot, rows, :] = jnp.broadcast_to(1.0 / jnp.sum(e, axis=-1, keepdims=True), (ATT_ROWS, LANES))
                p_ref[slot, rows, :] = e.astype(BF16)
            o = functools.reduce(jnp.add, [_dot(p_ref[slot, :, span], v_ref[kr, wide])
                                           for v_ref, kr, span in zip(v_refs, krows, spans)])
            outs.append(o[:, mine] * r_ref[slot])
        o_ref[qrows, vsl] = jnp.where(lane < V_DIM, outs[0], outs[1]).astype(BF16)


def _attention(q, k, v, row0, rows, seq_len, tq, ctx=None, n_seq=1):
    q_tiles = seq_len // tq
    n_seq = n_seq if q_tiles == 1 and ctx is None else 1
    q0, kv0 = row0 // (n_seq * tq), row0 // (n_seq * seq_len)
    kv_spec = lambda w: pl.BlockSpec((n_seq * seq_len, w), lambda b, t: (kv0 + b, 0))
    in_specs = [pl.BlockSpec((n_seq * tq, HEADS * HEAD_PAD), lambda b, t: (q0 + b * q_tiles + t, 0))]
    ks, vs = [kv_spec(HEADS * HEAD_PAD)], [kv_spec(HEADS * V_DIM)]
    kargs, vargs = [k], [v]
    n_keys = seq_len
    if ctx is not None:
        k_c, v_c = ctx
        past = k_c.shape[0] // (rows // seq_len)
        ks.insert(0, pl.BlockSpec((past, HEADS * HEAD_PAD), lambda b, t: (b, 0)))
        vs.insert(0, pl.BlockSpec((past, HEADS * V_DIM), lambda b, t: (b, 0)))
        kargs.insert(0, k_c)
        vargs.insert(0, v_c)
        n_keys += past
    return pl.pallas_call(
        functools.partial(_attn_body, len(kargs), n_seq),
        out_shape=jax.ShapeDtypeStruct((rows, HEADS * V_DIM), BF16),
        grid=(rows // (n_seq * seq_len), q_tiles),
        in_specs=in_specs + ks + vs,
        out_specs=pl.BlockSpec((n_seq * tq, HEADS * V_DIM), lambda b, t: (b * q_tiles + t, 0)),
        scratch_shapes=[pltpu.VMEM((2, tq, n_keys), F32), pltpu.VMEM((2, tq, n_keys), BF16),
                        pltpu.VMEM((2, tq, LANES), F32)],
        compiler_params=_params(2), name="attention",
    )(q, *kargs, *vargs)


def _s5_project_in(u_ref, rows, bm_ref, d, q, bu_ref):
    qsl = slice(q * LANES, (q + 1) * LANES)
    lhs = jnp.concatenate([u_ref[0, rows, qsl], u_ref[1, rows, qsl]], axis=0).astype(BF16)
    bu = _dot(lhs, bm_ref[d, q])
    for b in range(2):
        r = b * S5_QUARTERS + q
        for j in range(2 * S5_SLABS):
            bu_ref[d, j, pl.ds(r * S5_PITCH, S5_TC), :] = bu[b * S5_TC:(b + 1) * S5_TC, j * LANES:(j + 1) * LANES]


def _s5_scan(bu_ref, st_ref, a_ref, carry_ref, steps):
    nrow = 2 * S5_QUARTERS
    order = [(d, j) for d in range(2) for j in range(S5_SLABS)]
    lanes = lambda j: slice(j * LANES, (j + 1) * LANES)
    carry = {(d, j): (carry_ref[d, 0, :, lanes(j)], carry_ref[d, 1, :, lanes(j)]) for d, j in order}
    for i in steps:
        for d, j in order:
            t = i if d == 0 else S5_TC - 1 - i
            rows = pl.ds(t, nrow, stride=S5_PITCH)
            sr, si = carry[d, j]
            ar, ai = a_ref[d, 0, :, lanes(j)], a_ref[d, 1, :, lanes(j)]
            nr = ar * sr - ai * si + bu_ref[d, j, rows, :]
            ni = ar * si + ai * sr + bu_ref[d, S5_SLABS + j, rows, :]
            st_ref[d, j, rows, :] = nr
            st_ref[d, S5_SLABS + j, rows, :] = ni
            carry[d, j] = (nr, ni)
    for d, j in order:
        carry_ref[d, 0, :, lanes(j)], carry_ref[d, 1, :, lanes(j)] = carry[d, j]


def _s5_project_out(st_ref, d, q, cm_ref, y_ref, rows):
    qsl = slice(q * LANES, (q + 1) * LANES)
    lhs = jnp.concatenate(
        [jnp.concatenate([st_ref[d, j, pl.ds((b * S5_QUARTERS + q) * S5_PITCH, S5_TC), :]
                          for j in range(2 * S5_SLABS)], axis=-1) for b in range(2)], axis=0).astype(BF16)
    y = _dot(lhs, cm_ref[d, q])
    y_ref[0, rows, qsl] = y[0:S5_TC]
    y_ref[1, rows, qsl] = y[S5_TC:2 * S5_TC]


def _s5_body(n_super, has_h0, want_fin, *refs):
    refs = list(refs)
    uf_ref = refs.pop(0)
    ub_ref = refs.pop(0) if n_super > 1 else uf_ref
    bm_ref, cm_ref, a_ref = refs[:3]
    refs = refs[3:]
    h0_ref = refs.pop(0) if has_h0 else None
    yf_ref, yb_ref = refs[:2]
    refs = refs[2:]
    fin_ref = refs.pop(0) if want_fin else None
    bu0_ref, bu1_ref, st0_ref, st1_ref, carry_ref = refs
    s = pl.program_id(0)
    lo, hi = slice(0, S5_TC), slice(S5_TC, 2 * S5_TC)
    n_seg = S5_SEGMENTS
    seg_steps = S5_TC // n_seg
    seg_pairs = 2 * S5_QUARTERS // n_seg

    @pl.when(s == 0)
    def _():
        bu1_ref[...] = jnp.zeros(bu1_ref.shape, F32)
        st0_ref[...] = jnp.zeros(st0_ref.shape, F32)
        carry_ref[...] = jnp.zeros(carry_ref.shape, F32)

    def half(bu_w, bu_r, st_w, st_r, rows_f, rows_b, tag):
        for k in range(n_seg):
            @pl.when(s > -(1 + tag * n_seg + k))
            def _():
                io = ((uf_ref, rows_f, yf_ref), (ub_ref, rows_b, yb_ref))
                sub = seg_steps // seg_pairs
                for m in range(k * seg_pairs, (k + 1) * seg_pairs):
                    d, q = divmod(m, S5_QUARTERS)
                    _s5_project_in(io[d][0], io[d][1], bm_ref, d, q, bu_w)
                    _s5_scan(bu_r, st_w, a_ref, carry_ref, range(m * sub, (m + 1) * sub))
                    _s5_project_out(st_r, d, q, cm_ref, io[d][2], io[d][1])

    half(bu0_ref, bu1_ref, st1_ref, st0_ref, lo, hi, 0)

    @pl.when(s > -(1 + 2 * n_seg))
    def _():
        if want_fin:
            fin_ref[...] = carry_ref[...]
        start = h0_ref[...] if has_h0 else jnp.zeros(carry_ref.shape, F32)
        if n_super > 1:
            start = jnp.where(s % n_super == 0, start, carry_ref[...])
        carry_ref[...] = start

    half(bu1_ref, bu0_ref, st0_ref, st1_ref, hi, lo, 1)


def _s5(u, wts, seq_len, h0=None, want_fin=False):
    batch = u.shape[0] // seq_len
    n_super = seq_len // (2 * S5_TC)
    n_steps = (batch // 2) * n_super
    assert not (want_fin and n_super > 1)
    u3 = u.reshape(batch, seq_len, S5_WIDTH)
    nrow = 2 * S5_QUARTERS
    blk = (2, 2 * S5_TC, S5_WIDTH)

    def chunk_spec(delay, reverse):
        def index(s):
            s = jnp.clip(s - delay, 0, n_steps - 1)
            c = s % n_super
            return (s // n_super, n_super - 1 - c if reverse else c, 0)
        return pl.BlockSpec(blk, index)

    def state_spec(delay):
        return pl.BlockSpec((None, 2, 2, nrow, S5_QLANES),
                            lambda s: (jnp.clip(s - delay, 0, n_steps - 1) // n_super, 0, 0, 0, 0))

    in_specs, args = [chunk_spec(0, False)], [u3]
    if n_super > 1:
        in_specs.append(chunk_spec(0, True))
        args.append(u3)
    in_specs += [_const_spec((2, S5_QUARTERS, LANES, 2 * S5_QLANES), (0, 0, 0, 0)),
                 _const_spec((2, S5_QUARTERS, 2 * S5_QLANES, LANES), (0, 0, 0, 0)),
                 _const_spec((2, 2, nrow, S5_QLANES), (0, 0, 0, 0))]
    args += [wts["s5_b"], wts["s5_c"], wts["s5_a"]]
    if h0 is not None:
        in_specs.append(state_spec(0))
        args.append(h0)
    y_shape = jax.ShapeDtypeStruct((batch, seq_len, S5_WIDTH), F32)
    out_shape, out_specs = [y_shape, y_shape], [chunk_spec(1, False), chunk_spec(1, True)]
    if want_fin:
        out_shape.append(jax.ShapeDtypeStruct((batch // 2, 2, 2, nrow, S5_QLANES), F32))
        out_specs.append(state_spec(1))
    work = pltpu.VMEM((2, 2 * S5_SLABS, nrow * S5_PITCH, LANES), F32)
    return pl.pallas_call(
        functools.partial(_s5_body, n_super, h0 is not None, want_fin),
        out_shape=out_shape, grid=(n_steps + 1,), in_specs=in_specs, out_specs=out_specs,
        scratch_shapes=[work, work, work, work, pltpu.VMEM((2, 2, nrow, S5_QLANES), F32)],
        compiler_params=_params(1), name="s5",
    )(*args)


def _mixout_body(n_prompt, x_ref, attn_p, attn_l, u_p, u_l, yf_p, yf_l, yb_p, yb_l, mods_ref, g_ref, dsk_ref,
                 wglu_ref, bglu_ref, wout_ref, wg_ref, wu_ref, wd_ref, o_ref):
    is_prompt = pl.program_id(0) < n_prompt
    pick = lambda a, b: jnp.where(is_prompt, a[...], b[...])
    mods = mods_ref[...]
    g3 = g_ref[...]
    y = dsk_ref[...] * pick(u_p, u_l) + pick(yf_p, yf_l) + pick(yb_p, yb_l)
    z = _gelu_tanh(y)
    s5o = z * _sigmoid(_dot(z.astype(BF16), wglu_ref[...]) + bglu_ref[...])
    half = HEADS * V_DIM
    mix = _dot(pick(attn_p, attn_l), wout_ref[0:half, :]) + _dot(s5o.astype(BF16), wout_ref[half:, :])
    x2 = x_ref[...] + mods[5:6] * mix
    o_ref[...] = _ffn(x2, mods[6:7], mods[7:8], mods[8:9], g3[2:3], wg_ref, wu_ref, wd_ref)


def _mixout(x, attn, u, yf, yb, mods, norm_g, wts, groups):
    in_specs = [_row_spec(D_MODEL)]
    args = [x]
    for pair, w in ((attn, HEADS * V_DIM), (u, S5_WIDTH), (yf, S5_WIDTH), (yb, S5_WIDTH)):
        for g in range(2):
            in_specs.append(groups.spec(w, g))
            args.append(pair[g].reshape(groups.group_rows[g], w))
    in_specs += [_mods_spec(0, groups.cond), _normg_spec(0),
                 _const_spec((1, S5_WIDTH), (0, 0)), _const_spec((S5_WIDTH, S5_WIDTH), (0, 0)),
                 _const_spec((1, S5_WIDTH), (0, 0)), _const_spec((D_MODEL, D_MODEL), (0, 0))] + _ffn_specs(wts["ffn"][0, 1], 0, 1)
    args += [mods, norm_g, wts["s5_d"], wts["w_glu"], wts["b_glu"], wts["w_out"],
             *wts["ffn"][0, 1]]
    return pl.pallas_call(
        functools.partial(_mixout_body, groups.prompt_tiles),
        out_shape=jax.ShapeDtypeStruct((groups.rows, D_MODEL), F32),
        grid=(groups.tiles,), in_specs=in_specs, out_specs=_row_spec(D_MODEL),
        compiler_params=_params(1), name="mixout_ffn",
    )(*args)


def _ffn_body(sub, x_ref, mods_ref, g_ref, wg_ref, wu_ref, wd_ref, o_ref):
    mods = mods_ref[...]
    g3 = g_ref[...]
    m = 6 if sub else 0
    n = 2 if sub else 0
    o_ref[...] = _ffn(x_ref[...], mods[m:m + 1], mods[m + 1:m + 2], mods[m + 2:m + 3], g3[n:n + 1],
                      wg_ref, wu_ref, wd_ref)


def _ffn_call(x, mods, norm_g, wts, groups, layer, sub):
    wide = 2
    assert groups.prompt_tiles % wide == 0 and groups.latent_tiles_per_seq % wide == 0
    weights = wts["ffn"][layer, sub]
    return pl.pallas_call(
        functools.partial(_ffn_body, sub),
        out_shape=jax.ShapeDtypeStruct((groups.rows, D_MODEL), F32),
        grid=(groups.tiles // wide,),
        in_specs=[_row_spec(D_MODEL, wide * TM), _mods_spec(layer, lambda i: groups.cond(wide * i)),
                  _normg_spec(layer)] + _ffn_specs(weights, layer, sub),
        out_specs=_row_spec(D_MODEL, wide * TM),
        compiler_params=_params(1), name="ffn",
    )(x, mods, norm_g, *weights)


def _conv_body(n_prompt, seqs, x_ref, xp_ref, xn_ref, mods_ref, g_ref, cwin_ref, cw_ref, cwout_ref,
               wg_ref, wu_ref, wd_ref, op_ref, ol_ref):
    i = pl.program_id(0)
    is_prompt = i < n_prompt
    mods = mods_ref[...]
    g3 = g_ref[...]
    x = x_ref[...]
    n_ext = TM + 2 * SUBLANES
    xe = jnp.concatenate([xp_ref[...], x, xn_ref[...]], axis=0)
    hne = _modulate(xe, g3[1:2], mods[3:4], mods[4:5]).astype(BF16)
    pz = _dot(hne, cwin_ref[:, D_MODEL:3 * D_MODEL])
    z = pz[:, 0:D_MODEL] * pz[:, D_MODEL:2 * D_MODEL]
    gate_b = _dot(hne[SUBLANES:SUBLANES + TM], cwin_ref[:, 0:D_MODEL])
    main = slice(SUBLANES, SUBLANES + TM)
    row = i * TM + lax.broadcasted_iota(jnp.int32, (TM, 1), 0)
    pos = jnp.where(is_prompt, row % seqs[0], row % seqs[1])
    end = jnp.where(is_prompt, seqs[0] - 1, seqs[1] - 1)
    z_prev = jnp.where(pos == 0, 0.0, pltpu.roll(z, 1, 0)[main])
    z_next = jnp.where(pos == end, 0.0, pltpu.roll(z, n_ext - 1, 0)[main])
    cw = cw_ref[...]
    zc = z_prev * cw[0:1] + z[main] * cw[1:2] + z_next * cw[2:3]
    mix = _dot((gate_b * zc).astype(BF16), cwout_ref[...])
    x2 = x + mods[5:6] * mix
    out = _ffn(x2, mods[6:7], mods[7:8], mods[8:9], g3[2:3], wg_ref, wu_ref, wd_ref)

    @pl.when(is_prompt)
    def _():
        op_ref[...] = out

    @pl.when(jnp.logical_not(is_prompt))
    def _():
        ol_ref[...] = out


def _conv_mixer(x, mods, norm_g, wts, groups):
    per = TM // SUBLANES
    last = groups.rows // SUBLANES - 1
    halo = (SUBLANES, D_MODEL)
    in_specs = [_row_spec(D_MODEL),
                pl.BlockSpec(halo, lambda i: (jnp.maximum(i * per - 1, 0), 0)),
                pl.BlockSpec(halo, lambda i: (jnp.minimum((i + 1) * per, last), 0)),
                _mods_spec(1, groups.cond), _normg_spec(1),
                _const_spec((D_MODEL, 3 * D_MODEL), (0, 0)), _const_spec((CONV_K, D_MODEL), (0, 0)),
                _const_spec((D_MODEL, D_MODEL), (0, 0))] + _ffn_specs(wts["ffn"][1, 1], 1, 1)
    return pl.pallas_call(
        functools.partial(_conv_body, groups.prompt_tiles, groups.seq),
        out_shape=[jax.ShapeDtypeStruct((r, D_MODEL), F32) for r in groups.group_rows],
        grid=(groups.tiles,), in_specs=in_specs, out_specs=[groups.spec(D_MODEL, 0), groups.spec(D_MODEL, 1)],
        compiler_params=_params(1), name="conv_ffn",
    )(x, x, x, mods, norm_g, wts["conv_w_in"], wts["conv_w"], wts["conv_w_out"],
      *wts["ffn"][1, 1])


ROPE_PARTNER = np.concatenate([np.arange(8, 16), np.arange(0, 8), np.arange(24, 32), np.arange(16, 24)])


def _rope_tables(seq_len, g_q, g_k):
    t = np.arange(seq_len)
    quarter = ROPE // 4
    inv_freq = ROPE_BASE ** (-np.arange(quarter, dtype=np.float64) / quarter)
    cos = np.ones((seq_len, LANES))
    sin = np.zeros((seq_len, LANES))
    for base, pos in ((NOPE, t // GRID_W), (NOPE + ROPE // 2, t % GRID_W)):
        ang = pos[:, None].astype(np.float64) * inv_freq[None, :]
        cos[:, base:base + quarter] = np.cos(ang)
        cos[:, base + quarter:base + 2 * quarter] = np.cos(ang)
        sin[:, base:base + quarter] = -np.sin(ang)
        sin[:, base + quarter:base + 2 * quarter] = np.sin(ang)
    cos, sin = jnp.asarray(cos, F32), jnp.asarray(sin, F32)
    out = []
    for g in (g_q, g_k):
        g = g.reshape(QK_DIM)
        g_pad = jnp.pad(g, (0, HEAD_PAD - QK_DIM))
        g_partner = jnp.pad(g[NOPE:][ROPE_PARTNER], (NOPE, HEAD_PAD - QK_DIM))
        out += [g_pad[None, :] * cos, g_partner[None, :] * sin]
    return tuple(out)


def _s5_params(lam_re, lam_im, log_dt, b_re, b_im, c_re, c_im):
    dt = jnp.exp(log_dt)[..., None]
    lr = jnp.minimum(lam_re, LAMBDA_RE_MAX)
    li = lam_im
    mag = jnp.exp(lr * dt)
    ang = li * dt
    ab_re = mag * jnp.cos(ang)
    ab_im = mag * jnp.sin(ang)
    den = lr * lr + li * li
    nr = ab_re - 1.0
    ni = ab_im
    co_re = (nr * lr + ni * li) / den
    co_im = (ni * lr - nr * li) / den
    bb_re = co_re[..., None] * b_re - co_im[..., None] * b_im
    bb_im = co_re[..., None] * b_im + co_im[..., None] * b_re
    per_q = S5_GROUPS // S5_QUARTERS
    eye = jnp.eye(per_q, dtype=F32)

    def in_mat(bb):
        bb = bb.reshape(2, S5_QUARTERS, per_q, S5_STATE, S5_GROUP)
        return jnp.einsum("dqgpi,gh->dqgihp", bb, eye).reshape(2, S5_QUARTERS, LANES, S5_QLANES)

    def out_mat(cc):
        cc = cc.reshape(2, S5_QUARTERS, per_q, S5_GROUP, S5_STATE)
        return jnp.einsum("dqgip,gh->dqgphi", cc, eye).reshape(2, S5_QUARTERS, S5_QLANES, LANES)

    bm = jnp.concatenate([in_mat(bb_re), in_mat(bb_im)], axis=-1).astype(BF16)
    cm = jnp.concatenate([out_mat(c_re), -out_mat(c_im)], axis=-2).astype(BF16)

    def rows(a):
        a = a.reshape(2, 1, S5_QUARTERS, S5_QLANES)
        return jnp.broadcast_to(a, (2, 2, S5_QUARTERS, S5_QLANES)).reshape(2, 2 * S5_QUARTERS, S5_QLANES)

    return bm, cm, jnp.stack([rows(ab_re), rows(ab_im)], axis=1)


def _state_rows(s):
    b = s.shape[0]
    s = s.reshape(b // 2, 2, 2, S5_QUARTERS, S5_QLANES)
    return s.transpose(0, 2, 1, 3, 4).reshape(b // 2, 2, 2 * S5_QUARTERS, S5_QLANES)


def _state_unrows(s, batch):
    s = s.reshape(batch // 2, 2, 2, S5_QUARTERS, S5_QLANES).transpose(0, 2, 1, 3, 4)
    return s.reshape(batch, 1, 2, S5_GROUPS, S5_STATE)


def kernel(x_prompt, x_sample, cache_ckv, cache_krope, state_ssm_re, state_ssm_im, c, c_ctx, w_ada, b_ada, norm_g, ffn_w_gate, ffn_w_up, ffn_w_down, ab_w_in, mla_g_q_lat, mla_g_kv_lat, mla_w_uq, mla_w_ukv, mla_g_qnorm, mla_g_knorm, s5_lam_re, s5_lam_im, s5_log_dt, s5_b_re, s5_b_im, s5_c_re, s5_c_im, s5_d, s5_w_glu, s5_b_glu, ab_w_out, conv_w_in, conv_w, conv_w_out):
    batch, seq, _ = x_prompt.shape
    dec_batch, dec_seq, _ = x_sample.shape
    past = cache_ckv.shape[2]

    w_in = ab_w_in[0]
    kr_cols = w_in[:, Q_LORA + KV_LORA:Q_LORA + KV_LORA + ROPE]
    zeros = jnp.zeros_like(kr_cols)
    w_uq = jnp.pad(mla_w_uq[0].reshape(Q_LORA, HEADS, QK_DIM), ((0, 0), (0, 0), (0, HEAD_PAD - QK_DIM)))
    w_ukv = mla_w_ukv[0].reshape(KV_LORA, HEADS, NOPE + V_DIM)
    w_uk = jnp.pad(w_ukv[:, :, :NOPE], ((0, 0), (0, 0), (0, HEAD_PAD - NOPE)))
    bm, cm, a_rows = _s5_params(s5_lam_re[0], s5_lam_im[0], s5_log_dt[0], s5_b_re[0], s5_b_im[0],
                                s5_c_re[0], s5_c_im[0])
    stacked = (ffn_w_gate.astype(BF16), ffn_w_up.astype(BF16), ffn_w_down.astype(BF16))
    ffn = {(l, s): stacked for l in range(2) for s in range(2)}
    w_in_cols = [w_in[:, :Q_LORA + KV_LORA], w_in[:, Q_LORA + KV_LORA + ROPE:], kr_cols, zeros, kr_cols, zeros]
    w_in_partner = [zeros, zeros, kr_cols[:, ROPE_PARTNER], zeros]
    w_uq3 = mla_w_uq[0].reshape(Q_LORA, HEADS, QK_DIM)
    w_uq_partner = jnp.pad(w_uq3[:, :, NOPE:][:, :, ROPE_PARTNER], ((0, 0), (0, 0), (NOPE, HEAD_PAD - QK_DIM)))
    wts = {
        "ffn": ffn,
        "w_in_rot": jnp.concatenate(w_in_cols + w_in_partner, axis=1).astype(BF16),
        "g_q_lat": mla_g_q_lat, "g_kv_lat": mla_g_kv_lat,
        "w_uq_rot": jnp.concatenate([w_uq, w_uq_partner], axis=1).reshape(Q_LORA, 2 * HEADS * HEAD_PAD).astype(BF16),
        "w_ukv": jnp.concatenate([w_uk.reshape(KV_LORA, HEADS * HEAD_PAD),
                                  w_ukv[:, :, NOPE:].reshape(KV_LORA, HEADS * V_DIM)], axis=1).astype(BF16),
        "g_qn": jnp.pad(mla_g_qnorm, ((0, 0), (0, HEAD_PAD - QK_DIM))),
        "g_kn": jnp.pad(mla_g_knorm, ((0, 0), (0, HEAD_PAD - QK_DIM))),
        "rope": _rope_tables(dec_seq, mla_g_qnorm, mla_g_knorm),
        "s5_b": bm, "s5_c": cm, "s5_a": a_rows,
        "s5_d": s5_d, "w_glu": s5_w_glu[0].astype(BF16), "b_glu": s5_b_glu, "w_out": ab_w_out[0].astype(BF16),
        "conv_w_in": conv_w_in[0].astype(BF16), "conv_w": conv_w[0].T, "conv_w_out": conv_w_out[0].astype(BF16),
    }

    cond = jnp.zeros((SUBLANES, D_MODEL), F32).at[0].set(c_ctx).at[1:1 + dec_batch].set(c)
    mods = _ada(cond, w_ada, b_ada).reshape(w_ada.shape[0], SUBLANES, N_MOD, D_MODEL)

    groups = _Groups(batch * seq, seq, dec_batch * dec_seq, dec_seq)
    x1, q, k, v, u_p, ckv_p, kr_p, u_l = _mixin(
        x_prompt.reshape(batch * seq, D_MODEL), x_sample.reshape(dec_batch * dec_seq, D_MODEL), mods, norm_g, wts,
        groups)
    kr_pad = jnp.pad(cache_krope[:, 0].reshape(dec_batch * past, ROPE), ((0, 0), (NOPE, LANES - QK_DIM)))
    ctx = _ctx_kv(cache_ckv[:, 0].reshape(dec_batch * past, KV_LORA), kr_pad, wts)
    h0 = jnp.stack([_state_rows(state_ssm_re[:, 0]), _state_rows(state_ssm_im[:, 0])], axis=2)
    attn_p = _attention(q, k, v, 0, batch * seq, seq, seq, n_seq=2)
    attn_l = _attention(q, k, v, batch * seq, dec_batch * dec_seq, dec_seq, 512, ctx)
    yf_p, yb_p, fin = _s5(u_p, wts, seq, want_fin=True)
    yf_l, yb_l = _s5(u_l, wts, dec_seq, h0=h0)
    x2 = _mixout(x1, (attn_p, attn_l), (u_p, u_l), (yf_p, yf_l), (yb_p, yb_l), mods, norm_g, wts, groups)
    x3 = _ffn_call(x2, mods, norm_g, wts, groups, 1, 0)
    y_p, y_s = _conv_mixer(x3, mods, norm_g, wts, groups)
    return (y_p.reshape(batch, seq, D_MODEL), y_s.reshape(dec_batch, dec_seq, D_MODEL),
            ckv_p.reshape(batch, 1, seq, KV_LORA), kr_p.reshape(batch, 1, seq, ROPE),
            _state_unrows(fin[:, :, 0], batch), _state_unrows(fin[:, :, 1], batch))
```

```python
import functools
import math

import jax
import jax.numpy as jnp
import numpy as np
from jax import lax
from jax.experimental import pallas as pl
from jax.experimental.pallas import tpu as pltpu

F32 = jnp.float32
BF16 = jnp.bfloat16

LANES = 128
SUBLANES = 8
VMEM_LIMIT_BYTES = 60 * 1024 * 1024

D_MODEL = 1024
D_FF = 2816
N_MOD = 9
EPS = 1e-6
HEADS = 8
Q_LORA = 384
KV_LORA = 256
NOPE = 64
ROPE = 32
V_DIM = 64
QK_DIM = NOPE + ROPE
HEAD_PAD = LANES
ROPE_BASE = 10000.0
GRID_W = 64
S5_WIDTH = 512
S5_GROUP = 16
S5_GROUPS = 32
S5_STATE = 64
S5_LANES = S5_GROUPS * S5_STATE
S5_QUARTERS = 4
S5_QLANES = S5_LANES // S5_QUARTERS
S5_SLABS = S5_QLANES // LANES
S5_TC = 128
S5_PITCH = S5_TC + 4
S5_SEGMENTS = 1
LAMBDA_RE_MAX = -1e-4
CONV_K = 3

ATT_ROWS = 2 * SUBLANES
TM = 512
FF_CHUNKS = tuple((c, min(512, D_FF - c)) for c in range(0, D_FF, 512))

O_CKV = Q_LORA
O_U = Q_LORA + KV_LORA
O_KR = O_U + S5_WIDTH


def _dot(a, b):
    return jnp.dot(a, b, preferred_element_type=F32)


def _sigmoid(x):
    return 1.0 / (1.0 + jnp.exp(-x))


def _rms(x, g, n):
    ms = jnp.sum(x * x, axis=-1, keepdims=True) * (1.0 / n)
    return x * lax.rsqrt(ms + EPS) * g


def _modulate(x, g, shift, scale):
    return _rms(x, g, D_MODEL) * (1.0 + scale) + shift


def _ffn(x, shift, scale, gate, g, wg_ref, wu_ref, wd_ref):
    h = _modulate(x, g, shift, scale).astype(BF16)
    acc = jnp.zeros(x.shape, F32)
    for c0, cs in FF_CHUNKS:
        gt = _dot(h, wg_ref[:, c0:c0 + cs])
        up = _dot(h, wu_ref[:, c0:c0 + cs])
        act = ((gt * _sigmoid(gt)) * up).astype(BF16)
        acc = acc + _dot(act, wd_ref[c0:c0 + cs, :])
    return x + (0.5 * gate) * acc


def _head_inv_rms(xh):
    return lax.rsqrt(jnp.sum(xh * xh, axis=-1, keepdims=True) * (1.0 / QK_DIM) + EPS)


def _head_norm(xh, g):
    return xh * _head_inv_rms(xh) * g


def _gelu_tanh(x):
    return x * (0.5 * (1.0 + jnp.tanh(math.sqrt(2.0 / math.pi) * (x + 0.044715 * (x * x * x)))))


def _const_spec(shape, index):
    return pl.BlockSpec(shape, lambda *_: index, pipeline_mode=pl.Buffered(1))


def _params(n_axes):
    return pltpu.CompilerParams(dimension_semantics=("arbitrary",) * n_axes, vmem_limit_bytes=VMEM_LIMIT_BYTES)


def _ffn_specs(weights, layer, sub):
    return [_const_spec(w.shape, (0, 0)) if w.ndim == 2 else
            _const_spec((None, None) + w.shape[2:], (layer, sub, 0, 0)) for w in weights]


def _mods_spec(layer, cond_of_tile):
    return pl.BlockSpec((None, None, N_MOD, D_MODEL), lambda i: (layer, cond_of_tile(i), 0, 0))


def _normg_spec(layer):
    return _const_spec((None, 3, D_MODEL), (layer, 0, 0))


def _row_spec(width, rows=TM):
    return pl.BlockSpec((rows, width), lambda i: (i, 0))


class _Groups:
    def __init__(self, prompt_rows, prompt_seq, latent_rows, latent_seq):
        assert prompt_rows % TM == 0 and latent_seq % TM == 0 and TM % prompt_seq == 0
        assert prompt_rows % latent_seq == 0
        self.group_rows = (prompt_rows, latent_rows)
        self.seq = (prompt_seq, latent_seq)
        self.rows = prompt_rows + latent_rows
        self.prompt_tiles = prompt_rows // TM
        self.tiles = self.rows // TM
        self.latent_tiles_per_seq = latent_seq // TM

    def cond(self, i):
        return jnp.where(i < self.prompt_tiles, 0,
                         1 + jnp.maximum(i - self.prompt_tiles, 0) // self.latent_tiles_per_seq)

    def spec(self, width, group):
        n = self.prompt_tiles
        if group == 0:
            return pl.BlockSpec((TM, width), lambda i: (jnp.minimum(i, n - 1), 0))
        return pl.BlockSpec((TM, width), lambda i: (jnp.maximum(i - n, 0), 0))


def _ada_body(c_ref, w_ref, b_ref, o_ref):
    c = c_ref[...]
    s = (c * _sigmoid(c)).astype(BF16)
    o_ref[...] = _dot(s, w_ref[...].astype(BF16)) + b_ref[...]


def _ada(cond, w_ada, b_ada):
    depth, _, n = w_ada.shape
    tn = 2304
    return pl.pallas_call(
        _ada_body,
        out_shape=jax.ShapeDtypeStruct((depth, SUBLANES, n), F32),
        grid=(depth, n // tn),
        in_specs=[pl.BlockSpec((SUBLANES, D_MODEL), lambda l, j: (0, 0)),
                  pl.BlockSpec((None, D_MODEL, tn), lambda l, j: (l, 0, j)),
                  pl.BlockSpec((None, 1, tn), lambda l, j: (l, 0, j))],
        out_specs=pl.BlockSpec((None, SUBLANES, tn), lambda l, j: (l, 0, j)),
        compiler_params=_params(2),
        name="ada",
    )(cond, w_ada, b_ada.reshape(depth, 1, n))


def _mixin_body(n_prompt, *refs):
    (xp_ref, xs_ref, mods_ref, g_ref, wg_ref, wu_ref, wd_ref, win_ref, gql_ref, gkv_ref, wuq_ref, wukv_ref,
     gq_ref, gk_ref, rqc_ref, rqs_ref, rkc_ref, rks_ref,
     x1_ref, q_ref, k_ref, v_ref, u_ref, ckv_ref, kr_ref) = refs
    is_prompt = pl.program_id(0) < n_prompt
    mods = mods_ref[...]
    g3 = g_ref[...]
    x = jnp.where(is_prompt, xp_ref[...], xs_ref[...])
    x1 = _ffn(x, mods[0:1], mods[1:2], mods[2:3], g3[0:1], wg_ref, wu_ref, wd_ref)
    x1_ref[...] = x1
    hn = _modulate(x1, g3[1:2], mods[3:4], mods[4:5]).astype(BF16)
    proj = _dot(hn, win_ref[:, 0:O_KR + LANES])
    u_ref[...] = proj[:, O_U:O_KR]
    ckv = _rms(proj[:, O_CKV:O_U], gkv_ref[...], KV_LORA)
    krg = proj[:, O_KR:O_KR + LANES]
    qn = _rms(proj[:, 0:Q_LORA], gql_ref[...], Q_LORA).astype(BF16)
    kv = _dot(ckv.astype(BF16), wukv_ref[...])
    v_ref[...] = kv[:, HEADS * HEAD_PAD:].astype(BF16)
    lane = lax.broadcasted_iota(jnp.int32, (1, LANES), 1)
    kr_only = jnp.where((lane >= NOPE) & (lane < QK_DIM), krg, 0.0)
    gk = gk_ref[...]
    head = lambda h: slice(h * HEAD_PAD, (h + 1) * HEAD_PAD)

    @pl.when(is_prompt)
    def _():
        ckv_ref[...] = ckv
        kr_ref[...] = krg[:, 0:ROPE]
        qraw = _dot(qn, wuq_ref[:, 0:HEADS * HEAD_PAD])
        gq = gq_ref[...]
        for h in range(HEADS):
            q_ref[:, head(h)] = _head_norm(qraw[:, head(h)], gq).astype(BF16)
            k_ref[:, head(h)] = _head_norm(kv[:, head(h)] + kr_only, gk).astype(BF16)

    @pl.when(jnp.logical_not(is_prompt))
    def _():
        qraw = _dot(qn, wuq_ref[...])
        kr_partner = _dot(hn, win_ref[:, O_KR + LANES:O_KR + 2 * LANES])
        k_rot = kr_only * rkc_ref[...] + kr_partner * rks_ref[...]
        for h in range(HEADS):
            qa = qraw[:, head(h)]
            qh = (qa * rqc_ref[...] + qraw[:, head(HEADS + h)] * rqs_ref[...]) * _head_inv_rms(qa)
            kh = (kv[:, head(h)] * gk + k_rot) * _head_inv_rms(kv[:, head(h)] + kr_only)
            q_ref[:, head(h)] = qh.astype(BF16)
            k_ref[:, head(h)] = kh.astype(BF16)


def _mixin(x_prompt, x_latent, mods, norm_g, wts, groups):
    n_prompt = groups.prompt_tiles
    w_in, w_uq = wts["w_in_rot"], wts["w_uq_rot"]
    in_specs = [groups.spec(D_MODEL, 0), groups.spec(D_MODEL, 1), _mods_spec(0, groups.cond), _normg_spec(0)]
    in_specs += _ffn_specs(wts["ffn"][0, 0], 0, 0) + [
        _const_spec(w_in.shape, (0, 0)),
        _const_spec((1, Q_LORA), (0, 0)),
        _const_spec((1, KV_LORA), (0, 0)),
        _const_spec(w_uq.shape, (0, 0)),
        _const_spec((KV_LORA, HEADS * HEAD_PAD + HEADS * V_DIM), (0, 0)),
        _const_spec((1, HEAD_PAD), (0, 0)),
        _const_spec((1, HEAD_PAD), (0, 0)),
    ] + [pl.BlockSpec((TM, LANES), lambda i: (jnp.maximum(i - n_prompt, 0) % groups.latent_tiles_per_seq, 0))] * 4
    args = [x_prompt, x_latent, mods, norm_g, *wts["ffn"][0, 0], w_in, wts["g_q_lat"],
            wts["g_kv_lat"], w_uq, wts["w_ukv"], wts["g_qn"], wts["g_kn"]] + list(wts["rope"])
    out_shape, out_specs = [], []
    for w, dt in ((D_MODEL, F32), (HEADS * HEAD_PAD, BF16), (HEADS * HEAD_PAD, BF16), (HEADS * V_DIM, BF16),
                  (S5_WIDTH, F32)):
        out_shape.append(jax.ShapeDtypeStruct((groups.rows, w), dt))
        out_specs.append(_row_spec(w))
    for w in (KV_LORA, ROPE):
        out_shape.append(jax.ShapeDtypeStruct((groups.group_rows[0], w), F32))
        out_specs.append(groups.spec(w, 0))
    return pl.pallas_call(
        functools.partial(_mixin_body, n_prompt),
        out_shape=out_shape, grid=(groups.tiles,), in_specs=in_specs, out_specs=out_specs,
        compiler_params=_params(1), name="ffn_mixin",
    )(*args)


def _ctx_body(ckv_ref, krp_ref, wukv_ref, gk_ref, k_ref, v_ref):
    kv = _dot(ckv_ref[...].astype(BF16), wukv_ref[...])
    v_ref[...] = kv[:, HEADS * HEAD_PAD:].astype(BF16)
    gk = gk_ref[...]
    krp = krp_ref[...]
    for h in range(HEADS):
        sl = slice(h * HEAD_PAD, (h + 1) * HEAD_PAD)
        k_ref[:, sl] = _head_norm(kv[:, sl] + krp, gk).astype(BF16)


def _ctx_kv(ckv, kr_padded, wts):
    rows = ckv.shape[0]
    return pl.pallas_call(
        _ctx_body,
        out_shape=[jax.ShapeDtypeStruct((rows, HEADS * HEAD_PAD), BF16),
                   jax.ShapeDtypeStruct((rows, HEADS * V_DIM), BF16)],
        grid=(rows // TM,),
        in_specs=[_row_spec(KV_LORA), _row_spec(LANES),
                  _const_spec((KV_LORA, HEADS * HEAD_PAD + HEADS * V_DIM), (0, 0)),
                  _const_spec((1, HEAD_PAD), (0, 0))],
        out_specs=[_row_spec(HEADS * HEAD_PAD), _row_spec(HEADS * V_DIM)],
        compiler_params=_params(1), name="ctx_kv",
    )(ckv, kr_padded, wts["w_ukv"], wts["g_kn"])


def _attn_body(n_kv, n_seq, n_aliased, *refs):
    q_ref = refs[0]
    k_refs = refs[1:1 + n_kv]
    v_refs = refs[1 + n_kv:1 + 2 * n_kv]
    o_ref, s_ref, p_ref, r_ref = refs[1 + 2 * n_kv + n_aliased:]
    tq = q_ref.shape[0] // n_seq
    c = (QK_DIM ** -0.5) * math.log2(math.e)
    lane = lax.broadcasted_iota(jnp.int32, (1, LANES), 1)
    n_keys = [k_ref.shape[0] // n_seq for k_ref in k_refs]
    spans = [slice(sum(n_keys[:i]), sum(n_keys[:i + 1])) for i in range(n_kv)]
    for sq, pair in [(sq, pair) for sq in range(n_seq) for pair in range(HEADS // 2)]:
        qrows = slice(sq * tq, (sq + 1) * tq)
        krows = [slice(sq * n, (sq + 1) * n) for n in n_keys]
        wide = slice(pair // 2 * 2 * LANES, (pair // 2 + 1) * 2 * LANES)
        mine = slice(pair % 2 * LANES, (pair % 2 + 1) * LANES)
        vsl = slice(pair * LANES, (pair + 1) * LANES)
        outs = []
        for h in (2 * pair, 2 * pair + 1):
            slot = h % 2
            sl = slice(h * HEAD_PAD, (h + 1) * HEAD_PAD)
            q = q_ref[qrows, sl]
            for k_ref, kr, span in zip(k_refs, krows, spans):
                s_ref[slot, :, span] = lax.dot_general(q, k_ref[kr, sl], (((1,), (1,)), ((), ())),
                                                       preferred_element_type=F32)
            for r0 in range(0, tq, ATT_ROWS):
                rows = slice(r0, r0 + ATT_ROWS)
                s = s_ref[slot, rows, :]
                e = jnp.exp2((s - jnp.max(s, axis=-1, keepdims=True)) * c)
                r_ref[slot, rows, :] = jnp.broadcast_to(1.0 / jnp.sum(e, axis=-1, keepdims=True), (ATT_ROWS, LANES))
                p_ref[slot, rows, :] = e.astype(BF16)
            o = functools.reduce(jnp.add, [_dot(p_ref[slot, :, span], v_ref[kr, wide])
                                           for v_ref, kr, span in zip(v_refs, krows, spans)])
            outs.append(o[:, mine] * r_ref[slot])
        o_ref[qrows, vsl] = jnp.where(lane < V_DIM, outs[0], outs[1]).astype(BF16)


def _attention(q, k, v, row0, rows, seq_len, tq, ctx=None, n_seq=1, out_prev=None):
    q_tiles = seq_len // tq
    n_seq = n_seq if q_tiles == 1 and ctx is None else 1
    q0, kv0 = row0 // (n_seq * tq), row0 // (n_seq * seq_len)
    kv_spec = lambda w: pl.BlockSpec((n_seq * seq_len, w), lambda b, t: (kv0 + b, 0))
    in_specs = [pl.BlockSpec((n_seq * tq, HEADS * HEAD_PAD), lambda b, t: (q0 + b * q_tiles + t, 0))]
    ks, vs = [kv_spec(HEADS * HEAD_PAD)], [kv_spec(HEADS * V_DIM)]
    kargs, vargs = [k], [v]
    n_keys = seq_len
    if ctx is not None:
        k_c, v_c = ctx
        past = k_c.shape[0] // (rows // seq_len)
        ks.insert(0, pl.BlockSpec((past, HEADS * HEAD_PAD), lambda b, t: (b, 0)))
        vs.insert(0, pl.BlockSpec((past, HEADS * V_DIM), lambda b, t: (b, 0)))
        kargs.insert(0, k_c)
        vargs.insert(0, v_c)
        n_keys += past
    args = [q, *kargs, *vargs]
    in_specs = in_specs + ks + vs
    aliases = {}
    if out_prev is not None:
        aliases[len(args)] = 0
        in_specs.append(pl.BlockSpec(memory_space=pl.ANY))
        args.append(out_prev)
    return pl.pallas_call(
        functools.partial(_attn_body, len(kargs), n_seq, len(aliases)),
        out_shape=jax.ShapeDtypeStruct((q.shape[0], HEADS * V_DIM), BF16),
        grid=(rows // (n_seq * seq_len), q_tiles),
        in_specs=in_specs,
        out_specs=pl.BlockSpec((n_seq * tq, HEADS * V_DIM), lambda b, t: (q0 + b * q_tiles + t, 0)),
        scratch_shapes=[pltpu.VMEM((2, tq, n_keys), F32), pltpu.VMEM((2, tq, n_keys), BF16),
                        pltpu.VMEM((2, tq, LANES), F32)],
        input_output_aliases=aliases,
        compiler_params=_params(2), name="attention",
    )(*args)


def _s5_project_in(u_ref, rows, bm_ref, d, q, bu_ref):
    qsl = slice(q * LANES, (q + 1) * LANES)
    lhs = jnp.concatenate([u_ref[0, rows, qsl], u_ref[1, rows, qsl]], axis=0).astype(BF16)
    bu = _dot(lhs, bm_ref[d, q])
    for b in range(2):
        r = b * S5_QUARTERS + q
        for j in range(2 * S5_SLABS):
            bu_ref[d, j, pl.ds(r * S5_PITCH, S5_TC), :] = bu[b * S5_TC:(b + 1) * S5_TC, j * LANES:(j + 1) * LANES]


def _s5_scan(bu_ref, st_ref, a_ref, carry_ref, steps):
    nrow = 2 * S5_QUARTERS
    order = [(d, j) for d in range(2) for j in range(S5_SLABS)]
    lanes = lambda j: slice(j * LANES, (j + 1) * LANES)
    carry = {(d, j): (carry_ref[d, 0, :, lanes(j)], carry_ref[d, 1, :, lanes(j)]) for d, j in order}
    for i in steps:
        for d, j in order:
            t = i if d == 0 else S5_TC - 1 - i
            rows = pl.ds(t, nrow, stride=S5_PITCH)
            sr, si = carry[d, j]
            ar, ai = a_ref[d, 0, :, lanes(j)], a_ref[d, 1, :, lanes(j)]
            nr = ar * sr - ai * si + bu_ref[d, j, rows, :]
            ni = ar * si + ai * sr + bu_ref[d, S5_SLABS + j, rows, :]
            st_ref[d, j, rows, :] = nr
            st_ref[d, S5_SLABS + j, rows, :] = ni
            carry[d, j] = (nr, ni)
    for d, j in order:
        carry_ref[d, 0, :, lanes(j)], carry_ref[d, 1, :, lanes(j)] = carry[d, j]


def _s5_project_out(st_ref, d, q, cm_ref, y_ref, rows):
    qsl = slice(q * LANES, (q + 1) * LANES)
    lhs = jnp.concatenate(
        [jnp.concatenate([st_ref[d, j, pl.ds((b * S5_QUARTERS + q) * S5_PITCH, S5_TC), :]
                          for j in range(2 * S5_SLABS)], axis=-1) for b in range(2)], axis=0).astype(BF16)
    y = _dot(lhs, cm_ref[d, q])
    y_ref[0, rows, qsl] = y[0:S5_TC]
    y_ref[1, rows, qsl] = y[S5_TC:2 * S5_TC]


def _s5_body(n_super, has_h0, want_fin, n_aliased, *refs):
    refs = list(refs)
    uf_ref = refs.pop(0)
    ub_ref = refs.pop(0) if n_super > 1 else uf_ref
    bm_ref, cm_ref, a_ref = refs[:3]
    refs = refs[3:]
    h0_ref = refs.pop(0) if has_h0 else None
    refs = refs[n_aliased:]
    yf_ref, yb_ref = refs[:2]
    refs = refs[2:]
    fin_ref = refs.pop(0) if want_fin else None
    bu0_ref, bu1_ref, st0_ref, st1_ref, carry_ref = refs
    s = pl.program_id(0)
    lo, hi = slice(0, S5_TC), slice(S5_TC, 2 * S5_TC)
    n_seg = S5_SEGMENTS
    seg_steps = S5_TC // n_seg
    seg_pairs = 2 * S5_QUARTERS // n_seg

    @pl.when(s == 0)
    def _():
        bu1_ref[...] = jnp.zeros(bu1_ref.shape, F32)
        st0_ref[...] = jnp.zeros(st0_ref.shape, F32)
        carry_ref[...] = jnp.zeros(carry_ref.shape, F32)

    def half(bu_w, bu_r, st_w, st_r, rows_f, rows_b, tag):
        for k in range(n_seg):
            @pl.when(s > -(1 + tag * n_seg + k))
            def _():
                io = ((uf_ref, rows_f, yf_ref), (ub_ref, rows_b, yb_ref))
                sub = seg_steps // seg_pairs
                for m in range(k * seg_pairs, (k + 1) * seg_pairs):
                    d, q = divmod(m, S5_QUARTERS)
                    _s5_project_in(io[d][0], io[d][1], bm_ref, d, q, bu_w)
                    _s5_scan(bu_r, st_w, a_ref, carry_ref, range(m * sub, (m + 1) * sub))
                    _s5_project_out(st_r, d, q, cm_ref, io[d][2], io[d][1])

    half(bu0_ref, bu1_ref, st1_ref, st0_ref, lo, hi, 0)

    @pl.when(s > -(1 + 2 * n_seg))
    def _():
        if want_fin:
            fin_ref[...] = carry_ref[...]
        start = h0_ref[...] if has_h0 else jnp.zeros(carry_ref.shape, F32)
        if n_super > 1:
            start = jnp.where(s % n_super == 0, start, carry_ref[...])
        carry_ref[...] = start

    half(bu1_ref, bu0_ref, st0_ref, st1_ref, hi, lo, 1)


def _s5(u, wts, row0, rows, seq_len, h0=None, want_fin=False, y_prev=None):
    batch = rows // seq_len
    n_super = seq_len // (2 * S5_TC)
    n_steps = (batch // 2) * n_super
    assert not (want_fin and n_super > 1) and row0 % (2 * seq_len) == 0
    pair0 = row0 // (2 * seq_len)
    u3 = u.reshape(u.shape[0] // seq_len, seq_len, S5_WIDTH)
    nrow = 2 * S5_QUARTERS
    blk = (2, 2 * S5_TC, S5_WIDTH)

    def chunk_spec(delay, reverse):
        def index(s):
            s = jnp.clip(s - delay, 0, n_steps - 1)
            c = s % n_super
            return (pair0 + s // n_super, n_super - 1 - c if reverse else c, 0)
        return pl.BlockSpec(blk, index)

    def state_spec(delay):
        return pl.BlockSpec((None, 2, 2, nrow, S5_QLANES),
                            lambda s: (jnp.clip(s - delay, 0, n_steps - 1) // n_super, 0, 0, 0, 0))

    in_specs, args = [chunk_spec(0, False)], [u3]
    if n_super > 1:
        in_specs.append(chunk_spec(0, True))
        args.append(u3)
    in_specs += [_const_spec((2, S5_QUARTERS, LANES, 2 * S5_QLANES), (0, 0, 0, 0)),
                 _const_spec((2, S5_QUARTERS, 2 * S5_QLANES, LANES), (0, 0, 0, 0)),
                 _const_spec((2, 2, nrow, S5_QLANES), (0, 0, 0, 0))]
    args += [wts["s5_b"], wts["s5_c"], wts["s5_a"]]
    if h0 is not None:
        in_specs.append(state_spec(0))
        args.append(h0)
    aliases = {}
    if y_prev is not None:
        for n, y in enumerate(y_prev):
            aliases[len(args)] = n
            in_specs.append(pl.BlockSpec(memory_space=pl.ANY))
            args.append(y.reshape(u3.shape))
    y_shape = jax.ShapeDtypeStruct(u3.shape, F32)
    out_shape, out_specs = [y_shape, y_shape], [chunk_spec(1, False), chunk_spec(1, True)]
    if want_fin:
        out_shape.append(jax.ShapeDtypeStruct((batch // 2, 2, 2, nrow, S5_QLANES), F32))
        out_specs.append(state_spec(1))
    work = pltpu.VMEM((2, 2 * S5_SLABS, nrow * S5_PITCH, LANES), F32)
    return pl.pallas_call(
        functools.partial(_s5_body, n_super, h0 is not None, want_fin, len(aliases)),
        out_shape=out_shape, grid=(n_steps + 1,), in_specs=in_specs, out_specs=out_specs,
        scratch_shapes=[work, work, work, work, pltpu.VMEM((2, 2, nrow, S5_QLANES), F32)],
        input_output_aliases=aliases,
        compiler_params=_params(1), name="s5",
    )(*args)


def _mixout_body(x_ref, attn_ref, u_ref, yf_ref, yb_ref, mods0_ref, mods1_ref, g0_ref, g1_ref, dsk_ref,
                 wglu_ref, bglu_ref, wout_ref, wg0_ref, wu0_ref, wd0_ref, wg1_ref, wu1_ref, wd1_ref, o_ref):
    mods = mods0_ref[...]
    g3 = g0_ref[...]
    y = dsk_ref[...] * u_ref[...] + yf_ref[...] + yb_ref[...]
    z = _gelu_tanh(y)
    s5o = z * _sigmoid(_dot(z.astype(BF16), wglu_ref[...]) + bglu_ref[...])
    half = HEADS * V_DIM
    mix = _dot(attn_ref[...], wout_ref[0:half, :]) + _dot(s5o.astype(BF16), wout_ref[half:, :])
    x2 = x_ref[...] + mods[5:6] * mix
    x3 = _ffn(x2, mods[6:7], mods[7:8], mods[8:9], g3[2:3], wg0_ref, wu0_ref, wd0_ref)
    mods = mods1_ref[...]
    o_ref[...] = _ffn(x3, mods[0:1], mods[1:2], mods[2:3], g1_ref[...][0:1], wg1_ref, wu1_ref, wd1_ref)


def _mixout(x, attn, u, yf, yb, mods, norm_g, wts, groups):
    widths = (HEADS * V_DIM, S5_WIDTH, S5_WIDTH, S5_WIDTH)
    in_specs = [_row_spec(D_MODEL)] + [_row_spec(w) for w in widths]
    args = [x] + [a.reshape(groups.rows, w) for a, w in zip((attn, u, yf, yb), widths)]
    in_specs += [_mods_spec(0, groups.cond), _mods_spec(1, groups.cond), _normg_spec(0), _normg_spec(1),
                 _const_spec((1, S5_WIDTH), (0, 0)), _const_spec((S5_WIDTH, S5_WIDTH), (0, 0)),
                 _const_spec((1, S5_WIDTH), (0, 0)), _const_spec((D_MODEL, D_MODEL), (0, 0))]
    in_specs += _ffn_specs(wts["ffn"][0, 1], 0, 1) + _ffn_specs(wts["ffn"][1, 0], 1, 0)
    args += [mods, mods, norm_g, norm_g, wts["s5_d"], wts["w_glu"], wts["b_glu"], wts["w_out"],
             *wts["ffn"][0, 1], *wts["ffn"][1, 0]]
    return pl.pallas_call(
        _mixout_body,
        out_shape=jax.ShapeDtypeStruct((groups.rows, D_MODEL), F32),
        grid=(groups.tiles,), in_specs=in_specs, out_specs=_row_spec(D_MODEL),
        compiler_params=_params(1), name="mixout_ffn",
    )(*args)


def _conv_body(n_prompt, seqs, x_ref, xp_ref, xn_ref, mods_ref, g_ref, cwin_ref, cw_ref, cwout_ref,
               wg_ref, wu_ref, wd_ref, op_ref, ol_ref):
    i = pl.program_id(0)
    is_prompt = i < n_prompt
    mods = mods_ref[...]
    g3 = g_ref[...]
    x = x_ref[...]
    n_ext = TM + 2 * SUBLANES
    xe = jnp.concatenate([xp_ref[...], x, xn_ref[...]], axis=0)
    hne = _modulate(xe, g3[1:2], mods[3:4], mods[4:5]).astype(BF16)
    pz = _dot(hne, cwin_ref[:, D_MODEL:3 * D_MODEL])
    z = pz[:, 0:D_MODEL] * pz[:, D_MODEL:2 * D_MODEL]
    gate_b = _dot(hne[SUBLANES:SUBLANES + TM], cwin_ref[:, 0:D_MODEL])
    main = slice(SUBLANES, SUBLANES + TM)
    row = i * TM + lax.broadcasted_iota(jnp.int32, (TM, 1), 0)
    pos = jnp.where(is_prompt, row % seqs[0], row % seqs[1])
    end = jnp.where(is_prompt, seqs[0] - 1, seqs[1] - 1)
    z_prev = jnp.where(pos == 0, 0.0, pltpu.roll(z, 1, 0)[main])
    z_next = jnp.where(pos == end, 0.0, pltpu.roll(z, n_ext - 1, 0)[main])
    cw = cw_ref[...]
    zc = z_prev * cw[0:1] + z[main] * cw[1:2] + z_next * cw[2:3]
    mix = _dot((gate_b * zc).astype(BF16), cwout_ref[...])
    x2 = x + mods[5:6] * mix
    out = _ffn(x2, mods[6:7], mods[7:8], mods[8:9], g3[2:3], wg_ref, wu_ref, wd_ref)

    @pl.when(is_prompt)
    def _():
        op_ref[...] = out

    @pl.when(jnp.logical_not(is_prompt))
    def _():
        ol_ref[...] = out


def _conv_mixer(x, mods, norm_g, wts, groups):
    per = TM // SUBLANES
    last = groups.rows // SUBLANES - 1
    halo = (SUBLANES, D_MODEL)
    in_specs = [_row_spec(D_MODEL),
                pl.BlockSpec(halo, lambda i: (jnp.maximum(i * per - 1, 0), 0)),
                pl.BlockSpec(halo, lambda i: (jnp.minimum((i + 1) * per, last), 0)),
                _mods_spec(1, groups.cond), _normg_spec(1),
                _const_spec((D_MODEL, 3 * D_MODEL), (0, 0)), _const_spec((CONV_K, D_MODEL), (0, 0)),
                _const_spec((D_MODEL, D_MODEL), (0, 0))] + _ffn_specs(wts["ffn"][1, 1], 1, 1)
    return pl.pallas_call(
        functools.partial(_conv_body, groups.prompt_tiles, groups.seq),
        out_shape=[jax.ShapeDtypeStruct((r, D_MODEL), F32) for r in groups.group_rows],
        grid=(groups.tiles,), in_specs=in_specs, out_specs=[groups.spec(D_MODEL, 0), groups.spec(D_MODEL, 1)],
        compiler_params=_params(1), name="conv_ffn",
    )(x, x, x, mods, norm_g, wts["conv_w_in"], wts["conv_w"], wts["conv_w_out"],
      *wts["ffn"][1, 1])


ROPE_PARTNER = np.concatenate([np.arange(8, 16), np.arange(0, 8), np.arange(24, 32), np.arange(16, 24)])


def _rope_tables(seq_len, g_q, g_k):
    t = np.arange(seq_len)
    quarter = ROPE // 4
    inv_freq = ROPE_BASE ** (-np.arange(quarter, dtype=np.float64) / quarter)
    cos = np.ones((seq_len, LANES))
    sin = np.zeros((seq_len, LANES))
    for base, pos in ((NOPE, t // GRID_W), (NOPE + ROPE // 2, t % GRID_W)):
        ang = pos[:, None].astype(np.float64) * inv_freq[None, :]
        cos[:, base:base + quarter] = np.cos(ang)
        cos[:, base + quarter:base + 2 * quarter] = np.cos(ang)
        sin[:, base:base + quarter] = -np.sin(ang)
        sin[:, base + quarter:base + 2 * quarter] = np.sin(ang)
    cos, sin = jnp.asarray(cos, F32), jnp.asarray(sin, F32)
    out = []
    for g in (g_q, g_k):
        g = g.reshape(QK_DIM)
        g_pad = jnp.pad(g, (0, HEAD_PAD - QK_DIM))
        g_partner = jnp.pad(g[NOPE:][ROPE_PARTNER], (NOPE, HEAD_PAD - QK_DIM))
        out += [g_pad[None, :] * cos, g_partner[None, :] * sin]
    return tuple(out)


def _s5_params(lam_re, lam_im, log_dt, b_re, b_im, c_re, c_im):
    dt = jnp.exp(log_dt)[..., None]
    lr = jnp.minimum(lam_re, LAMBDA_RE_MAX)
    li = lam_im
    mag = jnp.exp(lr * dt)
    ang = li * dt
    ab_re = mag * jnp.cos(ang)
    ab_im = mag * jnp.sin(ang)
    den = lr * lr + li * li
    nr = ab_re - 1.0
    ni = ab_im
    co_re = (nr * lr + ni * li) / den
    co_im = (ni * lr - nr * li) / den
    bb_re = co_re[..., None] * b_re - co_im[..., None] * b_im
    bb_im = co_re[..., None] * b_im + co_im[..., None] * b_re
    per_q = S5_GROUPS // S5_QUARTERS
    eye = jnp.eye(per_q, dtype=F32)

    def in_mat(bb):
        bb = bb.reshape(2, S5_QUARTERS, per_q, S5_STATE, S5_GROUP)
        return jnp.einsum("dqgpi,gh->dqgihp", bb, eye).reshape(2, S5_QUARTERS, LANES, S5_QLANES)

    def out_mat(cc):
        cc = cc.reshape(2, S5_QUARTERS, per_q, S5_GROUP, S5_STATE)
        return jnp.einsum("dqgip,gh->dqgphi", cc, eye).reshape(2, S5_QUARTERS, S5_QLANES, LANES)

    bm = jnp.concatenate([in_mat(bb_re), in_mat(bb_im)], axis=-1).astype(BF16)
    cm = jnp.concatenate([out_mat(c_re), -out_mat(c_im)], axis=-2).astype(BF16)

    def rows(a):
        a = a.reshape(2, 1, S5_QUARTERS, S5_QLANES)
        return jnp.broadcast_to(a, (2, 2, S5_QUARTERS, S5_QLANES)).reshape(2, 2 * S5_QUARTERS, S5_QLANES)

    return bm, cm, jnp.stack([rows(ab_re), rows(ab_im)], axis=1)


def _state_rows(s):
    b = s.shape[0]
    s = s.reshape(b // 2, 2, 2, S5_QUARTERS, S5_QLANES)
    return s.transpose(0, 2, 1, 3, 4).reshape(b // 2, 2, 2 * S5_QUARTERS, S5_QLANES)


def _state_unrows(s, batch):
    s = s.reshape(batch // 2, 2, 2, S5_QUARTERS, S5_QLANES).transpose(0, 2, 1, 3, 4)
    return s.reshape(batch, 1, 2, S5_GROUPS, S5_STATE)


def kernel(x_prompt, x_sample, cache_ckv, cache_krope, state_ssm_re, state_ssm_im, c, c_ctx, w_ada, b_ada, norm_g, ffn_w_gate, ffn_w_up, ffn_w_down, ab_w_in, mla_g_q_lat, mla_g_kv_lat, mla_w_uq, mla_w_ukv, mla_g_qnorm, mla_g_knorm, s5_lam_re, s5_lam_im, s5_log_dt, s5_b_re, s5_b_im, s5_c_re, s5_c_im, s5_d, s5_w_glu, s5_b_glu, ab_w_out, conv_w_in, conv_w, conv_w_out):
    batch, seq, _ = x_prompt.shape
    dec_batch, dec_seq, _ = x_sample.shape
    past = cache_ckv.shape[2]

    w_in = ab_w_in[0]
    kr_cols = w_in[:, Q_LORA + KV_LORA:Q_LORA + KV_LORA + ROPE]
    zeros = jnp.zeros_like(kr_cols)
    w_uq = jnp.pad(mla_w_uq[0].reshape(Q_LORA, HEADS, QK_DIM), ((0, 0), (0, 0), (0, HEAD_PAD - QK_DIM)))
    w_ukv = mla_w_ukv[0].reshape(KV_LORA, HEADS, NOPE + V_DIM)
    w_uk = jnp.pad(w_ukv[:, :, :NOPE], ((0, 0), (0, 0), (0, HEAD_PAD - NOPE)))
    bm, cm, a_rows = _s5_params(s5_lam_re[0], s5_lam_im[0], s5_log_dt[0], s5_b_re[0], s5_b_im[0],
                                s5_c_re[0], s5_c_im[0])
    stacked = (ffn_w_gate.astype(BF16), ffn_w_up.astype(BF16), ffn_w_down.astype(BF16))
    ffn = {(l, s): stacked for l in range(2) for s in range(2)}
    w_in_cols = [w_in[:, :Q_LORA + KV_LORA], w_in[:, Q_LORA + KV_LORA + ROPE:], kr_cols, zeros, kr_cols, zeros]
    w_in_partner = [zeros, zeros, kr_cols[:, ROPE_PARTNER], zeros]
    w_uq3 = mla_w_uq[0].reshape(Q_LORA, HEADS, QK_DIM)
    w_uq_partner = jnp.pad(w_uq3[:, :, NOPE:][:, :, ROPE_PARTNER], ((0, 0), (0, 0), (NOPE, HEAD_PAD - QK_DIM)))
    wts = {
        "ffn": ffn,
        "w_in_rot": jnp.concatenate(w_in_cols + w_in_partner, axis=1).astype(BF16),
        "g_q_lat": mla_g_q_lat, "g_kv_lat": mla_g_kv_lat,
        "w_uq_rot": jnp.concatenate([w_uq, w_uq_partner], axis=1).reshape(Q_LORA, 2 * HEADS * HEAD_PAD).astype(BF16),
        "w_ukv": jnp.concatenate([w_uk.reshape(KV_LORA, HEADS * HEAD_PAD),
                                  w_ukv[:, :, NOPE:].reshape(KV_LORA, HEADS * V_DIM)], axis=1).astype(BF16),
        "g_qn": jnp.pad(mla_g_qnorm, ((0, 0), (0, HEAD_PAD - QK_DIM))),
        "g_kn": jnp.pad(mla_g_knorm, ((0, 0), (0, HEAD_PAD - QK_DIM))),
        "rope": _rope_tables(dec_seq, mla_g_qnorm, mla_g_knorm),
        "s5_b": bm, "s5_c": cm, "s5_a": a_rows,
        "s5_d": s5_d, "w_glu": s5_w_glu[0].astype(BF16), "b_glu": s5_b_glu, "w_out": ab_w_out[0].astype(BF16),
        "conv_w_in": conv_w_in[0].astype(BF16), "conv_w": conv_w[0].T, "conv_w_out": conv_w_out[0].astype(BF16),
    }

    cond = jnp.zeros((SUBLANES, D_MODEL), F32).at[0].set(c_ctx).at[1:1 + dec_batch].set(c)
    mods = _ada(cond, w_ada, b_ada).reshape(w_ada.shape[0], SUBLANES, N_MOD, D_MODEL)

    groups = _Groups(batch * seq, seq, dec_batch * dec_seq, dec_seq)
    rows_p, rows_l = groups.group_rows
    x1, q, k, v, u, ckv_p, kr_p = _mixin(
        x_prompt.reshape(rows_p, D_MODEL), x_sample.reshape(rows_l, D_MODEL), mods, norm_g, wts, groups)
    kr_pad = jnp.pad(cache_krope[:, 0].reshape(dec_batch * past, ROPE), ((0, 0), (NOPE, LANES - QK_DIM)))
    ctx = _ctx_kv(cache_ckv[:, 0].reshape(dec_batch * past, KV_LORA), kr_pad, wts)
    h0 = jnp.stack([_state_rows(state_ssm_re[:, 0]), _state_rows(state_ssm_im[:, 0])], axis=2)
    attn = _attention(q, k, v, 0, rows_p, seq, seq, n_seq=2)
    attn = _attention(q, k, v, rows_p, rows_l, dec_seq, 512, ctx, out_prev=attn)
    yf, yb, fin = _s5(u, wts, 0, rows_p, seq, want_fin=True)
    yf, yb = _s5(u, wts, rows_p, rows_l, dec_seq, h0=h0, y_prev=(yf, yb))
    x3 = _mixout(x1, attn, u, yf, yb, mods, norm_g, wts, groups)
    y_p, y_s = _conv_mixer(x3, mods, norm_g, wts, groups)
    return (y_p.reshape(batch, seq, D_MODEL), y_s.reshape(dec_batch, dec_seq, D_MODEL),
            ckv_p.reshape(batch, 1, seq, KV_LORA), kr_p.reshape(batch, 1, seq, ROPE),
            _state_unrows(fin[:, :, 0], batch), _state_unrows(fin[:, :, 1], batch))
```

```python
import functools
import math

import jax
import jax.numpy as jnp
import numpy as np
from jax import lax
from jax.experimental import pallas as pl
from jax.experimental.pallas import tpu as pltpu

F32 = jnp.float32
BF16 = jnp.bfloat16

LANES = 128
SUBLANES = 8
VMEM_LIMIT_BYTES = 60 * 1024 * 1024

D_MODEL = 1024
D_FF = 2816
N_MOD = 9
EPS = 1e-6
HEADS = 8
Q_LORA = 384
KV_LORA = 256
NOPE = 64
ROPE = 32
V_DIM = 64
QK_DIM = NOPE + ROPE
HEAD_PAD = LANES
ROPE_BASE = 10000.0
GRID_W = 64
S5_WIDTH = 512
S5_GROUP = 16
S5_GROUPS = 32
S5_STATE = 64
S5_LANES = S5_GROUPS * S5_STATE
S5_QUARTERS = 4
S5_QLANES = S5_LANES // S5_QUARTERS
S5_SLABS = S5_QLANES // LANES
S5_TC = 128
S5_PITCH = S5_TC + 4
S5_SEGMENTS = 1
LAMBDA_RE_MAX = -1e-4
CONV_K = 3

ATT_ROWS = 2 * SUBLANES
TM = 512
FF_CHUNKS = tuple((c, min(512, D_FF - c)) for c in range(0, D_FF, 512))

O_CKV = Q_LORA
O_U = Q_LORA + KV_LORA
O_KR = O_U + S5_WIDTH


def _dot(a, b):
    return jnp.dot(a, b, preferred_element_type=F32)


def _sigmoid(x):
    return 1.0 / (1.0 + jnp.exp(-x))


def _rms(x, g, n):
    ms = jnp.sum(x * x, axis=-1, keepdims=True) * (1.0 / n)
    return x * lax.rsqrt(ms + EPS) * g


def _modulate(x, g, shift, scale):
    return _rms(x, g, D_MODEL) * (1.0 + scale) + shift


def _ffn(x, shift, scale, gate, g, wg_ref, wu_ref, wd_ref):
    h = _modulate(x, g, shift, scale).astype(BF16)
    acc = jnp.zeros(x.shape, F32)
    for c0, cs in FF_CHUNKS:
        gt = _dot(h, wg_ref[:, c0:c0 + cs])
        up = _dot(h, wu_ref[:, c0:c0 + cs])
        act = ((gt * _sigmoid(gt)) * up).astype(BF16)
        acc = acc + _dot(act, wd_ref[c0:c0 + cs, :])
    return x + (0.5 * gate) * acc


def _head_inv_rms(xh):
    return lax.rsqrt(jnp.sum(xh * xh, axis=-1, keepdims=True) * (1.0 / QK_DIM) + EPS)


def _head_norm(xh, g):
    return xh * _head_inv_rms(xh) * g


def _gelu_tanh(x):
    return x * (0.5 * (1.0 + jnp.tanh(math.sqrt(2.0 / math.pi) * (x + 0.044715 * (x * x * x)))))


def _const_spec(shape, index):
    return pl.BlockSpec(shape, lambda *_: index, pipeline_mode=pl.Buffered(1))


def _params(n_axes):
    return pltpu.CompilerParams(dimension_semantics=("arbitrary",) * n_axes, vmem_limit_bytes=VMEM_LIMIT_BYTES)


def _ffn_specs(weights, layer, sub):
    return [_const_spec(w.shape, (0, 0)) if w.ndim == 2 else
            _const_spec((None, None) + w.shape[2:], (layer, sub, 0, 0)) for w in weights]


def _mods_spec(layer, cond_of_tile):
    return pl.BlockSpec((None, None, N_MOD, D_MODEL), lambda i: (layer, cond_of_tile(i), 0, 0))


def _normg_spec(layer):
    return _const_spec((None, 3, D_MODEL), (layer, 0, 0))


def _row_spec(width, rows=TM):
    return pl.BlockSpec((rows, width), lambda i: (i, 0))


class _Groups:
    def __init__(self, prompt_rows, prompt_seq, latent_rows, latent_seq):
        assert prompt_rows % TM == 0 and latent_seq % TM == 0 and TM % prompt_seq == 0
        assert prompt_rows % latent_seq == 0
        self.group_rows = (prompt_rows, latent_rows)
        self.seq = (prompt_seq, latent_seq)
        self.rows = prompt_rows + latent_rows
        self.prompt_tiles = prompt_rows // TM
        self.tiles = self.rows // TM
        self.latent_tiles_per_seq = latent_seq // TM

    def cond(self, i):
        return jnp.where(i < self.prompt_tiles, 0,
                         1 + jnp.maximum(i - self.prompt_tiles, 0) // self.latent_tiles_per_seq)

    def spec(self, width, group):
        n = self.prompt_tiles
        if group == 0:
            return pl.BlockSpec((TM, width), lambda i: (jnp.minimum(i, n - 1), 0))
        return pl.BlockSpec((TM, width), lambda i: (jnp.maximum(i - n, 0), 0))


def _ada_body(c_ref, w_ref, b_ref, o_ref):
    c = c_ref[...]
    s = (c * _sigmoid(c)).astype(BF16)
    o_ref[...] = _dot(s, w_ref[...].astype(BF16)) + b_ref[...]


def _ada(cond, w_ada, b_ada):
    depth, _, n = w_ada.shape
    tn = 2304
    return pl.pallas_call(
        _ada_body,
        out_shape=jax.ShapeDtypeStruct((depth, SUBLANES, n), F32),
        grid=(depth, n // tn),
        in_specs=[pl.BlockSpec((SUBLANES, D_MODEL), lambda l, j: (0, 0)),
                  pl.BlockSpec((None, D_MODEL, tn), lambda l, j: (l, 0, j)),
                  pl.BlockSpec((None, 1, tn), lambda l, j: (l, 0, j))],
        out_specs=pl.BlockSpec((None, SUBLANES, tn), lambda l, j: (l, 0, j)),
        compiler_params=_params(2),
        name="ada",
    )(cond, w_ada, b_ada.reshape(depth, 1, n))


def _mixin_body(n_prompt, *refs):
    (xp_ref, xs_ref, mods_ref, g_ref, wg_ref, wu_ref, wd_ref, win_ref, gql_ref, gkv_ref, wuq_ref, wukv_ref,
     gq_ref, gk_ref, rqc_ref, rqs_ref, rkc_ref, rks_ref,
     x1_ref, q_ref, k_ref, v_ref, u_ref, ckv_ref, kr_ref) = refs
    is_prompt = pl.program_id(0) < n_prompt
    mods = mods_ref[...]
    g3 = g_ref[...]
    x = jnp.where(is_prompt, xp_ref[...], xs_ref[...])
    x1 = _ffn(x, mods[0:1], mods[1:2], mods[2:3], g3[0:1], wg_ref, wu_ref, wd_ref)
    x1_ref[...] = x1
    hn = _modulate(x1, g3[1:2], mods[3:4], mods[4:5]).astype(BF16)
    proj = _dot(hn, win_ref[:, 0:O_KR + LANES])
    u_ref[...] = proj[:, O_U:O_KR]
    ckv = _rms(proj[:, O_CKV:O_U], gkv_ref[...], KV_LORA)
    krg = proj[:, O_KR:O_KR + LANES]
    qn = _rms(proj[:, 0:Q_LORA], gql_ref[...], Q_LORA).astype(BF16)
    kv = _dot(ckv.astype(BF16), wukv_ref[...])
    v_ref[...] = kv[:, HEADS * HEAD_PAD:].astype(BF16)
    lane = lax.broadcasted_iota(jnp.int32, (1, LANES), 1)
    kr_only = jnp.where((lane >= NOPE) & (lane < QK_DIM), krg, 0.0)
    gk = gk_ref[...]
    head = lambda h: slice(h * HEAD_PAD, (h + 1) * HEAD_PAD)

    @pl.when(is_prompt)
    def _():
        ckv_ref[...] = ckv
        kr_ref[...] = krg[:, 0:ROPE]
        qraw = _dot(qn, wuq_ref[:, 0:HEADS * HEAD_PAD])
        gq = gq_ref[...]
        for h in range(HEADS):
            q_ref[:, head(h)] = _head_norm(qraw[:, head(h)], gq).astype(BF16)
            k_ref[:, head(h)] = _head_norm(kv[:, head(h)] + kr_only, gk).astype(BF16)

    @pl.when(jnp.logical_not(is_prompt))
    def _():
        qraw = _dot(qn, wuq_ref[...])
        kr_partner = _dot(hn, win_ref[:, O_KR + LANES:O_KR + 2 * LANES])
        k_rot = kr_only * rkc_ref[...] + kr_partner * rks_ref[...]
        for h in range(HEADS):
            qa = qraw[:, head(h)]
            qh = (qa * rqc_ref[...] + qraw[:, head(HEADS + h)] * rqs_ref[...]) * _head_inv_rms(qa)
            kh = (kv[:, head(h)] * gk + k_rot) * _head_inv_rms(kv[:, head(h)] + kr_only)
            q_ref[:, head(h)] = qh.astype(BF16)
            k_ref[:, head(h)] = kh.astype(BF16)


def _mixin(x_prompt, x_latent, mods, norm_g, wts, groups):
    n_prompt = groups.prompt_tiles
    w_in, w_uq = wts["w_in_rot"], wts["w_uq_rot"]
    in_specs = [groups.spec(D_MODEL, 0), groups.spec(D_MODEL, 1), _mods_spec(0, groups.cond), _normg_spec(0)]
    in_specs += _ffn_specs(wts["ffn"][0, 0], 0, 0) + [
        _const_spec(w_in.shape, (0, 0)),
        _const_spec((1, Q_LORA), (0, 0)),
        _const_spec((1, KV_LORA), (0, 0)),
        _const_spec(w_uq.shape, (0, 0)),
        _const_spec((KV_LORA, HEADS * HEAD_PAD + HEADS * V_DIM), (0, 0)),
        _const_spec((1, HEAD_PAD), (0, 0)),
        _const_spec((1, HEAD_PAD), (0, 0)),
    ] + [pl.BlockSpec((TM, LANES), lambda i: (jnp.maximum(i - n_prompt, 0) % groups.latent_tiles_per_seq, 0))] * 4
    args = [x_prompt, x_latent, mods, norm_g, *wts["ffn"][0, 0], w_in, wts["g_q_lat"],
            wts["g_kv_lat"], w_uq, wts["w_ukv"], wts["g_qn"], wts["g_kn"]] + list(wts["rope"])
    out_shape, out_specs = [], []
    for w, dt in ((D_MODEL, F32), (HEADS * HEAD_PAD, BF16), (HEADS * HEAD_PAD, BF16), (HEADS * V_DIM, BF16),
                  (S5_WIDTH, F32)):
        out_shape.append(jax.ShapeDtypeStruct((groups.rows, w), dt))
        out_specs.append(_row_spec(w))
    for w in (KV_LORA, ROPE):
        out_shape.append(jax.ShapeDtypeStruct((groups.group_rows[0], w), F32))
        out_specs.append(groups.spec(w, 0))
    return pl.pallas_call(
        functools.partial(_mixin_body, n_prompt),
        out_shape=out_shape, grid=(groups.tiles,), in_specs=in_specs, out_specs=out_specs,
        compiler_params=_params(1), name="ffn_mixin",
    )(*args)


def _ctx_body(ckv_ref, krp_ref, wukv_ref, gk_ref, k_ref, v_ref):
    kv = _dot(ckv_ref[...].astype(BF16), wukv_ref[...])
    v_ref[...] = kv[:, HEADS * HEAD_PAD:].astype(BF16)
    gk = gk_ref[...]
    krp = krp_ref[...]
    for h in range(HEADS):
        sl = slice(h * HEAD_PAD, (h + 1) * HEAD_PAD)
        k_ref[:, sl] = _head_norm(kv[:, sl] + krp, gk).astype(BF16)


def _ctx_kv(ckv, kr_padded, wts):
    rows = ckv.shape[0]
    return pl.pallas_call(
        _ctx_body,
        out_shape=[jax.ShapeDtypeStruct((rows, HEADS * HEAD_PAD), BF16),
                   jax.ShapeDtypeStruct((rows, HEADS * V_DIM), BF16)],
        grid=(rows // TM,),
        in_specs=[_row_spec(KV_LORA), _row_spec(LANES),
                  _const_spec((KV_LORA, HEADS * HEAD_PAD + HEADS * V_DIM), (0, 0)),
                  _const_spec((1, HEAD_PAD), (0, 0))],
        out_specs=[_row_spec(HEADS * HEAD_PAD), _row_spec(HEADS * V_DIM)],
        compiler_params=_params(1), name="ctx_kv",
    )(ckv, kr_padded, wts["w_ukv"], wts["g_kn"])


def _attn_body(n_kv, n_seq, n_zero, *refs):
    q_ref = refs[0]
    k_refs = refs[1:1 + n_kv]
    v_refs = refs[1 + n_kv:1 + 2 * n_kv]
    o_ref = refs[1 + 2 * n_kv]
    zero_refs = refs[2 + 2 * n_kv:2 + 2 * n_kv + n_zero]
    s_ref, p_ref, r_ref = refs[2 + 2 * n_kv + n_zero:]
    for z_ref in zero_refs:
        z_ref[...] = jnp.zeros(z_ref.shape, z_ref.dtype)
    tq = q_ref.shape[0] // n_seq
    c = (QK_DIM ** -0.5) * math.log2(math.e)
    lane = lax.broadcasted_iota(jnp.int32, (1, LANES), 1)
    n_keys = [k_ref.shape[0] // n_seq for k_ref in k_refs]
    spans = [slice(sum(n_keys[:i]), sum(n_keys[:i + 1])) for i in range(n_kv)]
    for sq, pair in [(sq, pair) for sq in range(n_seq) for pair in range(HEADS // 2)]:
        qrows = slice(sq * tq, (sq + 1) * tq)
        krows = [slice(sq * n, (sq + 1) * n) for n in n_keys]
        wide = slice(pair // 2 * 2 * LANES, (pair // 2 + 1) * 2 * LANES)
        mine = slice(pair % 2 * LANES, (pair % 2 + 1) * LANES)
        vsl = slice(pair * LANES, (pair + 1) * LANES)
        outs = []
        for h in (2 * pair, 2 * pair + 1):
            slot = h % 2
            sl = slice(h * HEAD_PAD, (h + 1) * HEAD_PAD)
            q = q_ref[qrows, sl]
            for k_ref, kr, span in zip(k_refs, krows, spans):
                s_ref[slot, :, span] = lax.dot_general(q, k_ref[kr, sl], (((1,), (1,)), ((), ())),
                                                       preferred_element_type=F32)
            for r0 in range(0, tq, ATT_ROWS):
                rows = slice(r0, r0 + ATT_ROWS)
                s = s_ref[slot, rows, :]
                e = jnp.exp2((s - jnp.max(s, axis=-1, keepdims=True)) * c)
                r_ref[slot, rows, :] = jnp.broadcast_to(1.0 / jnp.sum(e, axis=-1, keepdims=True), (ATT_ROWS, LANES))
                p_ref[slot, rows, :] = e.astype(BF16)
            o = functools.reduce(jnp.add, [_dot(p_ref[slot, :, span], v_ref[kr, wide])
                                           for v_ref, kr, span in zip(v_refs, krows, spans)])
            outs.append(o[:, mine] * r_ref[slot])
        o_ref[qrows, vsl] = jnp.where(lane < V_DIM, outs[0], outs[1]).astype(BF16)


def _attention(q, k, v, row0, rows, seq_len, tq, ctx=None, n_seq=1, zeros=()):
    q_tiles = seq_len // tq
    n_seq = n_seq if q_tiles == 1 and ctx is None else 1
    q0, kv0 = row0 // (n_seq * tq), row0 // (n_seq * seq_len)
    kv_spec = lambda w: pl.BlockSpec((n_seq * seq_len, w), lambda b, t: (kv0 + b, 0))
    in_specs = [pl.BlockSpec((n_seq * tq, HEADS * HEAD_PAD), lambda b, t: (q0 + b * q_tiles + t, 0))]
    ks, vs = [kv_spec(HEADS * HEAD_PAD)], [kv_spec(HEADS * V_DIM)]
    kargs, vargs = [k], [v]
    n_keys = seq_len
    if ctx is not None:
        k_c, v_c = ctx
        past = k_c.shape[0] // (rows // seq_len)
        ks.insert(0, pl.BlockSpec((past, HEADS * HEAD_PAD), lambda b, t: (b, 0)))
        vs.insert(0, pl.BlockSpec((past, HEADS * V_DIM), lambda b, t: (b, 0)))
        kargs.insert(0, k_c)
        vargs.insert(0, v_c)
        n_keys += past
    grid = (rows // (n_seq * seq_len), q_tiles)
    out_shape = [jax.ShapeDtypeStruct((rows, HEADS * V_DIM), BF16)]
    out_specs = [pl.BlockSpec((n_seq * tq, HEADS * V_DIM), lambda b, t: (b * q_tiles + t, 0))]
    for shape in zeros:
        assert q_tiles == 1 and shape[0] % (SUBLANES * grid[0]) == 0
        out_shape.append(jax.ShapeDtypeStruct(shape, F32))
        out_specs.append(pl.BlockSpec((shape[0] // grid[0], shape[1]), lambda b, t: (b, 0)))
    return pl.pallas_call(
        functools.partial(_attn_body, len(kargs), n_seq, len(zeros)),
        out_shape=out_shape, grid=grid, in_specs=in_specs + ks + vs, out_specs=out_specs,
        scratch_shapes=[pltpu.VMEM((2, tq, n_keys), F32), pltpu.VMEM((2, tq, n_keys), BF16),
                        pltpu.VMEM((2, tq, LANES), F32)],
        compiler_params=_params(2), name="attention",
    )(q, *kargs, *vargs)


def _s5_project_in(u_ref, rows, bm_ref, d, q, bu_ref):
    qsl = slice(q * LANES, (q + 1) * LANES)
    lhs = jnp.concatenate([u_ref[0, rows, qsl], u_ref[1, rows, qsl]], axis=0).astype(BF16)
    bu = _dot(lhs, bm_ref[d, q])
    for b in range(2):
        r = b * S5_QUARTERS + q
        for j in range(2 * S5_SLABS):
            bu_ref[d, j, pl.ds(r * S5_PITCH, S5_TC), :] = bu[b * S5_TC:(b + 1) * S5_TC, j * LANES:(j + 1) * LANES]


def _s5_scan(bu_ref, st_ref, a_ref, carry_ref, steps):
    nrow = 2 * S5_QUARTERS
    order = [(d, j) for d in range(2) for j in range(S5_SLABS)]
    lanes = lambda j: slice(j * LANES, (j + 1) * LANES)
    carry = {(d, j): (carry_ref[d, 0, :, lanes(j)], carry_ref[d, 1, :, lanes(j)]) for d, j in order}
    for i in steps:
        for d, j in order:
            t = i if d == 0 else S5_TC - 1 - i
            rows = pl.ds(t, nrow, stride=S5_PITCH)
            sr, si = carry[d, j]
            ar, ai = a_ref[d, 0, :, lanes(j)], a_ref[d, 1, :, lanes(j)]
            nr = ar * sr - ai * si + bu_ref[d, j, rows, :]
            ni = ar * si + ai * sr + bu_ref[d, S5_SLABS + j, rows, :]
            st_ref[d, j, rows, :] = nr
            st_ref[d, S5_SLABS + j, rows, :] = ni
            carry[d, j] = (nr, ni)
    for d, j in order:
        carry_ref[d, 0, :, lanes(j)], carry_ref[d, 1, :, lanes(j)] = carry[d, j]


def _s5_project_out(st_ref, d, q, cm_ref, y_ref, rows):
    qsl = slice(q * LANES, (q + 1) * LANES)
    lhs = jnp.concatenate(
        [jnp.concatenate([st_ref[d, j, pl.ds((b * S5_QUARTERS + q) * S5_PITCH, S5_TC), :]
                          for j in range(2 * S5_SLABS)], axis=-1) for b in range(2)], axis=0).astype(BF16)
    y = _dot(lhs, cm_ref[d, q])
    y_ref[0, rows, qsl] = y[0:S5_TC]
    y_ref[1, rows, qsl] = y[S5_TC:2 * S5_TC]


def _s5_body(n_super, has_h0, want_fin, n_aliased, *refs):
    refs = list(refs)
    uf_ref = refs.pop(0)
    ub_ref = refs.pop(0) if n_super > 1 else uf_ref
    bm_ref, cm_ref, a_ref = refs[:3]
    refs = refs[3:]
    h0_ref = refs.pop(0) if has_h0 else None
    refs = refs[n_aliased:]
    yf_ref, yb_ref = refs[:2]
    refs = refs[2:]
    fin_ref = refs.pop(0) if want_fin else None
    bu0_ref, bu1_ref, st0_ref, st1_ref, carry_ref = refs
    s = pl.program_id(0)
    lo, hi = slice(0, S5_TC), slice(S5_TC, 2 * S5_TC)
    n_seg = S5_SEGMENTS
    seg_steps = S5_TC // n_seg
    seg_pairs = 2 * S5_QUARTERS // n_seg

    @pl.when(s == 0)
    def _():
        bu1_ref[...] = jnp.zeros(bu1_ref.shape, F32)
        st0_ref[...] = jnp.zeros(st0_ref.shape, F32)
        carry_ref[...] = jnp.zeros(carry_ref.shape, F32)

    def half(bu_w, bu_r, st_w, st_r, rows_f, rows_b, tag):
        for k in range(n_seg):
            @pl.when(s > -(1 + tag * n_seg + k))
            def _():
                io = ((uf_ref, rows_f, yf_ref), (ub_ref, rows_b, yb_ref))
                sub = seg_steps // seg_pairs
                for m in range(k * seg_pairs, (k + 1) * seg_pairs):
                    d, q = divmod(m, S5_QUARTERS)
                    _s5_project_in(io[d][0], io[d][1], bm_ref, d, q, bu_w)
                    _s5_scan(bu_r, st_w, a_ref, carry_ref, range(m * sub, (m + 1) * sub))
                    _s5_project_out(st_r, d, q, cm_ref, io[d][2], io[d][1])

    half(bu0_ref, bu1_ref, st1_ref, st0_ref, lo, hi, 0)

    @pl.when(s > -(1 + 2 * n_seg))
    def _():
        if want_fin:
            fin_ref[...] = carry_ref[...]
        start = h0_ref[...] if has_h0 else jnp.zeros(carry_ref.shape, F32)
        if n_super > 1:
            start = jnp.where(s % n_super == 0, start, carry_ref[...])
        carry_ref[...] = start

    half(bu1_ref, bu0_ref, st0_ref, st1_ref, hi, lo, 1)


def _s5(u, wts, row0, rows, seq_len, h0=None, want_fin=False, y_prev=None):
    batch = rows // seq_len
    n_super = seq_len // (2 * S5_TC)
    n_steps = (batch // 2) * n_super
    assert not (want_fin and n_super > 1) and row0 % (2 * seq_len) == 0
    pair0 = row0 // (2 * seq_len)
    u3 = u.reshape(u.shape[0] // seq_len, seq_len, S5_WIDTH)
    nrow = 2 * S5_QUARTERS
    blk = (2, 2 * S5_TC, S5_WIDTH)

    def chunk_spec(delay, reverse):
        def index(s):
            s = jnp.clip(s - delay, 0, n_steps - 1)
            c = s % n_super
            return (pair0 + s // n_super, n_super - 1 - c if reverse else c, 0)
        return pl.BlockSpec(blk, index)

    def state_spec(delay):
        return pl.BlockSpec((None, 2, 2, nrow, S5_QLANES),
                            lambda s: (jnp.clip(s - delay, 0, n_steps - 1) // n_super, 0, 0, 0, 0))

    in_specs, args = [chunk_spec(0, False)], [u3]
    if n_super > 1:
        in_specs.append(chunk_spec(0, True))
        args.append(u3)
    in_specs += [_const_spec((2, S5_QUARTERS, LANES, 2 * S5_QLANES), (0, 0, 0, 0)),
                 _const_spec((2, S5_QUARTERS, 2 * S5_QLANES, LANES), (0, 0, 0, 0)),
                 _const_spec((2, 2, nrow, S5_QLANES), (0, 0, 0, 0))]
    args += [wts["s5_b"], wts["s5_c"], wts["s5_a"]]
    if h0 is not None:
        in_specs.append(state_spec(0))
        args.append(h0)
    aliases = {}
    if y_prev is not None:
        for n, y in enumerate(y_prev):
            aliases[len(args)] = n
            in_specs.append(pl.BlockSpec(memory_space=pl.ANY))
            args.append(y.reshape(u3.shape))
    y_shape = jax.ShapeDtypeStruct(u3.shape, F32)
    out_shape, out_specs = [y_shape, y_shape], [chunk_spec(1, False), chunk_spec(1, True)]
    if want_fin:
        out_shape.append(jax.ShapeDtypeStruct((batch // 2, 2, 2, nrow, S5_QLANES), F32))
        out_specs.append(state_spec(1))
    work = pltpu.VMEM((2, 2 * S5_SLABS, nrow * S5_PITCH, LANES), F32)
    return pl.pallas_call(
        functools.partial(_s5_body, n_super, h0 is not None, want_fin, len(aliases)),
        out_shape=out_shape, grid=(n_steps + 1,), in_specs=in_specs, out_specs=out_specs,
        scratch_shapes=[work, work, work, work, pltpu.VMEM((2, 2, nrow, S5_QLANES), F32)],
        input_output_aliases=aliases,
        compiler_params=_params(1), name="s5",
    )(*args)


def _late_weights(hbm_refs, index, vmem_refs, sem):
    first = pl.program_id(0) == 0
    copies = [pltpu.make_async_copy(h.at[index], v, sem.at[n]) for n, (h, v) in enumerate(zip(hbm_refs, vmem_refs))]

    @pl.when(first)
    def _():
        for c in copies:
            c.start()

    def ready():
        @pl.when(first)
        def _():
            for c in copies:
                c.wait()

    return ready


def _mixout_body(n_prompt, x_ref, attn_p, attn_l, u_ref, yf_ref, yb_ref, mods0_ref, mods1_ref, g0_ref, g1_ref,
                 dsk_ref, wglu_ref, bglu_ref, wout_ref, wg0_ref, wu0_ref, wd0_ref, wg1_hbm, wu1_hbm, wd1_hbm, o_ref,
                 wg1_ref, wu1_ref, wd1_ref, sem):
    ready = _late_weights((wg1_hbm, wu1_hbm, wd1_hbm), (1, 0), (wg1_ref, wu1_ref, wd1_ref), sem)
    attn = jnp.where(pl.program_id(0) < n_prompt, attn_p[...], attn_l[...])
    mods = mods0_ref[...]
    g3 = g0_ref[...]
    y = dsk_ref[...] * u_ref[...] + yf_ref[...] + yb_ref[...]
    z = _gelu_tanh(y)
    s5o = z * _sigmoid(_dot(z.astype(BF16), wglu_ref[...]) + bglu_ref[...])
    half = HEADS * V_DIM
    mix = _dot(attn, wout_ref[0:half, :]) + _dot(s5o.astype(BF16), wout_ref[half:, :])
    x2 = x_ref[...] + mods[5:6] * mix
    x3 = _ffn(x2, mods[6:7], mods[7:8], mods[8:9], g3[2:3], wg0_ref, wu0_ref, wd0_ref)
    ready()
    mods = mods1_ref[...]
    o_ref[...] = _ffn(x3, mods[0:1], mods[1:2], mods[2:3], g1_ref[...][0:1], wg1_ref, wu1_ref, wd1_ref)


def _mixout(x, attn, u, yf, yb, mods, norm_g, wts, groups):
    in_specs = [_row_spec(D_MODEL), groups.spec(HEADS * V_DIM, 0), groups.spec(HEADS * V_DIM, 1)]
    in_specs += [_row_spec(S5_WIDTH)] * 3
    args = [x, *attn] + [a.reshape(groups.rows, S5_WIDTH) for a in (u, yf, yb)]
    in_specs += [_mods_spec(0, groups.cond), _mods_spec(1, groups.cond), _normg_spec(0), _normg_spec(1),
                 _const_spec((1, S5_WIDTH), (0, 0)), _const_spec((S5_WIDTH, S5_WIDTH), (0, 0)),
                 _const_spec((1, S5_WIDTH), (0, 0)), _const_spec((D_MODEL, D_MODEL), (0, 0))]
    late = wts["ffn"][1, 0]
    in_specs += _ffn_specs(wts["ffn"][0, 1], 0, 1) + [pl.BlockSpec(memory_space=pl.ANY)] * len(late)
    args += [mods, mods, norm_g, norm_g, wts["s5_d"], wts["w_glu"], wts["b_glu"], wts["w_out"],
             *wts["ffn"][0, 1], *late]
    return pl.pallas_call(
        functools.partial(_mixout_body, groups.prompt_tiles),
        out_shape=jax.ShapeDtypeStruct((groups.rows, D_MODEL), F32),
        grid=(groups.tiles,), in_specs=in_specs, out_specs=_row_spec(D_MODEL),
        scratch_shapes=[pltpu.VMEM(w.shape[2:], w.dtype) for w in late] + [pltpu.SemaphoreType.DMA((len(late),))],
        compiler_params=_params(1), name="mixout_ffn",
    )(*args)


def _conv_body(n_prompt, seqs, x_ref, xp_ref, xn_ref, mods_ref, g_ref, cwin_ref, cw_ref, cwout_ref,
               wg_hbm, wu_hbm, wd_hbm, op_ref, ol_ref, wg_ref, wu_ref, wd_ref, sem):
    ready = _late_weights((wg_hbm, wu_hbm, wd_hbm), (1, 1), (wg_ref, wu_ref, wd_ref), sem)
    i = pl.program_id(0)
    is_prompt = i < n_prompt
    mods = mods_ref[...]
    g3 = g_ref[...]
    x = x_ref[...]
    n_ext = TM + 2 * SUBLANES
    xe = jnp.concatenate([xp_ref[...], x, xn_ref[...]], axis=0)
    hne = _modulate(xe, g3[1:2], mods[3:4], mods[4:5]).astype(BF16)
    pz = _dot(hne, cwin_ref[:, D_MODEL:3 * D_MODEL])
    z = pz[:, 0:D_MODEL] * pz[:, D_MODEL:2 * D_MODEL]
    gate_b = _dot(hne[SUBLANES:SUBLANES + TM], cwin_ref[:, 0:D_MODEL])
    main = slice(SUBLANES, SUBLANES + TM)
    row = i * TM + lax.broadcasted_iota(jnp.int32, (TM, 1), 0)
    pos = jnp.where(is_prompt, row % seqs[0], row % seqs[1])
    end = jnp.where(is_prompt, seqs[0] - 1, seqs[1] - 1)
    z_prev = jnp.where(pos == 0, 0.0, pltpu.roll(z, 1, 0)[main])
    z_next = jnp.where(pos == end, 0.0, pltpu.roll(z, n_ext - 1, 0)[main])
    cw = cw_ref[...]
    zc = z_prev * cw[0:1] + z[main] * cw[1:2] + z_next * cw[2:3]
    mix = _dot((gate_b * zc).astype(BF16), cwout_ref[...])
    x2 = x + mods[5:6] * mix
    ready()
    out = _ffn(x2, mods[6:7], mods[7:8], mods[8:9], g3[2:3], wg_ref, wu_ref, wd_ref)

    @pl.when(is_prompt)
    def _():
        op_ref[...] = out

    @pl.when(jnp.logical_not(is_prompt))
    def _():
        ol_ref[...] = out


def _conv_mixer(x, mods, norm_g, wts, groups):
    per = TM // SUBLANES
    last = groups.rows // SUBLANES - 1
    halo = (SUBLANES, D_MODEL)
    in_specs = [_row_spec(D_MODEL),
                pl.BlockSpec(halo, lambda i: (jnp.maximum(i * per - 1, 0), 0)),
                pl.BlockSpec(halo, lambda i: (jnp.minimum((i + 1) * per, last), 0)),
                _mods_spec(1, groups.cond), _normg_spec(1),
                _const_spec((D_MODEL, 3 * D_MODEL), (0, 0)), _const_spec((CONV_K, D_MODEL), (0, 0)),
                _const_spec((D_MODEL, D_MODEL), (0, 0))]
    late = wts["ffn"][1, 1]
    in_specs += [pl.BlockSpec(memory_space=pl.ANY)] * len(late)
    return pl.pallas_call(
        functools.partial(_conv_body, groups.prompt_tiles, groups.seq),
        out_shape=[jax.ShapeDtypeStruct((r, D_MODEL), F32) for r in groups.group_rows],
        grid=(groups.tiles,), in_specs=in_specs, out_specs=[groups.spec(D_MODEL, 0), groups.spec(D_MODEL, 1)],
        scratch_shapes=[pltpu.VMEM(w.shape[2:], w.dtype) for w in late] + [pltpu.SemaphoreType.DMA((len(late),))],
        compiler_params=_params(1), name="conv_ffn",
    )(x, x, x, mods, norm_g, wts["conv_w_in"], wts["conv_w"], wts["conv_w_out"], *late)


ROPE_PARTNER = np.concatenate([np.arange(8, 16), np.arange(0, 8), np.arange(24, 32), np.arange(16, 24)])


def _rope_tables(seq_len, g_q, g_k):
    t = np.arange(seq_len)
    quarter = ROPE // 4
    inv_freq = ROPE_BASE ** (-np.arange(quarter, dtype=np.float64) / quarter)
    cos = np.ones((seq_len, LANES))
    sin = np.zeros((seq_len, LANES))
    for base, pos in ((NOPE, t // GRID_W), (NOPE + ROPE // 2, t % GRID_W)):
        ang = pos[:, None].astype(np.float64) * inv_freq[None, :]
        cos[:, base:base + quarter] = np.cos(ang)
        cos[:, base + quarter:base + 2 * quarter] = np.cos(ang)
        sin[:, base:base + quarter] = -np.sin(ang)
        sin[:, base + quarter:base + 2 * quarter] = np.sin(ang)
    cos, sin = jnp.asarray(cos, F32), jnp.asarray(sin, F32)
    out = []
    for g in (g_q, g_k):
        g = g.reshape(QK_DIM)
        g_pad = jnp.pad(g, (0, HEAD_PAD - QK_DIM))
        g_partner = jnp.pad(g[NOPE:][ROPE_PARTNER], (NOPE, HEAD_PAD - QK_DIM))
        out += [g_pad[None, :] * cos, g_partner[None, :] * sin]
    return tuple(out)


def _s5_params(lam_re, lam_im, log_dt, b_re, b_im, c_re, c_im):
    dt = jnp.exp(log_dt)[..., None]
    lr = jnp.minimum(lam_re, LAMBDA_RE_MAX)
    li = lam_im
    mag = jnp.exp(lr * dt)
    ang = li * dt
    ab_re = mag * jnp.cos(ang)
    ab_im = mag * jnp.sin(ang)
    den = lr * lr + li * li
    nr = ab_re - 1.0
    ni = ab_im
    co_re = (nr * lr + ni * li) / den
    co_im = (ni * lr - nr * li) / den
    bb_re = co_re[..., None] * b_re - co_im[..., None] * b_im
    bb_im = co_re[..., None] * b_im + co_im[..., None] * b_re
    per_q = S5_GROUPS // S5_QUARTERS
    eye = jnp.eye(per_q, dtype=F32)

    def in_mat(bb):
        bb = bb.reshape(2, S5_QUARTERS, per_q, S5_STATE, S5_GROUP)
        return jnp.einsum("dqgpi,gh->dqgihp", bb, eye).reshape(2, S5_QUARTERS, LANES, S5_QLANES)

    def out_mat(cc):
        cc = cc.reshape(2, S5_QUARTERS, per_q, S5_GROUP, S5_STATE)
        return jnp.einsum("dqgip,gh->dqgphi", cc, eye).reshape(2, S5_QUARTERS, S5_QLANES, LANES)

    bm = jnp.concatenate([in_mat(bb_re), in_mat(bb_im)], axis=-1).astype(BF16)
    cm = jnp.concatenate([out_mat(c_re), -out_mat(c_im)], axis=-2).astype(BF16)

    def rows(a):
        a = a.reshape(2, 1, S5_QUARTERS, S5_QLANES)
        return jnp.broadcast_to(a, (2, 2, S5_QUARTERS, S5_QLANES)).reshape(2, 2 * S5_QUARTERS, S5_QLANES)

    return bm, cm, jnp.stack([rows(ab_re), rows(ab_im)], axis=1)


def _state_rows(s):
    b = s.shape[0]
    s = s.reshape(b // 2, 2, 2, S5_QUARTERS, S5_QLANES)
    return s.transpose(0, 2, 1, 3, 4).reshape(b // 2, 2, 2 * S5_QUARTERS, S5_QLANES)


def _state_unrows(s, batch):
    s = s.reshape(batch // 2, 2, 2, S5_QUARTERS, S5_QLANES).transpose(0, 2, 1, 3, 4)
    return s.reshape(batch, 1, 2, S5_GROUPS, S5_STATE)


def kernel(x_prompt, x_sample, cache_ckv, cache_krope, state_ssm_re, state_ssm_im, c, c_ctx, w_ada, b_ada, norm_g, ffn_w_gate, ffn_w_up, ffn_w_down, ab_w_in, mla_g_q_lat, mla_g_kv_lat, mla_w_uq, mla_w_ukv, mla_g_qnorm, mla_g_knorm, s5_lam_re, s5_lam_im, s5_log_dt, s5_b_re, s5_b_im, s5_c_re, s5_c_im, s5_d, s5_w_glu, s5_b_glu, ab_w_out, conv_w_in, conv_w, conv_w_out):
    batch, seq, _ = x_prompt.shape
    dec_batch, dec_seq, _ = x_sample.shape
    past = cache_ckv.shape[2]

    w_in = ab_w_in[0]
    kr_cols = w_in[:, Q_LORA + KV_LORA:Q_LORA + KV_LORA + ROPE]
    zeros = jnp.zeros_like(kr_cols)
    w_uq = jnp.pad(mla_w_uq[0].reshape(Q_LORA, HEADS, QK_DIM), ((0, 0), (0, 0), (0, HEAD_PAD - QK_DIM)))
    w_ukv = mla_w_ukv[0].reshape(KV_LORA, HEADS, NOPE + V_DIM)
    w_uk = jnp.pad(w_ukv[:, :, :NOPE], ((0, 0), (0, 0), (0, HEAD_PAD - NOPE)))
    bm, cm, a_rows = _s5_params(s5_lam_re[0], s5_lam_im[0], s5_log_dt[0], s5_b_re[0], s5_b_im[0],
                                s5_c_re[0], s5_c_im[0])
    stacked = (ffn_w_gate.astype(BF16), ffn_w_up.astype(BF16), ffn_w_down.astype(BF16))
    ffn = {(l, s): stacked for l in range(2) for s in range(2)}
    w_in_cols = [w_in[:, :Q_LORA + KV_LORA], w_in[:, Q_LORA + KV_LORA + ROPE:], kr_cols, zeros, kr_cols, zeros]
    w_in_partner = [zeros, zeros, kr_cols[:, ROPE_PARTNER], zeros]
    w_uq3 = mla_w_uq[0].reshape(Q_LORA, HEADS, QK_DIM)
    w_uq_partner = jnp.pad(w_uq3[:, :, NOPE:][:, :, ROPE_PARTNER], ((0, 0), (0, 0), (NOPE, HEAD_PAD - QK_DIM)))
    wts = {
        "ffn": ffn,
        "w_in_rot": jnp.concatenate(w_in_cols + w_in_partner, axis=1).astype(BF16),
        "g_q_lat": mla_g_q_lat, "g_kv_lat": mla_g_kv_lat,
        "w_uq_rot": jnp.concatenate([w_uq, w_uq_partner], axis=1).reshape(Q_LORA, 2 * HEADS * HEAD_PAD).astype(BF16),
        "w_ukv": jnp.concatenate([w_uk.reshape(KV_LORA, HEADS * HEAD_PAD),
                                  w_ukv[:, :, NOPE:].reshape(KV_LORA, HEADS * V_DIM)], axis=1).astype(BF16),
        "g_qn": jnp.pad(mla_g_qnorm, ((0, 0), (0, HEAD_PAD - QK_DIM))),
        "g_kn": jnp.pad(mla_g_knorm, ((0, 0), (0, HEAD_PAD - QK_DIM))),
        "rope": _rope_tables(dec_seq, mla_g_qnorm, mla_g_knorm),
        "s5_b": bm, "s5_c": cm, "s5_a": a_rows,
        "s5_d": s5_d, "w_glu": s5_w_glu[0].astype(BF16), "b_glu": s5_b_glu, "w_out": ab_w_out[0].astype(BF16),
        "conv_w_in": conv_w_in[0].astype(BF16), "conv_w": conv_w[0].T, "conv_w_out": conv_w_out[0].astype(BF16),
    }

    cond = jnp.zeros((SUBLANES, D_MODEL), F32).at[0].set(c_ctx).at[1:1 + dec_batch].set(c)
    mods = _ada(cond, w_ada, b_ada).reshape(w_ada.shape[0], SUBLANES, N_MOD, D_MODEL)

    groups = _Groups(batch * seq, seq, dec_batch * dec_seq, dec_seq)
    rows_p, rows_l = groups.group_rows
    x1, q, k, v, u, ckv_p, kr_p = _mixin(
        x_prompt.reshape(rows_p, D_MODEL), x_sample.reshape(rows_l, D_MODEL), mods, norm_g, wts, groups)
    kr_pad = jnp.pad(cache_krope[:, 0].reshape(dec_batch * past, ROPE), ((0, 0), (NOPE, LANES - QK_DIM)))
    ctx = _ctx_kv(cache_ckv[:, 0].reshape(dec_batch * past, KV_LORA), kr_pad, wts)
    h0 = jnp.stack([_state_rows(state_ssm_re[:, 0]), _state_rows(state_ssm_im[:, 0])], axis=2)
    attn_p, yf, yb = _attention(q, k, v, 0, rows_p, seq, seq, n_seq=2, zeros=[(groups.rows, S5_WIDTH)] * 2)
    attn = (attn_p, _attention(q, k, v, rows_p, rows_l, dec_seq, 512, ctx)[0])
    yf, yb, fin = _s5(u, wts, 0, rows_p, seq, want_fin=True, y_prev=(yf, yb))
    yf, yb = _s5(u, wts, rows_p, rows_l, dec_seq, h0=h0, y_prev=(yf, yb))
    x3 = _mixout(x1, attn, u, yf, yb, mods, norm_g, wts, groups)
    y_p, y_s = _conv_mixer(x3, mods, norm_g, wts, groups)
    return (y_p.reshape(batch, seq, D_MODEL), y_s.reshape(dec_batch, dec_seq, D_MODEL),
            ckv_p.reshape(batch, 1, seq, KV_LORA), kr_p.reshape(batch, 1, seq, ROPE),
            _state_unrows(fin[:, :, 0], batch), _state_unrows(fin[:, :, 1], batch))
```

```python
import functools
import math

import jax
import jax.numpy as jnp
import numpy as np
from jax import lax
from jax.experimental import pallas as pl
from jax.experimental.pallas import tpu as pltpu

F32 = jnp.float32
BF16 = jnp.bfloat16

LANES = 128
SUBLANES = 8
VMEM_LIMIT_BYTES = 60 * 1024 * 1024

D_MODEL = 1024
D_FF = 2816
N_MOD = 9
EPS = 1e-6
HEADS = 8
Q_LORA = 384
KV_LORA = 256
NOPE = 64
ROPE = 32
V_DIM = 64
QK_DIM = NOPE + ROPE
HEAD_PAD = LANES
ROPE_BASE = 10000.0
GRID_W = 64
S5_WIDTH = 512
S5_GROUP = 16
S5_GROUPS = 32
S5_STATE = 64
S5_LANES = S5_GROUPS * S5_STATE
S5_QUARTERS = 4
S5_QLANES = S5_LANES // S5_QUARTERS
S5_SLABS = S5_QLANES // LANES
S5_TC = 128
S5_PITCH = S5_TC + 4
S5_SEGMENTS = 1
LAMBDA_RE_MAX = -1e-4
CONV_K = 3

ATT_ROWS = 2 * SUBLANES
TM = 512
FF_CHUNKS = tuple((c, min(512, D_FF - c)) for c in range(0, D_FF, 512))

O_CKV = Q_LORA
O_U = Q_LORA + KV_LORA
O_KR = O_U + S5_WIDTH


def _dot(a, b):
    return jnp.dot(a, b, preferred_element_type=F32)


def _sigmoid(x):
    return 1.0 / (1.0 + jnp.exp(-x))


def _rms(x, g, n):
    ms = jnp.sum(x * x, axis=-1, keepdims=True) * (1.0 / n)
    return x * lax.rsqrt(ms + EPS) * g


def _modulate(x, g, shift, scale):
    return _rms(x, g, D_MODEL) * (1.0 + scale) + shift


def _ffn(x, shift, scale, gate, g, wg_ref, wu_ref, wd_ref):
    h = _modulate(x, g, shift, scale).astype(BF16)
    acc = jnp.zeros(x.shape, F32)
    for c0, cs in FF_CHUNKS:
        gt = _dot(h, wg_ref[:, c0:c0 + cs])
        up = _dot(h, wu_ref[:, c0:c0 + cs])
        act = ((gt * _sigmoid(gt)) * up).astype(BF16)
        acc = acc + _dot(act, wd_ref[c0:c0 + cs, :])
    return x + (0.5 * gate) * acc


def _head_inv_rms(xh):
    return lax.rsqrt(jnp.sum(xh * xh, axis=-1, keepdims=True) * (1.0 / QK_DIM) + EPS)


def _head_norm(xh, g):
    return xh * _head_inv_rms(xh) * g


def _gelu_tanh(x):
    return x * (0.5 * (1.0 + jnp.tanh(math.sqrt(2.0 / math.pi) * (x + 0.044715 * (x * x * x)))))


def _const_spec(shape, index):
    return pl.BlockSpec(shape, lambda *_: index, pipeline_mode=pl.Buffered(1))


def _params(n_axes):
    return pltpu.CompilerParams(dimension_semantics=("arbitrary",) * n_axes, vmem_limit_bytes=VMEM_LIMIT_BYTES)


def _ffn_specs(weights, layer, sub):
    return [_const_spec(w.shape, (0, 0)) if w.ndim == 2 else
            _const_spec((None, None) + w.shape[2:], (layer, sub, 0, 0)) for w in weights]


def _mods_spec(layer, cond_of_tile):
    return pl.BlockSpec((None, None, N_MOD, D_MODEL), lambda i: (layer, cond_of_tile(i), 0, 0))


def _normg_spec(layer):
    return _const_spec((None, 3, D_MODEL), (layer, 0, 0))


def _row_spec(width, rows=TM):
    return pl.BlockSpec((rows, width), lambda i: (i, 0))


N_FFN_MATS = 3


def _cast_job(weights, layer, sub, n_steps, step_of):
    in_specs, out_specs, out_shape = [], [], []
    for w in weights:
        r, c = w.shape[2:]
        assert r % (n_steps * 2 * SUBLANES) == 0
        slab = lambda *g: jnp.minimum(step_of(*g), n_steps - 1)
        in_specs.append(pl.BlockSpec((None, None, r // n_steps, c), lambda *g, slab=slab: (layer, sub, slab(*g), 0)))
        out_specs.append(pl.BlockSpec((r // n_steps, c), lambda *g, slab=slab: (slab(*g), 0)))
        out_shape.append(jax.ShapeDtypeStruct((r, c), BF16))
    return in_specs, out_specs, out_shape


def _cast_slabs(src_refs, dst_refs):
    for src, dst in zip(src_refs, dst_refs):
        dst[...] = src[...].astype(BF16)


class _Groups:
    def __init__(self, prompt_rows, prompt_seq, latent_rows, latent_seq):
        assert prompt_rows % TM == 0 and latent_seq % TM == 0 and TM % prompt_seq == 0
        assert prompt_rows % latent_seq == 0
        self.group_rows = (prompt_rows, latent_rows)
        self.seq = (prompt_seq, latent_seq)
        self.rows = prompt_rows + latent_rows
        self.prompt_tiles = prompt_rows // TM
        self.tiles = self.rows // TM
        self.latent_tiles_per_seq = latent_seq // TM

    def cond(self, i):
        return jnp.where(i < self.prompt_tiles, 0,
                         1 + jnp.maximum(i - self.prompt_tiles, 0) // self.latent_tiles_per_seq)

    def spec(self, width, group):
        n = self.prompt_tiles
        if group == 0:
            return pl.BlockSpec((TM, width), lambda i: (jnp.minimum(i, n - 1), 0))
        return pl.BlockSpec((TM, width), lambda i: (jnp.maximum(i - n, 0), 0))


def _ada_body(c_ref, w_ref, b_ref, o_ref):
    c = c_ref[...]
    s = (c * _sigmoid(c)).astype(BF16)
    o_ref[...] = _dot(s, w_ref[...].astype(BF16)) + b_ref[...]


def _ada(cond, w_ada, b_ada):
    depth, _, n = w_ada.shape
    tn = 2304
    return pl.pallas_call(
        _ada_body,
        out_shape=jax.ShapeDtypeStruct((depth, SUBLANES, n), F32),
        grid=(depth, n // tn),
        in_specs=[pl.BlockSpec((SUBLANES, D_MODEL), lambda l, j: (0, 0)),
                  pl.BlockSpec((None, D_MODEL, tn), lambda l, j: (l, 0, j)),
                  pl.BlockSpec((None, 1, tn), lambda l, j: (l, 0, j))],
        out_specs=pl.BlockSpec((None, SUBLANES, tn), lambda l, j: (l, 0, j)),
        compiler_params=_params(2),
        name="ada",
    )(cond, w_ada, b_ada.reshape(depth, 1, n))


def _mixin_body(n_prompt, *refs):
    (xp_ref, xs_ref, mods_ref, g_ref, wg_ref, wu_ref, wd_ref, win_ref, gql_ref, gkv_ref, wuq_ref, wukv_ref,
     gq_ref, gk_ref, rqc_ref, rqs_ref, rkc_ref, rks_ref,
     x1_ref, q_ref, k_ref, v_ref, u_ref, ckv_ref, kr_ref) = refs
    is_prompt = pl.program_id(0) < n_prompt
    mods = mods_ref[...]
    g3 = g_ref[...]
    x = jnp.where(is_prompt, xp_ref[...], xs_ref[...])
    x1 = _ffn(x, mods[0:1], mods[1:2], mods[2:3], g3[0:1], wg_ref, wu_ref, wd_ref)
    x1_ref[...] = x1
    hn = _modulate(x1, g3[1:2], mods[3:4], mods[4:5]).astype(BF16)
    proj = _dot(hn, win_ref[:, 0:O_KR + LANES])
    u_ref[...] = proj[:, O_U:O_KR]
    ckv = _rms(proj[:, O_CKV:O_U], gkv_ref[...], KV_LORA)
    krg = proj[:, O_KR:O_KR + LANES]
    qn = _rms(proj[:, 0:Q_LORA], gql_ref[...], Q_LORA).astype(BF16)
    kv = _dot(ckv.astype(BF16), wukv_ref[...])
    v_ref[...] = kv[:, HEADS * HEAD_PAD:].astype(BF16)
    lane = lax.broadcasted_iota(jnp.int32, (1, LANES), 1)
    kr_only = jnp.where((lane >= NOPE) & (lane < QK_DIM), krg, 0.0)
    gk = gk_ref[...]
    head = lambda h: slice(h * HEAD_PAD, (h + 1) * HEAD_PAD)

    @pl.when(is_prompt)
    def _():
        ckv_ref[...] = ckv
        kr_ref[...] = krg[:, 0:ROPE]
        qraw = _dot(qn, wuq_ref[:, 0:HEADS * HEAD_PAD])
        gq = gq_ref[...]
        for h in range(HEADS):
            q_ref[:, head(h)] = _head_norm(qraw[:, head(h)], gq).astype(BF16)
            k_ref[:, head(h)] = _head_norm(kv[:, head(h)] + kr_only, gk).astype(BF16)

    @pl.when(jnp.logical_not(is_prompt))
    def _():
        qraw = _dot(qn, wuq_ref[...])
        kr_partner = _dot(hn, win_ref[:, O_KR + LANES:O_KR + 2 * LANES])
        k_rot = kr_only * rkc_ref[...] + kr_partner * rks_ref[...]
        for h in range(HEADS):
            qa = qraw[:, head(h)]
            qh = (qa * rqc_ref[...] + qraw[:, head(HEADS + h)] * rqs_ref[...]) * _head_inv_rms(qa)
            kh = (kv[:, head(h)] * gk + k_rot) * _head_inv_rms(kv[:, head(h)] + kr_only)
            q_ref[:, head(h)] = qh.astype(BF16)
            k_ref[:, head(h)] = kh.astype(BF16)


def _mixin(x_prompt, x_latent, mods, norm_g, wts, groups):
    n_prompt = groups.prompt_tiles
    w_in, w_uq = wts["w_in_rot"], wts["w_uq_rot"]
    in_specs = [groups.spec(D_MODEL, 0), groups.spec(D_MODEL, 1), _mods_spec(0, groups.cond), _normg_spec(0)]
    in_specs += _ffn_specs(wts["ffn"][0, 0], 0, 0) + [
        _const_spec(w_in.shape, (0, 0)),
        _const_spec((1, Q_LORA), (0, 0)),
        _const_spec((1, KV_LORA), (0, 0)),
        _const_spec(w_uq.shape, (0, 0)),
        _const_spec((KV_LORA, HEADS * HEAD_PAD + HEADS * V_DIM), (0, 0)),
        _const_spec((1, HEAD_PAD), (0, 0)),
        _const_spec((1, HEAD_PAD), (0, 0)),
    ] + [pl.BlockSpec((TM, LANES), lambda i: (jnp.maximum(i - n_prompt, 0) % groups.latent_tiles_per_seq, 0))] * 4
    args = [x_prompt, x_latent, mods, norm_g, *wts["ffn"][0, 0], w_in, wts["g_q_lat"],
            wts["g_kv_lat"], w_uq, wts["w_ukv"], wts["g_qn"], wts["g_kn"]] + list(wts["rope"])
    out_shape, out_specs = [], []
    for w, dt in ((D_MODEL, F32), (HEADS * HEAD_PAD, BF16), (HEADS * HEAD_PAD, BF16), (HEADS * V_DIM, BF16),
                  (S5_WIDTH, F32)):
        out_shape.append(jax.ShapeDtypeStruct((groups.rows, w), dt))
        out_specs.append(_row_spec(w))
    for w in (KV_LORA, ROPE):
        out_shape.append(jax.ShapeDtypeStruct((groups.group_rows[0], w), F32))
        out_specs.append(groups.spec(w, 0))
    return pl.pallas_call(
        functools.partial(_mixin_body, n_prompt),
        out_shape=out_shape, grid=(groups.tiles,), in_specs=in_specs, out_specs=out_specs,
        compiler_params=_params(1), name="ffn_mixin",
    )(*args)


def _ctx_body(ckv_ref, krp_ref, wukv_ref, gk_ref, k_ref, v_ref):
    kv = _dot(ckv_ref[...].astype(BF16), wukv_ref[...])
    v_ref[...] = kv[:, HEADS * HEAD_PAD:].astype(BF16)
    gk = gk_ref[...]
    krp = krp_ref[...]
    for h in range(HEADS):
        sl = slice(h * HEAD_PAD, (h + 1) * HEAD_PAD)
        k_ref[:, sl] = _head_norm(kv[:, sl] + krp, gk).astype(BF16)


def _ctx_kv(ckv, kr_padded, wts):
    rows = ckv.shape[0]
    return pl.pallas_call(
        _ctx_body,
        out_shape=[jax.ShapeDtypeStruct((rows, HEADS * HEAD_PAD), BF16),
                   jax.ShapeDtypeStruct((rows, HEADS * V_DIM), BF16)],
        grid=(rows // TM,),
        in_specs=[_row_spec(KV_LORA), _row_spec(LANES),
                  _const_spec((KV_LORA, HEADS * HEAD_PAD + HEADS * V_DIM), (0, 0)),
                  _const_spec((1, HEAD_PAD), (0, 0))],
        out_specs=[_row_spec(HEADS * HEAD_PAD), _row_spec(HEADS * V_DIM)],
        compiler_params=_params(1), name="ctx_kv",
    )(ckv, kr_padded, wts["w_ukv"], wts["g_kn"])


def _attn_body(n_kv, n_seq, n_zero, n_cast, *refs):
    refs = list(refs)
    q_ref = refs.pop(0)
    k_refs, v_refs, cast_in = refs[:n_kv], refs[n_kv:2 * n_kv], refs[2 * n_kv:2 * n_kv + n_cast]
    refs = refs[2 * n_kv + n_cast:]
    o_ref = refs.pop(0)
    zero_refs, cast_out = refs[:n_zero], refs[n_zero:n_zero + n_cast]
    s_ref, p_ref, r_ref = refs[n_zero + n_cast:]
    _cast_slabs(cast_in, cast_out)
    for z_ref in zero_refs:
        z_ref[...] = jnp.zeros(z_ref.shape, z_ref.dtype)
    tq = q_ref.shape[0] // n_seq
    c = (QK_DIM ** -0.5) * math.log2(math.e)
    lane = lax.broadcasted_iota(jnp.int32, (1, LANES), 1)
    n_keys = [k_ref.shape[0] // n_seq for k_ref in k_refs]
    spans = [slice(sum(n_keys[:i]), sum(n_keys[:i + 1])) for i in range(n_kv)]
    for sq, pair in [(sq, pair) for sq in range(n_seq) for pair in range(HEADS // 2)]:
        qrows = slice(sq * tq, (sq + 1) * tq)
        krows = [slice(sq * n, (sq + 1) * n) for n in n_keys]
        wide = slice(pair // 2 * 2 * LANES, (pair // 2 + 1) * 2 * LANES)
        mine = slice(pair % 2 * LANES, (pair % 2 + 1) * LANES)
        vsl = slice(pair * LANES, (pair + 1) * LANES)
        outs = []
        for h in (2 * pair, 2 * pair + 1):
            slot = h % 2
            sl = slice(h * HEAD_PAD, (h + 1) * HEAD_PAD)
            q = q_ref[qrows, sl]
            for k_ref, kr, span in zip(k_refs, krows, spans):
                s_ref[slot, :, span] = lax.dot_general(q, k_ref[kr, sl], (((1,), (1,)), ((), ())),
                                                       preferred_element_type=F32)
            for r0 in range(0, tq, ATT_ROWS):
                rows = slice(r0, r0 + ATT_ROWS)
                s = s_ref[slot, rows, :]
                e = jnp.exp2((s - jnp.max(s, axis=-1, keepdims=True)) * c)
                r_ref[slot, rows, :] = jnp.broadcast_to(1.0 / jnp.sum(e, axis=-1, keepdims=True), (ATT_ROWS, LANES))
                p_ref[slot, rows, :] = e.astype(BF16)
            o = functools.reduce(jnp.add, [_dot(p_ref[slot, :, span], v_ref[kr, wide])
                                           for v_ref, kr, span in zip(v_refs, krows, spans)])
            outs.append(o[:, mine] * r_ref[slot])
        o_ref[qrows, vsl] = jnp.where(lane < V_DIM, outs[0], outs[1]).astype(BF16)


def _attention(q, k, v, row0, rows, seq_len, tq, ctx=None, n_seq=1, zeros=(), cast=None):
    q_tiles = seq_len // tq
    n_seq = n_seq if q_tiles == 1 and ctx is None else 1
    q0, kv0 = row0 // (n_seq * tq), row0 // (n_seq * seq_len)
    kv_spec = lambda w: pl.BlockSpec((n_seq * seq_len, w), lambda b, t: (kv0 + b, 0))
    in_specs = [pl.BlockSpec((n_seq * tq, HEADS * HEAD_PAD), lambda b, t: (q0 + b * q_tiles + t, 0))]
    ks, vs = [kv_spec(HEADS * HEAD_PAD)], [kv_spec(HEADS * V_DIM)]
    kargs, vargs = [k], [v]
    n_keys = seq_len
    if ctx is not None:
        k_c, v_c = ctx
        past = k_c.shape[0] // (rows // seq_len)
        ks.insert(0, pl.BlockSpec((past, HEADS * HEAD_PAD), lambda b, t: (b, 0)))
        vs.insert(0, pl.BlockSpec((past, HEADS * V_DIM), lambda b, t: (b, 0)))
        kargs.insert(0, k_c)
        vargs.insert(0, v_c)
        n_keys += past
    grid = (rows // (n_seq * seq_len), q_tiles)
    out_shape = [jax.ShapeDtypeStruct((rows, HEADS * V_DIM), BF16)]
    out_specs = [pl.BlockSpec((n_seq * tq, HEADS * V_DIM), lambda b, t: (b * q_tiles + t, 0))]
    for shape in zeros:
        assert q_tiles == 1 and shape[0] % (SUBLANES * grid[0]) == 0
        out_shape.append(jax.ShapeDtypeStruct(shape, F32))
        out_specs.append(pl.BlockSpec((shape[0] // grid[0], shape[1]), lambda b, t: (b, 0)))
    in_specs, args = in_specs + ks + vs, [q, *kargs, *vargs]
    if cast is not None:
        weights, layer, sub = cast
        c_in, c_out, c_shape = _cast_job(weights, layer, sub, grid[0] * grid[1], lambda b, t: b * q_tiles + t)
        in_specs, args = in_specs + c_in, args + list(weights)
        out_specs, out_shape = out_specs + c_out, out_shape + c_shape
    return pl.pallas_call(
        functools.partial(_attn_body, len(kargs), n_seq, len(zeros), 0 if cast is None else N_FFN_MATS),
        out_shape=out_shape, grid=grid, in_specs=in_specs, out_specs=out_specs,
        scratch_shapes=[pltpu.VMEM((2, tq, n_keys), F32), pltpu.VMEM((2, tq, n_keys), BF16),
                        pltpu.VMEM((2, tq, LANES), F32)],
        compiler_params=_params(2), name="attention",
    )(*args)


def _s5_project_in(u_ref, rows, bm_ref, d, q, bu_ref):
    qsl = slice(q * LANES, (q + 1) * LANES)
    lhs = jnp.concatenate([u_ref[0, rows, qsl], u_ref[1, rows, qsl]], axis=0).astype(BF16)
    bu = _dot(lhs, bm_ref[d, q])
    for b in range(2):
        r = b * S5_QUARTERS + q
        for j in range(2 * S5_SLABS):
            bu_ref[d, j, pl.ds(r * S5_PITCH, S5_TC), :] = bu[b * S5_TC:(b + 1) * S5_TC, j * LANES:(j + 1) * LANES]


def _s5_scan(bu_ref, st_ref, a_ref, carry_ref, steps):
    nrow = 2 * S5_QUARTERS
    order = [(d, j) for d in range(2) for j in range(S5_SLABS)]
    lanes = lambda j: slice(j * LANES, (j + 1) * LANES)
    carry = {(d, j): (carry_ref[d, 0, :, lanes(j)], carry_ref[d, 1, :, lanes(j)]) for d, j in order}
    for i in steps:
        for d, j in order:
            t = i if d == 0 else S5_TC - 1 - i
            rows = pl.ds(t, nrow, stride=S5_PITCH)
            sr, si = carry[d, j]
            ar, ai = a_ref[d, 0, :, lanes(j)], a_ref[d, 1, :, lanes(j)]
            nr = ar * sr - ai * si + bu_ref[d, j, rows, :]
            ni = ar * si + ai * sr + bu_ref[d, S5_SLABS + j, rows, :]
            st_ref[d, j, rows, :] = nr
            st_ref[d, S5_SLABS + j, rows, :] = ni
            carry[d, j] = (nr, ni)
    for d, j in order:
        carry_ref[d, 0, :, lanes(j)], carry_ref[d, 1, :, lanes(j)] = carry[d, j]


def _s5_project_out(st_ref, d, q, cm_ref, y_ref, rows):
    qsl = slice(q * LANES, (q + 1) * LANES)
    lhs = jnp.concatenate(
        [jnp.concatenate([st_ref[d, j, pl.ds((b * S5_QUARTERS + q) * S5_PITCH, S5_TC), :]
                          for j in range(2 * S5_SLABS)], axis=-1) for b in range(2)], axis=0).astype(BF16)
    y = _dot(lhs, cm_ref[d, q])
    y_ref[0, rows, qsl] = y[0:S5_TC]
    y_ref[1, rows, qsl] = y[S5_TC:2 * S5_TC]


def _s5_body(n_super, has_h0, want_fin, n_aliased, n_cast, *refs):
    refs = list(refs)
    uf_ref = refs.pop(0)
    ub_ref = refs.pop(0) if n_super > 1 else uf_ref
    bm_ref, cm_ref, a_ref = refs[:3]
    refs = refs[3:]
    h0_ref = refs.pop(0) if has_h0 else None
    refs = refs[n_aliased:]
    cast_in, refs = refs[:n_cast], refs[n_cast:]
    yf_ref, yb_ref = refs[:2]
    refs = refs[2:]
    fin_ref = refs.pop(0) if want_fin else None
    cast_out, refs = refs[:n_cast], refs[n_cast:]
    bu0_ref, bu1_ref, st0_ref, st1_ref, carry_ref = refs
    _cast_slabs(cast_in, cast_out)
    s = pl.program_id(0)
    lo, hi = slice(0, S5_TC), slice(S5_TC, 2 * S5_TC)
    n_seg = S5_SEGMENTS
    seg_steps = S5_TC // n_seg
    seg_pairs = 2 * S5_QUARTERS // n_seg

    @pl.when(s == 0)
    def _():
        bu1_ref[...] = jnp.zeros(bu1_ref.shape, F32)
        st0_ref[...] = jnp.zeros(st0_ref.shape, F32)
        carry_ref[...] = jnp.zeros(carry_ref.shape, F32)

    def half(bu_w, bu_r, st_w, st_r, rows_f, rows_b, tag):
        for k in range(n_seg):
            @pl.when(s > -(1 + tag * n_seg + k))
            def _():
                io = ((uf_ref, rows_f, yf_ref), (ub_ref, rows_b, yb_ref))
                sub = seg_steps // seg_pairs
                for m in range(k * seg_pairs, (k + 1) * seg_pairs):
                    d, q = divmod(m, S5_QUARTERS)
                    _s5_project_in(io[d][0], io[d][1], bm_ref, d, q, bu_w)
                    _s5_scan(bu_r, st_w, a_ref, carry_ref, range(m * sub, (m + 1) * sub))
                    _s5_project_out(st_r, d, q, cm_ref, io[d][2], io[d][1])

    half(bu0_ref, bu1_ref, st1_ref, st0_ref, lo, hi, 0)

    @pl.when(s > -(1 + 2 * n_seg))
    def _():
        if want_fin:
            fin_ref[...] = carry_ref[...]
        start = h0_ref[...] if has_h0 else jnp.zeros(carry_ref.shape, F32)
        if n_super > 1:
            start = jnp.where(s % n_super == 0, start, carry_ref[...])
        carry_ref[...] = start

    half(bu1_ref, bu0_ref, st0_ref, st1_ref, hi, lo, 1)


def _s5(u, wts, row0, rows, seq_len, h0=None, want_fin=False, y_prev=None, cast=None):
    batch = rows // seq_len
    n_super = seq_len // (2 * S5_TC)
    n_steps = (batch // 2) * n_super
    assert not (want_fin and n_super > 1) and row0 % (2 * seq_len) == 0
    pair0 = row0 // (2 * seq_len)
    u3 = u.reshape(u.shape[0] // seq_len, seq_len, S5_WIDTH)
    nrow = 2 * S5_QUARTERS
    blk = (2, 2 * S5_TC, S5_WIDTH)

    def chunk_spec(delay, reverse):
        def index(s):
            s = jnp.clip(s - delay, 0, n_steps - 1)
            c = s % n_super
            return (pair0 + s // n_super, n_super - 1 - c if reverse else c, 0)
        return pl.BlockSpec(blk, index)

    def state_spec(delay):
        return pl.BlockSpec((None, 2, 2, nrow, S5_QLANES),
                            lambda s: (jnp.clip(s - delay, 0, n_steps - 1) // n_super, 0, 0, 0, 0))

    in_specs, args = [chunk_spec(0, False)], [u3]
    if n_super > 1:
        in_specs.append(chunk_spec(0, True))
        args.append(u3)
    in_specs += [_const_spec((2, S5_QUARTERS, LANES, 2 * S5_QLANES), (0, 0, 0, 0)),
                 _const_spec((2, S5_QUARTERS, 2 * S5_QLANES, LANES), (0, 0, 0, 0)),
                 _const_spec((2, 2, nrow, S5_QLANES), (0, 0, 0, 0))]
    args += [wts["s5_b"], wts["s5_c"], wts["s5_a"]]
    if h0 is not None:
        in_specs.append(state_spec(0))
        args.append(h0)
    aliases = {}
    if y_prev is not None:
        for n, y in enumerate(y_prev):
            aliases[len(args)] = n
            in_specs.append(pl.BlockSpec(memory_space=pl.ANY))
            args.append(y.reshape(u3.shape))
    y_shape = jax.ShapeDtypeStruct(u3.shape, F32)
    out_shape, out_specs = [y_shape, y_shape], [chunk_spec(1, False), chunk_spec(1, True)]
    if want_fin:
        out_shape.append(jax.ShapeDtypeStruct((batch // 2, 2, 2, nrow, S5_QLANES), F32))
        out_specs.append(state_spec(1))
    if cast is not None:
        weights, layer, sub = cast
        c_in, c_out, c_shape = _cast_job(weights, layer, sub, n_steps, lambda s: s)
        in_specs, args = in_specs + c_in, args + list(weights)
        out_specs, out_shape = out_specs + c_out, out_shape + c_shape
    work = pltpu.VMEM((2, 2 * S5_SLABS, nrow * S5_PITCH, LANES), F32)
    return pl.pallas_call(
        functools.partial(_s5_body, n_super, h0 is not None, want_fin, len(aliases),
                          0 if cast is None else N_FFN_MATS),
        out_shape=out_shape, grid=(n_steps + 1,), in_specs=in_specs, out_specs=out_specs,
        scratch_shapes=[work, work, work, work, pltpu.VMEM((2, 2, nrow, S5_QLANES), F32)],
        input_output_aliases=aliases,
        compiler_params=_params(1), name="s5",
    )(*args)


def _late_weights(hbm_refs, vmem_refs, sem):
    first = pl.program_id(0) == 0
    copies = [pltpu.make_async_copy(h, v, sem.at[n]) for n, (h, v) in enumerate(zip(hbm_refs, vmem_refs))]

    @pl.when(first)
    def _():
        for c in copies:
            c.start()

    def ready():
        @pl.when(first)
        def _():
            for c in copies:
                c.wait()

    return ready


def _mixout_body(n_prompt, x_ref, attn_p, attn_l, u_ref, yf_ref, yb_ref, mods0_ref, mods1_ref, g0_ref, g1_ref,
                 dsk_ref, wglu_ref, bglu_ref, wout_ref, wg0_ref, wu0_ref, wd0_ref, wg1_hbm, wu1_hbm, wd1_hbm, o_ref,
                 wg1_ref, wu1_ref, wd1_ref, sem):
    ready = _late_weights((wg1_hbm, wu1_hbm, wd1_hbm), (wg1_ref, wu1_ref, wd1_ref), sem)
    attn = jnp.where(pl.program_id(0) < n_prompt, attn_p[...], attn_l[...])
    mods = mods0_ref[...]
    g3 = g0_ref[...]
    y = dsk_ref[...] * u_ref[...] + yf_ref[...] + yb_ref[...]
    z = _gelu_tanh(y)
    s5o = z * _sigmoid(_dot(z.astype(BF16), wglu_ref[...]) + bglu_ref[...])
    half = HEADS * V_DIM
    mix = _dot(attn, wout_ref[0:half, :]) + _dot(s5o.astype(BF16), wout_ref[half:, :])
    x2 = x_ref[...] + mods[5:6] * mix
    x3 = _ffn(x2, mods[6:7], mods[7:8], mods[8:9], g3[2:3], wg0_ref, wu0_ref, wd0_ref)
    ready()
    mods = mods1_ref[...]
    o_ref[...] = _ffn(x3, mods[0:1], mods[1:2], mods[2:3], g1_ref[...][0:1], wg1_ref, wu1_ref, wd1_ref)


def _mixout(x, attn, u, yf, yb, mods, norm_g, wts, groups):
    in_specs = [_row_spec(D_MODEL), groups.spec(HEADS * V_DIM, 0), groups.spec(HEADS * V_DIM, 1)]
    in_specs += [_row_spec(S5_WIDTH)] * 3
    args = [x, *attn] + [a.reshape(groups.rows, S5_WIDTH) for a in (u, yf, yb)]
    in_specs += [_mods_spec(0, groups.cond), _mods_spec(1, groups.cond), _normg_spec(0), _normg_spec(1),
                 _const_spec((1, S5_WIDTH), (0, 0)), _const_spec((S5_WIDTH, S5_WIDTH), (0, 0)),
                 _const_spec((1, S5_WIDTH), (0, 0)), _const_spec((D_MODEL, D_MODEL), (0, 0))]
    late = wts["ffn"][1, 0]
    in_specs += _ffn_specs(wts["ffn"][0, 1], 0, 1) + [pl.BlockSpec(memory_space=pl.ANY)] * len(late)
    args += [mods, mods, norm_g, norm_g, wts["s5_d"], wts["w_glu"], wts["b_glu"], wts["w_out"],
             *wts["ffn"][0, 1], *late]
    return pl.pallas_call(
        functools.partial(_mixout_body, groups.prompt_tiles),
        out_shape=jax.ShapeDtypeStruct((groups.rows, D_MODEL), F32),
        grid=(groups.tiles,), in_specs=in_specs, out_specs=_row_spec(D_MODEL),
        scratch_shapes=[pltpu.VMEM(w.shape, w.dtype) for w in late] + [pltpu.SemaphoreType.DMA((len(late),))],
        compiler_params=_params(1), name="mixout_ffn",
    )(*args)


def _conv_body(n_prompt, seqs, x_ref, xp_ref, xn_ref, mods_ref, g_ref, cwin_ref, cw_ref, cwout_ref,
               wg_hbm, wu_hbm, wd_hbm, op_ref, ol_ref, wg_ref, wu_ref, wd_ref, sem):
    ready = _late_weights((wg_hbm, wu_hbm, wd_hbm), (wg_ref, wu_ref, wd_ref), sem)
    i = pl.program_id(0)
    is_prompt = i < n_prompt
    mods = mods_ref[...]
    g3 = g_ref[...]
    x = x_ref[...]
    n_ext = TM + 2 * SUBLANES
    xe = jnp.concatenate([xp_ref[...], x, xn_ref[...]], axis=0)
    hne = _modulate(xe, g3[1:2], mods[3:4], mods[4:5]).astype(BF16)
    pz = _dot(hne, cwin_ref[:, D_MODEL:3 * D_MODEL])
    z = pz[:, 0:D_MODEL] * pz[:, D_MODEL:2 * D_MODEL]
    gate_b = _dot(hne[SUBLANES:SUBLANES + TM], cwin_ref[:, 0:D_MODEL])
    main = slice(SUBLANES, SUBLANES + TM)
    row = i * TM + lax.broadcasted_iota(jnp.int32, (TM, 1), 0)
    pos = jnp.where(is_prompt, row % seqs[0], row % seqs[1])
    end = jnp.where(is_prompt, seqs[0] - 1, seqs[1] - 1)
    z_prev = jnp.where(pos == 0, 0.0, pltpu.roll(z, 1, 0)[main])
    z_next = jnp.where(pos == end, 0.0, pltpu.roll(z, n_ext - 1, 0)[main])
    cw = cw_ref[...]
    zc = z_prev * cw[0:1] + z[main] * cw[1:2] + z_next * cw[2:3]
    mix = _dot((gate_b * zc).astype(BF16), cwout_ref[...])
    x2 = x + mods[5:6] * mix
    ready()
    out = _ffn(x2, mods[6:7], mods[7:8], mods[8:9], g3[2:3], wg_ref, wu_ref, wd_ref)

    @pl.when(is_prompt)
    def _():
        op_ref[...] = out

    @pl.when(jnp.logical_not(is_prompt))
    def _():
        ol_ref[...] = out


def _conv_mixer(x, mods, norm_g, wts, groups):
    per = TM // SUBLANES
    last = groups.rows // SUBLANES - 1
    halo = (SUBLANES, D_MODEL)
    in_specs = [_row_spec(D_MODEL),
                pl.BlockSpec(halo, lambda i: (jnp.maximum(i * per - 1, 0), 0)),
                pl.BlockSpec(halo, lambda i: (jnp.minimum((i + 1) * per, last), 0)),
                _mods_spec(1, groups.cond), _normg_spec(1),
                _const_spec((D_MODEL, 3 * D_MODEL), (0, 0)), _const_spec((CONV_K, D_MODEL), (0, 0)),
                _const_spec((D_MODEL, D_MODEL), (0, 0))]
    late = wts["ffn"][1, 1]
    in_specs += [pl.BlockSpec(memory_space=pl.ANY)] * len(late)
    return pl.pallas_call(
        functools.partial(_conv_body, groups.prompt_tiles, groups.seq),
        out_shape=[jax.ShapeDtypeStruct((r, D_MODEL), F32) for r in groups.group_rows],
        grid=(groups.tiles,), in_specs=in_specs, out_specs=[groups.spec(D_MODEL, 0), groups.spec(D_MODEL, 1)],
        scratch_shapes=[pltpu.VMEM(w.shape, w.dtype) for w in late] + [pltpu.SemaphoreType.DMA((len(late),))],
        compiler_params=_params(1), name="conv_ffn",
    )(x, x, x, mods, norm_g, wts["conv_w_in"], wts["conv_w"], wts["conv_w_out"], *late)


ROPE_PARTNER = np.concatenate([np.arange(8, 16), np.arange(0, 8), np.arange(24, 32), np.arange(16, 24)])


def _rope_tables(seq_len, g_q, g_k):
    t = np.arange(seq_len)
    quarter = ROPE // 4
    inv_freq = ROPE_BASE ** (-np.arange(quarter, dtype=np.float64) / quarter)
    cos = np.ones((seq_len, LANES))
    sin = np.zeros((seq_len, LANES))
    for base, pos in ((NOPE, t // GRID_W), (NOPE + ROPE // 2, t % GRID_W)):
        ang = pos[:, None].astype(np.float64) * inv_freq[None, :]
        cos[:, base:base + quarter] = np.cos(ang)
        cos[:, base + quarter:base + 2 * quarter] = np.cos(ang)
        sin[:, base:base + quarter] = -np.sin(ang)
        sin[:, base + quarter:base + 2 * quarter] = np.sin(ang)
    cos, sin = jnp.asarray(cos, F32), jnp.asarray(sin, F32)
    out = []
    for g in (g_q, g_k):
        g = g.reshape(QK_DIM)
        g_pad = jnp.pad(g, (0, HEAD_PAD - QK_DIM))
        g_partner = jnp.pad(g[NOPE:][ROPE_PARTNER], (NOPE, HEAD_PAD - QK_DIM))
        out += [g_pad[None, :] * cos, g_partner[None, :] * sin]
    return tuple(out)


def _s5_params(lam_re, lam_im, log_dt, b_re, b_im, c_re, c_im):
    dt = jnp.exp(log_dt)[..., None]
    lr = jnp.minimum(lam_re, LAMBDA_RE_MAX)
    li = lam_im
    mag = jnp.exp(lr * dt)
    ang = li * dt
    ab_re = mag * jnp.cos(ang)
    ab_im = mag * jnp.sin(ang)
    den = lr * lr + li * li
    nr = ab_re - 1.0
    ni = ab_im
    co_re = (nr * lr + ni * li) / den
    co_im = (ni * lr - nr * li) / den
    bb_re = co_re[..., None] * b_re - co_im[..., None] * b_im
    bb_im = co_re[..., None] * b_im + co_im[..., None] * b_re
    per_q = S5_GROUPS // S5_QUARTERS
    eye = jnp.eye(per_q, dtype=F32)

    def in_mat(bb):
        bb = bb.reshape(2, S5_QUARTERS, per_q, S5_STATE, S5_GROUP)
        return jnp.einsum("dqgpi,gh->dqgihp", bb, eye).reshape(2, S5_QUARTERS, LANES, S5_QLANES)

    def out_mat(cc):
        cc = cc.reshape(2, S5_QUARTERS, per_q, S5_GROUP, S5_STATE)
        return jnp.einsum("dqgip,gh->dqgphi", cc, eye).reshape(2, S5_QUARTERS, S5_QLANES, LANES)

    bm = jnp.concatenate([in_mat(bb_re), in_mat(bb_im)], axis=-1).astype(BF16)
    cm = jnp.concatenate([out_mat(c_re), -out_mat(c_im)], axis=-2).astype(BF16)

    def rows(a):
        a = a.reshape(2, 1, S5_QUARTERS, S5_QLANES)
        return jnp.broadcast_to(a, (2, 2, S5_QUARTERS, S5_QLANES)).reshape(2, 2 * S5_QUARTERS, S5_QLANES)

    return bm, cm, jnp.stack([rows(ab_re), rows(ab_im)], axis=1)


def _state_rows(s):
    b = s.shape[0]
    s = s.reshape(b // 2, 2, 2, S5_QUARTERS, S5_QLANES)
    return s.transpose(0, 2, 1, 3, 4).reshape(b // 2, 2, 2 * S5_QUARTERS, S5_QLANES)


def _state_unrows(s, batch):
    s = s.reshape(batch // 2, 2, 2, S5_QUARTERS, S5_QLANES).transpose(0, 2, 1, 3, 4)
    return s.reshape(batch, 1, 2, S5_GROUPS, S5_STATE)


def kernel(x_prompt, x_sample, cache_ckv, cache_krope, state_ssm_re, state_ssm_im, c, c_ctx, w_ada, b_ada, norm_g, ffn_w_gate, ffn_w_up, ffn_w_down, ab_w_in, mla_g_q_lat, mla_g_kv_lat, mla_w_uq, mla_w_ukv, mla_g_qnorm, mla_g_knorm, s5_lam_re, s5_lam_im, s5_log_dt, s5_b_re, s5_b_im, s5_c_re, s5_c_im, s5_d, s5_w_glu, s5_b_glu, ab_w_out, conv_w_in, conv_w, conv_w_out):
    batch, seq, _ = x_prompt.shape
    dec_batch, dec_seq, _ = x_sample.shape
    past = cache_ckv.shape[2]

    w_in = ab_w_in[0]
    kr_cols = w_in[:, Q_LORA + KV_LORA:Q_LORA + KV_LORA + ROPE]
    zeros = jnp.zeros_like(kr_cols)
    w_uq = jnp.pad(mla_w_uq[0].reshape(Q_LORA, HEADS, QK_DIM), ((0, 0), (0, 0), (0, HEAD_PAD - QK_DIM)))
    w_ukv = mla_w_ukv[0].reshape(KV_LORA, HEADS, NOPE + V_DIM)
    w_uk = jnp.pad(w_ukv[:, :, :NOPE], ((0, 0), (0, 0), (0, HEAD_PAD - NOPE)))
    bm, cm, a_rows = _s5_params(s5_lam_re[0], s5_lam_im[0], s5_log_dt[0], s5_b_re[0], s5_b_im[0],
                                s5_c_re[0], s5_c_im[0])
    stacked = (ffn_w_gate, ffn_w_up, ffn_w_down)
    ffn = {(0, 0): tuple(w[0, 0].astype(BF16) for w in stacked)}
    w_in_cols = [w_in[:, :Q_LORA + KV_LORA], w_in[:, Q_LORA + KV_LORA + ROPE:], kr_cols, zeros, kr_cols, zeros]
    w_in_partner = [zeros, zeros, kr_cols[:, ROPE_PARTNER], zeros]
    w_uq3 = mla_w_uq[0].reshape(Q_LORA, HEADS, QK_DIM)
    w_uq_partner = jnp.pad(w_uq3[:, :, NOPE:][:, :, ROPE_PARTNER], ((0, 0), (0, 0), (NOPE, HEAD_PAD - QK_DIM)))
    wts = {
        "ffn": ffn,
        "w_in_rot": jnp.concatenate(w_in_cols + w_in_partner, axis=1).astype(BF16),
        "g_q_lat": mla_g_q_lat, "g_kv_lat": mla_g_kv_lat,
        "w_uq_rot": jnp.concatenate([w_uq, w_uq_partner], axis=1).reshape(Q_LORA, 2 * HEADS * HEAD_PAD).astype(BF16),
        "w_ukv": jnp.concatenate([w_uk.reshape(KV_LORA, HEADS * HEAD_PAD),
                                  w_ukv[:, :, NOPE:].reshape(KV_LORA, HEADS * V_DIM)], axis=1).astype(BF16),
        "g_qn": jnp.pad(mla_g_qnorm, ((0, 0), (0, HEAD_PAD - QK_DIM))),
        "g_kn": jnp.pad(mla_g_knorm, ((0, 0), (0, HEAD_PAD - QK_DIM))),
        "rope": _rope_tables(dec_seq, mla_g_qnorm, mla_g_knorm),
        "s5_b": bm, "s5_c": cm, "s5_a": a_rows,
        "s5_d": s5_d, "w_glu": s5_w_glu[0].astype(BF16), "b_glu": s5_b_glu, "w_out": ab_w_out[0].astype(BF16),
        "conv_w_in": conv_w_in[0].astype(BF16), "conv_w": conv_w[0].T, "conv_w_out": conv_w_out[0].astype(BF16),
    }

    cond = jnp.zeros((SUBLANES, D_MODEL), F32).at[0].set(c_ctx).at[1:1 + dec_batch].set(c)
    mods = _ada(cond, w_ada, b_ada).reshape(w_ada.shape[0], SUBLANES, N_MOD, D_MODEL)

    groups = _Groups(batch * seq, seq, dec_batch * dec_seq, dec_seq)
    rows_p, rows_l = groups.group_rows
    x1, q, k, v, u, ckv_p, kr_p = _mixin(
        x_prompt.reshape(rows_p, D_MODEL), x_sample.reshape(rows_l, D_MODEL), mods, norm_g, wts, groups)
    kr_pad = jnp.pad(cache_krope[:, 0].reshape(dec_batch * past, ROPE), ((0, 0), (NOPE, LANES - QK_DIM)))
    ctx = _ctx_kv(cache_ckv[:, 0].reshape(dec_batch * past, KV_LORA), kr_pad, wts)
    h0 = jnp.stack([_state_rows(state_ssm_re[:, 0]), _state_rows(state_ssm_im[:, 0])], axis=2)
    attn_p, yf, yb, *ffn[1, 1] = _attention(q, k, v, 0, rows_p, seq, seq, n_seq=2,
                                            zeros=[(groups.rows, S5_WIDTH)] * 2, cast=(stacked, 1, 1))
    attn_l, *ffn[0, 1] = _attention(q, k, v, rows_p, rows_l, dec_seq, 256, ctx, cast=(stacked, 0, 1))
    attn = (attn_p, attn_l)
    yf, yb, fin, *ffn[1, 0] = _s5(u, wts, 0, rows_p, seq, want_fin=True, y_prev=(yf, yb), cast=(stacked, 1, 0))
    yf, yb = _s5(u, wts, rows_p, rows_l, dec_seq, h0=h0, y_prev=(yf, yb))
    x3 = _mixout(x1, attn, u, yf, yb, mods, norm_g, wts, groups)
    y_p, y_s = _conv_mixer(x3, mods, norm_g, wts, groups)
    return (y_p.reshape(batch, seq, D_MODEL), y_s.reshape(dec_batch, dec_seq, D_MODEL),
            ckv_p.reshape(batch, 1, seq, KV_LORA), kr_p.reshape(batch, 1, seq, ROPE),
            _state_unrows(fin[:, :, 0], batch), _state_unrows(fin[:, :, 1], batch))
```

```python
import functools
import math

import jax
import jax.numpy as jnp
import numpy as np
from jax import lax
from jax.experimental import pallas as pl
from jax.experimental.pallas import tpu as pltpu

F32 = jnp.float32
BF16 = jnp.bfloat16

LANES = 128
SUBLANES = 8
VMEM_LIMIT_BYTES = 60 * 1024 * 1024

D_MODEL = 1024
D_FF = 2816
N_MOD = 9
EPS = 1e-6
HEADS = 8
Q_LORA = 384
KV_LORA = 256
NOPE = 64
ROPE = 32
V_DIM = 64
QK_DIM = NOPE + ROPE
HEAD_PAD = LANES
ROPE_BASE = 10000.0
GRID_W = 64
S5_WIDTH = 512
S5_GROUP = 16
S5_GROUPS = 32
S5_STATE = 64
S5_LANES = S5_GROUPS * S5_STATE
S5_QUARTERS = 4
S5_QLANES = S5_LANES // S5_QUARTERS
S5_SLABS = S5_QLANES // LANES
S5_TC = 128
S5_PITCH = S5_TC + 4
S5_SEGMENTS = 1
LAMBDA_RE_MAX = -1e-4
CONV_K = 3

ATT_ROWS = 2 * SUBLANES
TM = 512
FF_CHUNKS = tuple((c, min(512, D_FF - c)) for c in range(0, D_FF, 512))

O_CKV = Q_LORA
O_U = Q_LORA + KV_LORA
O_KR = O_U + S5_WIDTH


def _dot(a, b):
    return jnp.dot(a, b, preferred_element_type=F32)


def _sigmoid(x):
    return 1.0 / (1.0 + jnp.exp(-x))


def _rms(x, g, n):
    ms = jnp.sum(x * x, axis=-1, keepdims=True) * (1.0 / n)
    return x * lax.rsqrt(ms + EPS) * g


def _modulate(x, g, shift, scale):
    return _rms(x, g, D_MODEL) * (1.0 + scale) + shift


def _ffn(x, shift, scale, gate, g, wg_ref, wu_ref, wd_ref):
    h = _modulate(x, g, shift, scale).astype(BF16)
    acc = jnp.zeros(x.shape, F32)
    for c0, cs in FF_CHUNKS:
        gt = _dot(h, wg_ref[:, c0:c0 + cs])
        up = _dot(h, wu_ref[:, c0:c0 + cs])
        act = ((gt * _sigmoid(gt)) * up).astype(BF16)
        acc = acc + _dot(act, wd_ref[c0:c0 + cs, :])
    return x + (0.5 * gate) * acc


def _head_inv_rms(xh):
    return lax.rsqrt(jnp.sum(xh * xh, axis=-1, keepdims=True) * (1.0 / QK_DIM) + EPS)


def _head_norm(xh, g):
    return xh * _head_inv_rms(xh) * g


def _gelu_tanh(x):
    return x * (0.5 * (1.0 + jnp.tanh(math.sqrt(2.0 / math.pi) * (x + 0.044715 * (x * x * x)))))


def _const_spec(shape, index):
    return pl.BlockSpec(shape, lambda *_: index, pipeline_mode=pl.Buffered(1))


def _params(n_axes):
    return pltpu.CompilerParams(dimension_semantics=("arbitrary",) * n_axes, vmem_limit_bytes=VMEM_LIMIT_BYTES)


def _ffn_specs(weights, layer, sub):
    return [_const_spec(w.shape, (0, 0)) if w.ndim == 2 else
            _const_spec((None, None) + w.shape[2:], (layer, sub, 0, 0)) for w in weights]


def _mods_spec(layer, cond_of_tile):
    return pl.BlockSpec((None, None, N_MOD, D_MODEL), lambda i: (layer, cond_of_tile(i), 0, 0))


def _normg_spec(layer):
    return _const_spec((None, 3, D_MODEL), (layer, 0, 0))


def _row_spec(width, rows=TM):
    return pl.BlockSpec((rows, width), lambda i: (i, 0))


def _cast_job(items, n_steps, step_of):
    in_specs, out_specs, out_shape = [], [], []
    for w, lead in items:
        r, c = w.shape[len(lead):]
        assert r % (n_steps * 2 * SUBLANES) == 0
        slab = lambda *g: jnp.minimum(step_of(*g), n_steps - 1)
        in_specs.append(pl.BlockSpec((None,) * len(lead) + (r // n_steps, c),
                                     lambda *g, lead=lead, slab=slab: (*lead, slab(*g), 0)))
        out_specs.append(pl.BlockSpec((r // n_steps, c), lambda *g, slab=slab: (slab(*g), 0)))
        out_shape.append(jax.ShapeDtypeStruct((r, c), BF16))
    return in_specs, out_specs, out_shape


def _cast_slabs(src_refs, dst_refs):
    for src, dst in zip(src_refs, dst_refs):
        dst[...] = src[...].astype(BF16)


class _Groups:
    def __init__(self, prompt_rows, prompt_seq, latent_rows, latent_seq):
        assert prompt_rows % TM == 0 and latent_seq % TM == 0 and TM % prompt_seq == 0
        assert prompt_rows % latent_seq == 0
        self.group_rows = (prompt_rows, latent_rows)
        self.seq = (prompt_seq, latent_seq)
        self.rows = prompt_rows + latent_rows
        self.prompt_tiles = prompt_rows // TM
        self.tiles = self.rows // TM
        self.latent_tiles_per_seq = latent_seq // TM

    def cond(self, i):
        return jnp.where(i < self.prompt_tiles, 0,
                         1 + jnp.maximum(i - self.prompt_tiles, 0) // self.latent_tiles_per_seq)

    def spec(self, width, group):
        n = self.prompt_tiles
        if group == 0:
            return pl.BlockSpec((TM, width), lambda i: (jnp.minimum(i, n - 1), 0))
        return pl.BlockSpec((TM, width), lambda i: (jnp.maximum(i - n, 0), 0))


def _ada_body(c_ref, w_ref, b_ref, o_ref):
    c = c_ref[...]
    s = (c * _sigmoid(c)).astype(BF16)
    o_ref[...] = _dot(s, w_ref[...].astype(BF16)) + b_ref[...]


def _ada(cond, w_ada, b_ada):
    depth, _, n = w_ada.shape
    tn = 2304
    return pl.pallas_call(
        _ada_body,
        out_shape=jax.ShapeDtypeStruct((depth, SUBLANES, n), F32),
        grid=(depth, n // tn),
        in_specs=[pl.BlockSpec((SUBLANES, D_MODEL), lambda l, j: (0, 0)),
                  pl.BlockSpec((None, D_MODEL, tn), lambda l, j: (l, 0, j)),
                  pl.BlockSpec((None, 1, tn), lambda l, j: (l, 0, j))],
        out_specs=pl.BlockSpec((None, SUBLANES, tn), lambda l, j: (l, 0, j)),
        compiler_params=_params(2),
        name="ada",
    )(cond, w_ada, b_ada.reshape(depth, 1, n))


def _mixin_body(n_prompt, *refs):
    (xp_ref, xs_ref, mods_ref, g_ref, wg_ref, wu_ref, wd_ref, win_ref, gql_ref, gkv_ref, wuq_ref, wukv_ref,
     gq_ref, gk_ref, rqc_ref, rqs_ref, rkc_ref, rks_ref,
     x1_ref, q_ref, k_ref, v_ref, u_ref, ckv_ref, kr_ref) = refs
    is_prompt = pl.program_id(0) < n_prompt
    mods = mods_ref[...]
    g3 = g_ref[...]
    x = jnp.where(is_prompt, xp_ref[...], xs_ref[...])
    x1 = _ffn(x, mods[0:1], mods[1:2], mods[2:3], g3[0:1], wg_ref, wu_ref, wd_ref)
    x1_ref[...] = x1
    hn = _modulate(x1, g3[1:2], mods[3:4], mods[4:5]).astype(BF16)
    proj = _dot(hn, win_ref[:, 0:O_KR + LANES])
    u_ref[...] = proj[:, O_U:O_KR]
    ckv = _rms(proj[:, O_CKV:O_U], gkv_ref[...], KV_LORA)
    krg = proj[:, O_KR:O_KR + LANES]
    qn = _rms(proj[:, 0:Q_LORA], gql_ref[...], Q_LORA).astype(BF16)
    kv = _dot(ckv.astype(BF16), wukv_ref[...])
    v_ref[...] = kv[:, HEADS * HEAD_PAD:].astype(BF16)
    lane = lax.broadcasted_iota(jnp.int32, (1, LANES), 1)
    kr_only = jnp.where((lane >= NOPE) & (lane < QK_DIM), krg, 0.0)
    gk = gk_ref[...]
    head = lambda h: slice(h * HEAD_PAD, (h + 1) * HEAD_PAD)

    @pl.when(is_prompt)
    def _():
        ckv_ref[...] = ckv
        kr_ref[...] = krg[:, 0:ROPE]
        qraw = _dot(qn, wuq_ref[:, 0:HEADS * HEAD_PAD])
        gq = gq_ref[...]
        for h in range(HEADS):
            q_ref[:, head(h)] = _head_norm(qraw[:, head(h)], gq).astype(BF16)
            k_ref[:, head(h)] = _head_norm(kv[:, head(h)] + kr_only, gk).astype(BF16)

    @pl.when(jnp.logical_not(is_prompt))
    def _():
        qraw = _dot(qn, wuq_ref[...])
        kr_partner = _dot(hn, win_ref[:, O_KR + LANES:O_KR + 2 * LANES])
        k_rot = kr_only * rkc_ref[...] + kr_partner * rks_ref[...]
        for h in range(HEADS):
            qa = qraw[:, head(h)]
            qh = (qa * rqc_ref[...] + qraw[:, head(HEADS + h)] * rqs_ref[...]) * _head_inv_rms(qa)
            kh = (kv[:, head(h)] * gk + k_rot) * _head_inv_rms(kv[:, head(h)] + kr_only)
            q_ref[:, head(h)] = qh.astype(BF16)
            k_ref[:, head(h)] = kh.astype(BF16)


def _mixin(x_prompt, x_latent, mods, norm_g, wts, groups):
    n_prompt = groups.prompt_tiles
    w_in, w_uq = wts["w_in_rot"], wts["w_uq_rot"]
    in_specs = [groups.spec(D_MODEL, 0), groups.spec(D_MODEL, 1), _mods_spec(0, groups.cond), _normg_spec(0)]
    in_specs += _ffn_specs(wts["ffn"][0, 0], 0, 0) + [
        _const_spec(w_in.shape, (0, 0)),
        _const_spec((1, Q_LORA), (0, 0)),
        _const_spec((1, KV_LORA), (0, 0)),
        _const_spec(w_uq.shape, (0, 0)),
        _const_spec((KV_LORA, HEADS * HEAD_PAD + HEADS * V_DIM), (0, 0)),
        _const_spec((1, HEAD_PAD), (0, 0)),
        _const_spec((1, HEAD_PAD), (0, 0)),
    ] + [pl.BlockSpec((TM, LANES), lambda i: (jnp.maximum(i - n_prompt, 0) % groups.latent_tiles_per_seq, 0))] * 4
    args = [x_prompt, x_latent, mods, norm_g, *wts["ffn"][0, 0], w_in, wts["g_q_lat"],
            wts["g_kv_lat"], w_uq, wts["w_ukv"], wts["g_qn"], wts["g_kn"]] + list(wts["rope"])
    out_shape, out_specs = [], []
    for w, dt in ((D_MODEL, F32), (HEADS * HEAD_PAD, BF16), (HEADS * HEAD_PAD, BF16), (HEADS * V_DIM, BF16),
                  (S5_WIDTH, F32)):
        out_shape.append(jax.ShapeDtypeStruct((groups.rows, w), dt))
        out_specs.append(_row_spec(w))
    for w in (KV_LORA, ROPE):
        out_shape.append(jax.ShapeDtypeStruct((groups.group_rows[0], w), F32))
        out_specs.append(groups.spec(w, 0))
    return pl.pallas_call(
        functools.partial(_mixin_body, n_prompt),
        out_shape=out_shape, grid=(groups.tiles,), in_specs=in_specs, out_specs=out_specs,
        compiler_params=_params(1), name="ffn_mixin",
    )(*args)


def _ctx_body(ckv_ref, krp_ref, wukv_ref, gk_ref, k_ref, v_ref):
    kv = _dot(ckv_ref[...].astype(BF16), wukv_ref[...])
    v_ref[...] = kv[:, HEADS * HEAD_PAD:].astype(BF16)
    gk = gk_ref[...]
    krp = krp_ref[...]
    for h in range(HEADS):
        sl = slice(h * HEAD_PAD, (h + 1) * HEAD_PAD)
        k_ref[:, sl] = _head_norm(kv[:, sl] + krp, gk).astype(BF16)


def _ctx_kv(ckv, kr_padded, wts):
    rows = ckv.shape[0]
    return pl.pallas_call(
        _ctx_body,
        out_shape=[jax.ShapeDtypeStruct((rows, HEADS * HEAD_PAD), BF16),
                   jax.ShapeDtypeStruct((rows, HEADS * V_DIM), BF16)],
        grid=(rows // TM,),
        in_specs=[_row_spec(KV_LORA), _row_spec(LANES),
                  _const_spec((KV_LORA, HEADS * HEAD_PAD + HEADS * V_DIM), (0, 0)),
                  _const_spec((1, HEAD_PAD), (0, 0))],
        out_specs=[_row_spec(HEADS * HEAD_PAD), _row_spec(HEADS * V_DIM)],
        compiler_params=_params(1), name="ctx_kv",
    )(ckv, kr_padded, wts["w_ukv"], wts["g_kn"])


def _attn_body(n_kv, n_seq, n_zero, n_cast, *refs):
    refs = list(refs)
    q_ref = refs.pop(0)
    k_refs, v_refs, cast_in = refs[:n_kv], refs[n_kv:2 * n_kv], refs[2 * n_kv:2 * n_kv + n_cast]
    refs = refs[2 * n_kv + n_cast:]
    o_ref = refs.pop(0)
    zero_refs, cast_out = refs[:n_zero], refs[n_zero:n_zero + n_cast]
    s_ref, p_ref, r_ref = refs[n_zero + n_cast:]
    _cast_slabs(cast_in, cast_out)
    for z_ref in zero_refs:
        z_ref[...] = jnp.zeros(z_ref.shape, z_ref.dtype)
    tq = q_ref.shape[0] // n_seq
    c = (QK_DIM ** -0.5) * math.log2(math.e)
    lane = lax.broadcasted_iota(jnp.int32, (1, LANES), 1)
    n_keys = [k_ref.shape[0] // n_seq for k_ref in k_refs]
    spans = [slice(sum(n_keys[:i]), sum(n_keys[:i + 1])) for i in range(n_kv)]
    for sq, pair in [(sq, pair) for sq in range(n_seq) for pair in range(HEADS // 2)]:
        qrows = slice(sq * tq, (sq + 1) * tq)
        krows = [slice(sq * n, (sq + 1) * n) for n in n_keys]
        wide = slice(pair // 2 * 2 * LANES, (pair // 2 + 1) * 2 * LANES)
        mine = slice(pair % 2 * LANES, (pair % 2 + 1) * LANES)
        vsl = slice(pair * LANES, (pair + 1) * LANES)
        outs = []
        for h in (2 * pair, 2 * pair + 1):
            slot = h % 2
            sl = slice(h * HEAD_PAD, (h + 1) * HEAD_PAD)
            q = q_ref[qrows, sl]
            for k_ref, kr, span in zip(k_refs, krows, spans):
                s_ref[slot, :, span] = lax.dot_general(q, k_ref[kr, sl], (((1,), (1,)), ((), ())),
                                                       preferred_element_type=F32)
            for r0 in range(0, tq, ATT_ROWS):
                rows = slice(r0, r0 + ATT_ROWS)
                s = s_ref[slot, rows, :]
                e = jnp.exp2((s - jnp.max(s, axis=-1, keepdims=True)) * c)
                r_ref[slot, rows, :] = jnp.broadcast_to(1.0 / jnp.sum(e, axis=-1, keepdims=True), (ATT_ROWS, LANES))
                p_ref[slot, rows, :] = e.astype(BF16)
            o = functools.reduce(jnp.add, [_dot(p_ref[slot, :, span], v_ref[kr, wide])
                                           for v_ref, kr, span in zip(v_refs, krows, spans)])
            outs.append(o[:, mine] * r_ref[slot])
        o_ref[qrows, vsl] = jnp.where(lane < V_DIM, outs[0], outs[1]).astype(BF16)


def _attention(q, k, v, row0, rows, seq_len, tq, ctx=None, n_seq=1, zeros=(), cast=()):
    q_tiles = seq_len // tq
    n_seq = n_seq if q_tiles == 1 and ctx is None else 1
    q0, kv0 = row0 // (n_seq * tq), row0 // (n_seq * seq_len)
    kv_spec = lambda w: pl.BlockSpec((n_seq * seq_len, w), lambda b, t: (kv0 + b, 0))
    in_specs = [pl.BlockSpec((n_seq * tq, HEADS * HEAD_PAD), lambda b, t: (q0 + b * q_tiles + t, 0))]
    ks, vs = [kv_spec(HEADS * HEAD_PAD)], [kv_spec(HEADS * V_DIM)]
    kargs, vargs = [k], [v]
    n_keys = seq_len
    if ctx is not None:
        k_c, v_c = ctx
        past = k_c.shape[0] // (rows // seq_len)
        ks.insert(0, pl.BlockSpec((past, HEADS * HEAD_PAD), lambda b, t: (b, 0)))
        vs.insert(0, pl.BlockSpec((past, HEADS * V_DIM), lambda b, t: (b, 0)))
        kargs.insert(0, k_c)
        vargs.insert(0, v_c)
        n_keys += past
    grid = (rows // (n_seq * seq_len), q_tiles)
    out_shape = [jax.ShapeDtypeStruct((rows, HEADS * V_DIM), BF16)]
    out_specs = [pl.BlockSpec((n_seq * tq, HEADS * V_DIM), lambda b, t: (b * q_tiles + t, 0))]
    for shape in zeros:
        assert q_tiles == 1 and shape[0] % (SUBLANES * grid[0]) == 0
        out_shape.append(jax.ShapeDtypeStruct(shape, F32))
        out_specs.append(pl.BlockSpec((shape[0] // grid[0], shape[1]), lambda b, t: (b, 0)))
    in_specs, args = in_specs + ks + vs, [q, *kargs, *vargs]
    c_in, c_out, c_shape = _cast_job(cast, grid[0] * grid[1], lambda b, t: b * q_tiles + t)
    in_specs, args = in_specs + c_in, args + [w for w, _ in cast]
    out_specs, out_shape = out_specs + c_out, out_shape + c_shape
    return pl.pallas_call(
        functools.partial(_attn_body, len(kargs), n_seq, len(zeros), len(cast)),
        out_shape=out_shape, grid=grid, in_specs=in_specs, out_specs=out_specs,
        scratch_shapes=[pltpu.VMEM((2, tq, n_keys), F32), pltpu.VMEM((2, tq, n_keys), BF16),
                        pltpu.VMEM((2, tq, LANES), F32)],
        compiler_params=_params(2), name="attention",
    )(*args)


def _s5_project_in(u_ref, rows, bm_ref, d, q, bu_ref):
    qsl = slice(q * LANES, (q + 1) * LANES)
    lhs = jnp.concatenate([u_ref[0, rows, qsl], u_ref[1, rows, qsl]], axis=0).astype(BF16)
    bu = _dot(lhs, bm_ref[d, q])
    for b in range(2):
        r = b * S5_QUARTERS + q
        for j in range(2 * S5_SLABS):
            bu_ref[d, j, pl.ds(r * S5_PITCH, S5_TC), :] = bu[b * S5_TC:(b + 1) * S5_TC, j * LANES:(j + 1) * LANES]


def _s5_scan(bu_ref, st_ref, a_ref, carry_ref, steps):
    nrow = 2 * S5_QUARTERS
    order = [(d, j) for d in range(2) for j in range(S5_SLABS)]
    lanes = lambda j: slice(j * LANES, (j + 1) * LANES)
    carry = {(d, j): (carry_ref[d, 0, :, lanes(j)], carry_ref[d, 1, :, lanes(j)]) for d, j in order}
    for i in steps:
        for d, j in order:
            t = i if d == 0 else S5_TC - 1 - i
            rows = pl.ds(t, nrow, stride=S5_PITCH)
            sr, si = carry[d, j]
            ar, ai = a_ref[d, 0, :, lanes(j)], a_ref[d, 1, :, lanes(j)]
            nr = ar * sr - ai * si + bu_ref[d, j, rows, :]
            ni = ar * si + ai * sr + bu_ref[d, S5_SLABS + j, rows, :]
            st_ref[d, j, rows, :] = nr
            st_ref[d, S5_SLABS + j, rows, :] = ni
            carry[d, j] = (nr, ni)
    for d, j in order:
        carry_ref[d, 0, :, lanes(j)], carry_ref[d, 1, :, lanes(j)] = carry[d, j]


def _s5_project_out(st_ref, d, q, cm_ref, y_ref, rows):
    qsl = slice(q * LANES, (q + 1) * LANES)
    lhs = jnp.concatenate(
        [jnp.concatenate([st_ref[d, j, pl.ds((b * S5_QUARTERS + q) * S5_PITCH, S5_TC), :]
                          for j in range(2 * S5_SLABS)], axis=-1) for b in range(2)], axis=0).astype(BF16)
    y = _dot(lhs, cm_ref[d, q])
    y_ref[0, rows, qsl] = y[0:S5_TC]
    y_ref[1, rows, qsl] = y[S5_TC:2 * S5_TC]


def _s5_body(n_super, has_h0, want_fin, n_aliased, n_cast, *refs):
    refs = list(refs)
    uf_ref = refs.pop(0)
    ub_ref = refs.pop(0) if n_super > 1 else uf_ref
    bm_ref, cm_ref, a_ref = refs[:3]
    refs = refs[3:]
    h0_ref = refs.pop(0) if has_h0 else None
    refs = refs[n_aliased:]
    cast_in, refs = refs[:n_cast], refs[n_cast:]
    yf_ref, yb_ref = refs[:2]
    refs = refs[2:]
    fin_ref = refs.pop(0) if want_fin else None
    cast_out, refs = refs[:n_cast], refs[n_cast:]
    bu0_ref, bu1_ref, st0_ref, st1_ref, carry_ref = refs
    _cast_slabs(cast_in, cast_out)
    s = pl.program_id(0)
    lo, hi = slice(0, S5_TC), slice(S5_TC, 2 * S5_TC)
    n_seg = S5_SEGMENTS
    seg_steps = S5_TC // n_seg
    seg_pairs = 2 * S5_QUARTERS // n_seg

    @pl.when(s == 0)
    def _():
        bu1_ref[...] = jnp.zeros(bu1_ref.shape, F32)
        st0_ref[...] = jnp.zeros(st0_ref.shape, F32)
        carry_ref[...] = jnp.zeros(carry_ref.shape, F32)

    def half(bu_w, bu_r, st_w, st_r, rows_f, rows_b, tag):
        for k in range(n_seg):
            @pl.when(s > -(1 + tag * n_seg + k))
            def _():
                io = ((uf_ref, rows_f, yf_ref), (ub_ref, rows_b, yb_ref))
                sub = seg_steps // seg_pairs
                for m in range(k * seg_pairs, (k + 1) * seg_pairs):
                    d, q = divmod(m, S5_QUARTERS)
                    _s5_project_in(io[d][0], io[d][1], bm_ref, d, q, bu_w)
                    _s5_scan(bu_r, st_w, a_ref, carry_ref, range(m * sub, (m + 1) * sub))
                    _s5_project_out(st_r, d, q, cm_ref, io[d][2], io[d][1])

    half(bu0_ref, bu1_ref, st1_ref, st0_ref, lo, hi, 0)

    @pl.when(s > -(1 + 2 * n_seg))
    def _():
        if want_fin:
            fin_ref[...] = carry_ref[...]
        start = h0_ref[...] if has_h0 else jnp.zeros(carry_ref.shape, F32)
        if n_super > 1:
            start = jnp.where(s % n_super == 0, start, carry_ref[...])
        carry_ref[...] = start

    half(bu1_ref, bu0_ref, st0_ref, st1_ref, hi, lo, 1)


def _s5(u, wts, row0, rows, seq_len, h0=None, want_fin=False, y_prev=None, cast=()):
    batch = rows // seq_len
    n_super = seq_len // (2 * S5_TC)
    n_steps = (batch // 2) * n_super
    assert not (want_fin and n_super > 1) and row0 % (2 * seq_len) == 0
    pair0 = row0 // (2 * seq_len)
    u3 = u.reshape(u.shape[0] // seq_len, seq_len, S5_WIDTH)
    nrow = 2 * S5_QUARTERS
    blk = (2, 2 * S5_TC, S5_WIDTH)

    def chunk_spec(delay, reverse):
        def index(s):
            s = jnp.clip(s - delay, 0, n_steps - 1)
            c = s % n_super
            return (pair0 + s // n_super, n_super - 1 - c if reverse else c, 0)
        return pl.BlockSpec(blk, index)

    def state_spec(delay):
        return pl.BlockSpec((None, 2, 2, nrow, S5_QLANES),
                            lambda s: (jnp.clip(s - delay, 0, n_steps - 1) // n_super, 0, 0, 0, 0))

    in_specs, args = [chunk_spec(0, False)], [u3]
    if n_super > 1:
        in_specs.append(chunk_spec(0, True))
        args.append(u3)
    in_specs += [_const_spec((2, S5_QUARTERS, LANES, 2 * S5_QLANES), (0, 0, 0, 0)),
                 _const_spec((2, S5_QUARTERS, 2 * S5_QLANES, LANES), (0, 0, 0, 0)),
                 _const_spec((2, 2, nrow, S5_QLANES), (0, 0, 0, 0))]
    args += [wts["s5_b"], wts["s5_c"], wts["s5_a"]]
    if h0 is not None:
        in_specs.append(state_spec(0))
        args.append(h0)
    aliases = {}
    if y_prev is not None:
        for n, y in enumerate(y_prev):
            aliases[len(args)] = n
            in_specs.append(pl.BlockSpec(memory_space=pl.ANY))
            args.append(y.reshape(u3.shape))
    y_shape = jax.ShapeDtypeStruct(u3.shape, F32)
    out_shape, out_specs = [y_shape, y_shape], [chunk_spec(1, False), chunk_spec(1, True)]
    if want_fin:
        out_shape.append(jax.ShapeDtypeStruct((batch // 2, 2, 2, nrow, S5_QLANES), F32))
        out_specs.append(state_spec(1))
    c_in, c_out, c_shape = _cast_job(cast, n_steps, lambda s: s)
    in_specs, args = in_specs + c_in, args + [w for w, _ in cast]
    out_specs, out_shape = out_specs + c_out, out_shape + c_shape
    work = pltpu.VMEM((2, 2 * S5_SLABS, nrow * S5_PITCH, LANES), F32)
    return pl.pallas_call(
        functools.partial(_s5_body, n_super, h0 is not None, want_fin, len(aliases), len(cast)),
        out_shape=out_shape, grid=(n_steps + 1,), in_specs=in_specs, out_specs=out_specs,
        scratch_shapes=[work, work, work, work, pltpu.VMEM((2, 2, nrow, S5_QLANES), F32)],
        input_output_aliases=aliases,
        compiler_params=_params(1), name="s5",
    )(*args)


def _late_weights(hbm_refs, vmem_refs, sem):
    first = pl.program_id(0) == 0
    copies = [pltpu.make_async_copy(h, v, sem.at[n]) for n, (h, v) in enumerate(zip(hbm_refs, vmem_refs))]

    @pl.when(first)
    def _():
        for c in copies:
            c.start()

    def ready():
        @pl.when(first)
        def _():
            for c in copies:
                c.wait()

    return ready


def _mixout_body(n_prompt, x_ref, attn_p, attn_l, u_ref, yf_ref, yb_ref, mods0_ref, mods1_ref, g0_ref, g1_ref,
                 dsk_ref, wglu_ref, bglu_ref, wout_ref, wg0_ref, wu0_ref, wd0_ref, wg1_hbm, wu1_hbm, wd1_hbm, o_ref,
                 wg1_ref, wu1_ref, wd1_ref, sem):
    ready = _late_weights((wg1_hbm, wu1_hbm, wd1_hbm), (wg1_ref, wu1_ref, wd1_ref), sem)
    attn = jnp.where(pl.program_id(0) < n_prompt, attn_p[...], attn_l[...])
    mods = mods0_ref[...]
    g3 = g0_ref[...]
    y = dsk_ref[...] * u_ref[...] + yf_ref[...] + yb_ref[...]
    z = _gelu_tanh(y)
    s5o = z * _sigmoid(_dot(z.astype(BF16), wglu_ref[...]) + bglu_ref[...])
    half = HEADS * V_DIM
    mix = _dot(attn, wout_ref[0:half, :]) + _dot(s5o.astype(BF16), wout_ref[half:, :])
    x2 = x_ref[...] + mods[5:6] * mix
    x3 = _ffn(x2, mods[6:7], mods[7:8], mods[8:9], g3[2:3], wg0_ref, wu0_ref, wd0_ref)
    ready()
    mods = mods1_ref[...]
    o_ref[...] = _ffn(x3, mods[0:1], mods[1:2], mods[2:3], g1_ref[...][0:1], wg1_ref, wu1_ref, wd1_ref)


def _mixout(x, attn, u, yf, yb, mods, norm_g, wts, groups):
    in_specs = [_row_spec(D_MODEL), groups.spec(HEADS * V_DIM, 0), groups.spec(HEADS * V_DIM, 1)]
    in_specs += [_row_spec(S5_WIDTH)] * 3
    args = [x, *attn] + [a.reshape(groups.rows, S5_WIDTH) for a in (u, yf, yb)]
    in_specs += [_mods_spec(0, groups.cond), _mods_spec(1, groups.cond), _normg_spec(0), _normg_spec(1),
                 _const_spec((1, S5_WIDTH), (0, 0)), _const_spec((S5_WIDTH, S5_WIDTH), (0, 0)),
                 _const_spec((1, S5_WIDTH), (0, 0)), _const_spec((D_MODEL, D_MODEL), (0, 0))]
    late = wts["ffn"][1, 0]
    in_specs += _ffn_specs(wts["ffn"][0, 1], 0, 1) + [pl.BlockSpec(memory_space=pl.ANY)] * len(late)
    args += [mods, mods, norm_g, norm_g, wts["s5_d"], wts["w_glu"], wts["b_glu"], wts["w_out"],
             *wts["ffn"][0, 1], *late]
    return pl.pallas_call(
        functools.partial(_mixout_body, groups.prompt_tiles),
        out_shape=jax.ShapeDtypeStruct((groups.rows, D_MODEL), F32),
        grid=(groups.tiles,), in_specs=in_specs, out_specs=_row_spec(D_MODEL),
        scratch_shapes=[pltpu.VMEM(w.shape, w.dtype) for w in late] + [pltpu.SemaphoreType.DMA((len(late),))],
        compiler_params=_params(1), name="mixout_ffn",
    )(*args)


def _conv_body(n_prompt, seqs, x_ref, xp_ref, xn_ref, mods_ref, g_ref, cwin_ref, cw_ref, cwout_ref,
               wg_hbm, wu_hbm, wd_hbm, op_ref, ol_ref, wg_ref, wu_ref, wd_ref, sem):
    ready = _late_weights((wg_hbm, wu_hbm, wd_hbm), (wg_ref, wu_ref, wd_ref), sem)
    i = pl.program_id(0)
    is_prompt = i < n_prompt
    mods = mods_ref[...]
    g3 = g_ref[...]
    x = x_ref[...]
    n_ext = TM + 2 * SUBLANES
    xe = jnp.concatenate([xp_ref[...], x, xn_ref[...]], axis=0)
    hne = _modulate(xe, g3[1:2], mods[3:4], mods[4:5]).astype(BF16)
    pz = _dot(hne, cwin_ref[:, D_MODEL:3 * D_MODEL])
    z = pz[:, 0:D_MODEL] * pz[:, D_MODEL:2 * D_MODEL]
    gate_b = _dot(hne[SUBLANES:SUBLANES + TM], cwin_ref[:, 0:D_MODEL])
    main = slice(SUBLANES, SUBLANES + TM)
    row = i * TM + lax.broadcasted_iota(jnp.int32, (TM, 1), 0)
    pos = jnp.where(is_prompt, row % seqs[0], row % seqs[1])
    end = jnp.where(is_prompt, seqs[0] - 1, seqs[1] - 1)
    z_prev = jnp.where(pos == 0, 0.0, pltpu.roll(z, 1, 0)[main])
    z_next = jnp.where(pos == end, 0.0, pltpu.roll(z, n_ext - 1, 0)[main])
    cw = cw_ref[...]
    zc = z_prev * cw[0:1] + z[main] * cw[1:2] + z_next * cw[2:3]
    mix = _dot((gate_b * zc).astype(BF16), cwout_ref[...])
    x2 = x + mods[5:6] * mix
    ready()
    out = _ffn(x2, mods[6:7], mods[7:8], mods[8:9], g3[2:3], wg_ref, wu_ref, wd_ref)

    @pl.when(is_prompt)
    def _():
        op_ref[...] = out

    @pl.when(jnp.logical_not(is_prompt))
    def _():
        ol_ref[...] = out


def _conv_mixer(x, mods, norm_g, wts, groups):
    per = TM // SUBLANES
    last = groups.rows // SUBLANES - 1
    halo = (SUBLANES, D_MODEL)
    in_specs = [_row_spec(D_MODEL),
                pl.BlockSpec(halo, lambda i: (jnp.maximum(i * per - 1, 0), 0)),
                pl.BlockSpec(halo, lambda i: (jnp.minimum((i + 1) * per, last), 0)),
                _mods_spec(1, groups.cond), _normg_spec(1),
                _const_spec((D_MODEL, 3 * D_MODEL), (0, 0)), _const_spec((CONV_K, D_MODEL), (0, 0)),
                _const_spec((D_MODEL, D_MODEL), (0, 0))]
    late = wts["ffn"][1, 1]
    in_specs += [pl.BlockSpec(memory_space=pl.ANY)] * len(late)
    return pl.pallas_call(
        functools.partial(_conv_body, groups.prompt_tiles, groups.seq),
        out_shape=[jax.ShapeDtypeStruct((r, D_MODEL), F32) for r in groups.group_rows],
        grid=(groups.tiles,), in_specs=in_specs, out_specs=[groups.spec(D_MODEL, 0), groups.spec(D_MODEL, 1)],
        scratch_shapes=[pltpu.VMEM(w.shape, w.dtype) for w in late] + [pltpu.SemaphoreType.DMA((len(late),))],
        compiler_params=_params(1), name="conv_ffn",
    )(x, x, x, mods, norm_g, wts["conv_w_in"], wts["conv_w"], wts["conv_w_out"], *late)


ROPE_PARTNER = np.concatenate([np.arange(8, 16), np.arange(0, 8), np.arange(24, 32), np.arange(16, 24)])


def _rope_tables(seq_len, g_q, g_k):
    t = np.arange(seq_len)
    quarter = ROPE // 4
    inv_freq = ROPE_BASE ** (-np.arange(quarter, dtype=np.float64) / quarter)
    cos = np.ones((seq_len, LANES))
    sin = np.zeros((seq_len, LANES))
    for base, pos in ((NOPE, t // GRID_W), (NOPE + ROPE // 2, t % GRID_W)):
        ang = pos[:, None].astype(np.float64) * inv_freq[None, :]
        cos[:, base:base + quarter] = np.cos(ang)
        cos[:, base + quarter:base + 2 * quarter] = np.cos(ang)
        sin[:, base:base + quarter] = -np.sin(ang)
        sin[:, base + quarter:base + 2 * quarter] = np.sin(ang)
    cos, sin = jnp.asarray(cos, F32), jnp.asarray(sin, F32)
    out = []
    for g in (g_q, g_k):
        g = g.reshape(QK_DIM)
        g_pad = jnp.pad(g, (0, HEAD_PAD - QK_DIM))
        g_partner = jnp.pad(g[NOPE:][ROPE_PARTNER], (NOPE, HEAD_PAD - QK_DIM))
        out += [g_pad[None, :] * cos, g_partner[None, :] * sin]
    return tuple(out)


def _s5_params(lam_re, lam_im, log_dt, b_re, b_im, c_re, c_im):
    dt = jnp.exp(log_dt)[..., None]
    lr = jnp.minimum(lam_re, LAMBDA_RE_MAX)
    li = lam_im
    mag = jnp.exp(lr * dt)
    ang = li * dt
    ab_re = mag * jnp.cos(ang)
    ab_im = mag * jnp.sin(ang)
    den = lr * lr + li * li
    nr = ab_re - 1.0
    ni = ab_im
    co_re = (nr * lr + ni * li) / den
    co_im = (ni * lr - nr * li) / den
    bb_re = co_re[..., None] * b_re - co_im[..., None] * b_im
    bb_im = co_re[..., None] * b_im + co_im[..., None] * b_re
    per_q = S5_GROUPS // S5_QUARTERS
    eye = jnp.eye(per_q, dtype=F32)

    def in_mat(bb):
        bb = bb.reshape(2, S5_QUARTERS, per_q, S5_STATE, S5_GROUP)
        return jnp.einsum("dqgpi,gh->dqgihp", bb, eye).reshape(2, S5_QUARTERS, LANES, S5_QLANES)

    def out_mat(cc):
        cc = cc.reshape(2, S5_QUARTERS, per_q, S5_GROUP, S5_STATE)
        return jnp.einsum("dqgip,gh->dqgphi", cc, eye).reshape(2, S5_QUARTERS, S5_QLANES, LANES)

    bm = jnp.concatenate([in_mat(bb_re), in_mat(bb_im)], axis=-1).astype(BF16)
    cm = jnp.concatenate([out_mat(c_re), -out_mat(c_im)], axis=-2).astype(BF16)

    def rows(a):
        a = a.reshape(2, 1, S5_QUARTERS, S5_QLANES)
        return jnp.broadcast_to(a, (2, 2, S5_QUARTERS, S5_QLANES)).reshape(2, 2 * S5_QUARTERS, S5_QLANES)

    return bm, cm, jnp.stack([rows(ab_re), rows(ab_im)], axis=1)


def _state_rows(s):
    b = s.shape[0]
    s = s.reshape(b // 2, 2, 2, S5_QUARTERS, S5_QLANES)
    return s.transpose(0, 2, 1, 3, 4).reshape(b // 2, 2, 2 * S5_QUARTERS, S5_QLANES)


def _state_unrows(s, batch):
    s = s.reshape(batch // 2, 2, 2, S5_QUARTERS, S5_QLANES).transpose(0, 2, 1, 3, 4)
    return s.reshape(batch, 1, 2, S5_GROUPS, S5_STATE)


def kernel(x_prompt, x_sample, cache_ckv, cache_krope, state_ssm_re, state_ssm_im, c, c_ctx, w_ada, b_ada, norm_g, ffn_w_gate, ffn_w_up, ffn_w_down, ab_w_in, mla_g_q_lat, mla_g_kv_lat, mla_w_uq, mla_w_ukv, mla_g_qnorm, mla_g_knorm, s5_lam_re, s5_lam_im, s5_log_dt, s5_b_re, s5_b_im, s5_c_re, s5_c_im, s5_d, s5_w_glu, s5_b_glu, ab_w_out, conv_w_in, conv_w, conv_w_out):
    batch, seq, _ = x_prompt.shape
    dec_batch, dec_seq, _ = x_sample.shape
    past = cache_ckv.shape[2]

    w_in = ab_w_in[0]
    kr_cols = w_in[:, Q_LORA + KV_LORA:Q_LORA + KV_LORA + ROPE]
    zeros = jnp.zeros_like(kr_cols)
    w_uq = jnp.pad(mla_w_uq[0].reshape(Q_LORA, HEADS, QK_DIM), ((0, 0), (0, 0), (0, HEAD_PAD - QK_DIM)))
    w_ukv = mla_w_ukv[0].reshape(KV_LORA, HEADS, NOPE + V_DIM)
    w_uk = jnp.pad(w_ukv[:, :, :NOPE], ((0, 0), (0, 0), (0, HEAD_PAD - NOPE)))
    bm, cm, a_rows = _s5_params(s5_lam_re[0], s5_lam_im[0], s5_log_dt[0], s5_b_re[0], s5_b_im[0],
                                s5_c_re[0], s5_c_im[0])
    stacked = (ffn_w_gate, ffn_w_up, ffn_w_down)
    ffn = {(0, 0): tuple(w[0, 0].astype(BF16) for w in stacked)}
    w_in_cols = [w_in[:, :Q_LORA + KV_LORA], w_in[:, Q_LORA + KV_LORA + ROPE:], kr_cols, zeros, kr_cols, zeros]
    w_in_partner = [zeros, zeros, kr_cols[:, ROPE_PARTNER], zeros]
    w_uq3 = mla_w_uq[0].reshape(Q_LORA, HEADS, QK_DIM)
    w_uq_partner = jnp.pad(w_uq3[:, :, NOPE:][:, :, ROPE_PARTNER], ((0, 0), (0, 0), (NOPE, HEAD_PAD - QK_DIM)))
    wts = {
        "ffn": ffn,
        "w_in_rot": jnp.concatenate(w_in_cols + w_in_partner, axis=1).astype(BF16),
        "g_q_lat": mla_g_q_lat, "g_kv_lat": mla_g_kv_lat,
        "w_uq_rot": jnp.concatenate([w_uq, w_uq_partner], axis=1).reshape(Q_LORA, 2 * HEADS * HEAD_PAD).astype(BF16),
        "w_ukv": jnp.concatenate([w_uk.reshape(KV_LORA, HEADS * HEAD_PAD),
                                  w_ukv[:, :, NOPE:].reshape(KV_LORA, HEADS * V_DIM)], axis=1).astype(BF16),
        "g_qn": jnp.pad(mla_g_qnorm, ((0, 0), (0, HEAD_PAD - QK_DIM))),
        "g_kn": jnp.pad(mla_g_knorm, ((0, 0), (0, HEAD_PAD - QK_DIM))),
        "rope": _rope_tables(dec_seq, mla_g_qnorm, mla_g_knorm),
        "s5_b": bm, "s5_c": cm, "s5_a": a_rows,
        "s5_d": s5_d, "w_glu": s5_w_glu[0].astype(BF16), "b_glu": s5_b_glu, "w_out": ab_w_out[0].astype(BF16),
        "conv_w": conv_w[0].T, "conv_w_out": conv_w_out[0].astype(BF16),
    }

    cond = jnp.zeros((SUBLANES, D_MODEL), F32).at[0].set(c_ctx).at[1:1 + dec_batch].set(c)
    mods = _ada(cond, w_ada, b_ada).reshape(w_ada.shape[0], SUBLANES, N_MOD, D_MODEL)

    groups = _Groups(batch * seq, seq, dec_batch * dec_seq, dec_seq)
    rows_p, rows_l = groups.group_rows
    x1, q, k, v, u, ckv_p, kr_p = _mixin(
        x_prompt.reshape(rows_p, D_MODEL), x_sample.reshape(rows_l, D_MODEL), mods, norm_g, wts, groups)
    kr_pad = jnp.pad(cache_krope[:, 0].reshape(dec_batch * past, ROPE), ((0, 0), (NOPE, LANES - QK_DIM)))
    ctx = _ctx_kv(cache_ckv[:, 0].reshape(dec_batch * past, KV_LORA), kr_pad, wts)
    h0 = jnp.stack([_state_rows(state_ssm_re[:, 0]), _state_rows(state_ssm_im[:, 0])], axis=2)
    ffn_set = lambda l, s: [(w, (l, s)) for w in stacked]
    attn_p, yf, yb, down11, wts["conv_w_in"] = _attention(
        q, k, v, 0, rows_p, seq, seq, n_seq=2, zeros=[(groups.rows, S5_WIDTH)] * 2,
        cast=[(ffn_w_down, (1, 1)), (conv_w_in, (0,))])
    attn_l, *ffn[0, 1] = _attention(q, k, v, rows_p, rows_l, dec_seq, 256, ctx, cast=ffn_set(0, 1))
    attn = (attn_p, attn_l)
    yf, yb, fin, *ffn[1, 0] = _s5(u, wts, 0, rows_p, seq, want_fin=True, y_prev=(yf, yb), cast=ffn_set(1, 0))
    yf, yb, gate11, up11 = _s5(u, wts, rows_p, rows_l, dec_seq, h0=h0, y_prev=(yf, yb),
                               cast=[(ffn_w_gate, (1, 1)), (ffn_w_up, (1, 1))])
    ffn[1, 1] = (gate11, up11, down11)
    x3 = _mixout(x1, attn, u, yf, yb, mods, norm_g, wts, groups)
    y_p, y_s = _conv_mixer(x3, mods, norm_g, wts, groups)
    return (y_p.reshape(batch, seq, D_MODEL), y_s.reshape(dec_batch, dec_seq, D_MODEL),
            ckv_p.reshape(batch, 1, seq, KV_LORA), kr_p.reshape(batch, 1, seq, ROPE),
            _state_unrows(fin[:, :, 0], batch), _state_unrows(fin[:, :, 1], batch))
```

```python
import functools
import math

import jax
import jax.numpy as jnp
import numpy as np
from jax import lax
from jax.experimental import pallas as pl
from jax.experimental.pallas import tpu as pltpu

F32 = jnp.float32
BF16 = jnp.bfloat16

LANES = 128
SUBLANES = 8
VMEM_LIMIT_BYTES = 60 * 1024 * 1024

D_MODEL = 1024
D_FF = 2816
N_MOD = 9
EPS = 1e-6
HEADS = 8
Q_LORA = 384
KV_LORA = 256
NOPE = 64
ROPE = 32
V_DIM = 64
QK_DIM = NOPE + ROPE
HEAD_PAD = LANES
ROPE_BASE = 10000.0
GRID_W = 64
S5_WIDTH = 512
S5_GROUP = 16
S5_GROUPS = 32
S5_STATE = 64
S5_LANES = S5_GROUPS * S5_STATE
S5_QUARTERS = 4
S5_QLANES = S5_LANES // S5_QUARTERS
S5_SLABS = S5_QLANES // LANES
S5_TC = 128
S5_PITCH = S5_TC + 4
S5_SEGMENTS = 1
LAMBDA_RE_MAX = -1e-4
CONV_K = 3

ATT_ROWS = 2 * SUBLANES
TM = 512
FF_CHUNKS = tuple((c, min(512, D_FF - c)) for c in range(0, D_FF, 512))

O_CKV = Q_LORA
O_U = Q_LORA + KV_LORA
O_KR = O_U + S5_WIDTH


def _dot(a, b):
    return jnp.dot(a, b, preferred_element_type=F32)


def _sigmoid(x):
    return 1.0 / (1.0 + jnp.exp(-x))


def _rms(x, g, n):
    ms = jnp.sum(x * x, axis=-1, keepdims=True) * (1.0 / n)
    return x * lax.rsqrt(ms + EPS) * g


def _modulate(x, g, shift, scale):
    return _rms(x, g, D_MODEL) * (1.0 + scale) + shift


def _ffn(x, shift, scale, gate, g, wg_ref, wu_ref, wd_ref):
    h = _modulate(x, g, shift, scale).astype(BF16)
    acc = jnp.zeros(x.shape, F32)
    for c0, cs in FF_CHUNKS:
        gt = _dot(h, wg_ref[:, c0:c0 + cs])
        up = _dot(h, wu_ref[:, c0:c0 + cs])
        act = ((gt * _sigmoid(gt)) * up).astype(BF16)
        acc = acc + _dot(act, wd_ref[c0:c0 + cs, :])
    return x + (0.5 * gate) * acc


def _head_inv_rms(xh):
    return lax.rsqrt(jnp.sum(xh * xh, axis=-1, keepdims=True) * (1.0 / QK_DIM) + EPS)


def _head_norm(xh, g):
    return xh * _head_inv_rms(xh) * g


def _gelu_tanh(x):
    return x * (0.5 * (1.0 + jnp.tanh(math.sqrt(2.0 / math.pi) * (x + 0.044715 * (x * x * x)))))


def _const_spec(shape, index):
    return pl.BlockSpec(shape, lambda *_: index, pipeline_mode=pl.Buffered(1))


def _params(n_axes):
    return pltpu.CompilerParams(dimension_semantics=("arbitrary",) * n_axes, vmem_limit_bytes=VMEM_LIMIT_BYTES)


def _ffn_specs(weights, layer, sub):
    return [_const_spec(w.shape, (0, 0)) if w.ndim == 2 else
            _const_spec((None, None) + w.shape[2:], (layer, sub, 0, 0)) for w in weights]


def _mods_spec(layer, cond_of_tile):
    return pl.BlockSpec((None, None, N_MOD, D_MODEL), lambda i: (layer, cond_of_tile(i), 0, 0))


def _normg_spec(layer):
    return _const_spec((None, 3, D_MODEL), (layer, 0, 0))


def _row_spec(width, rows=TM):
    return pl.BlockSpec((rows, width), lambda i: (i, 0))


def _cast_job(items, n_steps, step_of):
    in_specs, out_specs, out_shape = [], [], []
    for w, lead in items:
        r, c = w.shape[len(lead):]
        assert r % (n_steps * 2 * SUBLANES) == 0
        slab = lambda *g: jnp.minimum(step_of(*g), n_steps - 1)
        in_specs.append(pl.BlockSpec((None,) * len(lead) + (r // n_steps, c),
                                     lambda *g, lead=lead, slab=slab: (*lead, slab(*g), 0)))
        out_specs.append(pl.BlockSpec((r // n_steps, c), lambda *g, slab=slab: (slab(*g), 0)))
        out_shape.append(jax.ShapeDtypeStruct((r, c), BF16))
    return in_specs, out_specs, out_shape


def _cast_slabs(src_refs, dst_refs):
    for src, dst in zip(src_refs, dst_refs):
        dst[...] = src[...].astype(BF16)


class _Groups:
    def __init__(self, prompt_rows, prompt_seq, latent_rows, latent_seq):
        assert prompt_rows % TM == 0 and latent_seq % TM == 0 and TM % prompt_seq == 0
        assert prompt_rows % latent_seq == 0
        self.group_rows = (prompt_rows, latent_rows)
        self.seq = (prompt_seq, latent_seq)
        self.rows = prompt_rows + latent_rows
        self.prompt_tiles = prompt_rows // TM
        self.tiles = self.rows // TM
        self.latent_tiles_per_seq = latent_seq // TM

    def cond(self, i):
        return jnp.where(i < self.prompt_tiles, 0,
                         1 + jnp.maximum(i - self.prompt_tiles, 0) // self.latent_tiles_per_seq)

    def spec(self, width, group):
        n = self.prompt_tiles
        if group == 0:
            return pl.BlockSpec((TM, width), lambda i: (jnp.minimum(i, n - 1), 0))
        return pl.BlockSpec((TM, width), lambda i: (jnp.maximum(i - n, 0), 0))


def _ada_body(c_ref, w_ref, b_ref, o_ref):
    c = c_ref[...]
    s = (c * _sigmoid(c)).astype(BF16)
    o_ref[...] = _dot(s, w_ref[...].astype(BF16)) + b_ref[...]


def _ada(cond, w_ada, b_ada):
    depth, _, n = w_ada.shape
    tn = 2304
    return pl.pallas_call(
        _ada_body,
        out_shape=jax.ShapeDtypeStruct((depth, SUBLANES, n), F32),
        grid=(depth, n // tn),
        in_specs=[pl.BlockSpec((SUBLANES, D_MODEL), lambda l, j: (0, 0)),
                  pl.BlockSpec((None, D_MODEL, tn), lambda l, j: (l, 0, j)),
                  pl.BlockSpec((None, 1, tn), lambda l, j: (l, 0, j))],
        out_specs=pl.BlockSpec((None, SUBLANES, tn), lambda l, j: (l, 0, j)),
        compiler_params=_params(2),
        name="ada",
    )(cond, w_ada, b_ada.reshape(depth, 1, n))


def _mixin_body(n_prompt, *refs):
    (xp_ref, xs_ref, mods_ref, g_ref, wg_ref, wu_ref, wd_ref, win_ref, gql_ref, gkv_ref, wuq_ref, wukv_ref,
     gq_ref, gk_ref, rqc_ref, rqs_ref, rkc_ref, rks_ref,
     x1_ref, q_ref, k_ref, v_ref, u_ref, ckv_ref, kr_ref) = refs
    is_prompt = pl.program_id(0) < n_prompt
    mods = mods_ref[...]
    g3 = g_ref[...]
    x = jnp.where(is_prompt, xp_ref[...], xs_ref[...])
    x1 = _ffn(x, mods[0:1], mods[1:2], mods[2:3], g3[0:1], wg_ref, wu_ref, wd_ref)
    x1_ref[...] = x1
    hn = _modulate(x1, g3[1:2], mods[3:4], mods[4:5]).astype(BF16)
    proj = _dot(hn, win_ref[:, 0:O_KR + LANES])
    u_ref[...] = proj[:, O_U:O_KR]
    ckv = _rms(proj[:, O_CKV:O_U], gkv_ref[...], KV_LORA)
    krg = proj[:, O_KR:O_KR + LANES]
    qn = _rms(proj[:, 0:Q_LORA], gql_ref[...], Q_LORA).astype(BF16)
    kv = _dot(ckv.astype(BF16), wukv_ref[...])
    v_ref[...] = kv[:, HEADS * HEAD_PAD:].astype(BF16)
    lane = lax.broadcasted_iota(jnp.int32, (1, LANES), 1)
    kr_only = jnp.where((lane >= NOPE) & (lane < QK_DIM), krg, 0.0)
    gk = gk_ref[...]
    head = lambda h: slice(h * HEAD_PAD, (h + 1) * HEAD_PAD)

    @pl.when(is_prompt)
    def _():
        ckv_ref[...] = ckv
        kr_ref[...] = krg[:, 0:ROPE]
        qraw = _dot(qn, wuq_ref[:, 0:HEADS * HEAD_PAD])
        gq = gq_ref[...]
        for h in range(HEADS):
            q_ref[:, head(h)] = _head_norm(qraw[:, head(h)], gq).astype(BF16)
            k_ref[:, head(h)] = _head_norm(kv[:, head(h)] + kr_only, gk).astype(BF16)

    @pl.when(jnp.logical_not(is_prompt))
    def _():
        qraw = _dot(qn, wuq_ref[...])
        kr_partner = _dot(hn, win_ref[:, O_KR + LANES:O_KR + 2 * LANES])
        k_rot = kr_only * rkc_ref[...] + kr_partner * rks_ref[...]
        for h in range(HEADS):
            qa = qraw[:, head(h)]
            qh = (qa * rqc_ref[...] + qraw[:, head(HEADS + h)] * rqs_ref[...]) * _head_inv_rms(qa)
            kh = (kv[:, head(h)] * gk + k_rot) * _head_inv_rms(kv[:, head(h)] + kr_only)
            q_ref[:, head(h)] = qh.astype(BF16)
            k_ref[:, head(h)] = kh.astype(BF16)


def _mixin(x_prompt, x_latent, mods, norm_g, wts, groups):
    n_prompt = groups.prompt_tiles
    w_in, w_uq = wts["w_in_rot"], wts["w_uq_rot"]
    in_specs = [groups.spec(D_MODEL, 0), groups.spec(D_MODEL, 1), _mods_spec(0, groups.cond), _normg_spec(0)]
    in_specs += _ffn_specs(wts["ffn"][0, 0], 0, 0) + [
        _const_spec(w_in.shape, (0, 0)),
        _const_spec((1, Q_LORA), (0, 0)),
        _const_spec((1, KV_LORA), (0, 0)),
        _const_spec(w_uq.shape, (0, 0)),
        _const_spec((KV_LORA, HEADS * HEAD_PAD + HEADS * V_DIM), (0, 0)),
        _const_spec((1, HEAD_PAD), (0, 0)),
        _const_spec((1, HEAD_PAD), (0, 0)),
    ] + [pl.BlockSpec((TM, LANES), lambda i: (jnp.maximum(i - n_prompt, 0) % groups.latent_tiles_per_seq, 0))] * 4
    args = [x_prompt, x_latent, mods, norm_g, *wts["ffn"][0, 0], w_in, wts["g_q_lat"],
            wts["g_kv_lat"], w_uq, wts["w_ukv"], wts["g_qn"], wts["g_kn"]] + list(wts["rope"])
    out_shape, out_specs = [], []
    for w, dt in ((D_MODEL, F32), (HEADS * HEAD_PAD, BF16), (HEADS * HEAD_PAD, BF16), (HEADS * V_DIM, BF16),
                  (S5_WIDTH, F32)):
        out_shape.append(jax.ShapeDtypeStruct((groups.rows, w), dt))
        out_specs.append(_row_spec(w))
    for w in (KV_LORA, ROPE):
        out_shape.append(jax.ShapeDtypeStruct((groups.group_rows[0], w), F32))
        out_specs.append(groups.spec(w, 0))
    return pl.pallas_call(
        functools.partial(_mixin_body, n_prompt),
        out_shape=out_shape, grid=(groups.tiles,), in_specs=in_specs, out_specs=out_specs,
        compiler_params=_params(1), name="ffn_mixin",
    )(*args)


def _ctx_body(ckv_ref, krp_ref, wukv_ref, gk_ref, k_ref, v_ref):
    kv = _dot(ckv_ref[...].astype(BF16), wukv_ref[...])
    v_ref[...] = kv[:, HEADS * HEAD_PAD:].astype(BF16)
    gk = gk_ref[...]
    krp = krp_ref[...]
    for h in range(HEADS):
        sl = slice(h * HEAD_PAD, (h + 1) * HEAD_PAD)
        k_ref[:, sl] = _head_norm(kv[:, sl] + krp, gk).astype(BF16)


def _ctx_kv(ckv, kr_padded, wts):
    rows = ckv.shape[0]
    return pl.pallas_call(
        _ctx_body,
        out_shape=[jax.ShapeDtypeStruct((rows, HEADS * HEAD_PAD), BF16),
                   jax.ShapeDtypeStruct((rows, HEADS * V_DIM), BF16)],
        grid=(rows // TM,),
        in_specs=[_row_spec(KV_LORA), _row_spec(LANES),
                  _const_spec((KV_LORA, HEADS * HEAD_PAD + HEADS * V_DIM), (0, 0)),
                  _const_spec((1, HEAD_PAD), (0, 0))],
        out_specs=[_row_spec(HEADS * HEAD_PAD), _row_spec(HEADS * V_DIM)],
        compiler_params=_params(1), name="ctx_kv",
    )(ckv, kr_padded, wts["w_ukv"], wts["g_kn"])


def _attn_body(n_kv, n_seq, n_zero, n_cast, *refs):
    refs = list(refs)
    q_ref = refs.pop(0)
    k_refs, v_refs, cast_in = refs[:n_kv], refs[n_kv:2 * n_kv], refs[2 * n_kv:2 * n_kv + n_cast]
    refs = refs[2 * n_kv + n_cast:]
    o_ref = refs.pop(0)
    zero_refs, cast_out = refs[:n_zero], refs[n_zero:n_zero + n_cast]
    s_ref, p_ref, r_ref = refs[n_zero + n_cast:]
    _cast_slabs(cast_in, cast_out)
    for z_ref in zero_refs:
        z_ref[...] = jnp.zeros(z_ref.shape, z_ref.dtype)
    tq = q_ref.shape[0] // n_seq
    c = (QK_DIM ** -0.5) * math.log2(math.e)
    lane = lax.broadcasted_iota(jnp.int32, (1, LANES), 1)
    n_keys = [k_ref.shape[0] // n_seq for k_ref in k_refs]
    spans = [slice(sum(n_keys[:i]), sum(n_keys[:i + 1])) for i in range(n_kv)]
    for sq, pair in [(sq, pair) for sq in range(n_seq) for pair in range(HEADS // 2)]:
        qrows = slice(sq * tq, (sq + 1) * tq)
        krows = [slice(sq * n, (sq + 1) * n) for n in n_keys]
        wide = slice(pair // 2 * 2 * LANES, (pair // 2 + 1) * 2 * LANES)
        mine = slice(pair % 2 * LANES, (pair % 2 + 1) * LANES)
        vsl = slice(pair * LANES, (pair + 1) * LANES)
        outs = []
        for h in (2 * pair, 2 * pair + 1):
            slot = h % 2
            sl = slice(h * HEAD_PAD, (h + 1) * HEAD_PAD)
            q = q_ref[qrows, sl]
            for k_ref, kr, span in zip(k_refs, krows, spans):
                s_ref[slot, :, span] = lax.dot_general(q, k_ref[kr, sl], (((1,), (1,)), ((), ())),
                                                       preferred_element_type=F32)
            for r0 in range(0, tq, ATT_ROWS):
                rows = slice(r0, r0 + ATT_ROWS)
                s = s_ref[slot, rows, :]
                e = jnp.exp2((s - jnp.max(s, axis=-1, keepdims=True)) * c)
                r_ref[slot, rows, :] = jnp.broadcast_to(1.0 / jnp.sum(e, axis=-1, keepdims=True), (ATT_ROWS, LANES))
                p_ref[slot, rows, :] = e.astype(BF16)
            o = functools.reduce(jnp.add, [_dot(p_ref[slot, :, span], v_ref[kr, wide])
                                           for v_ref, kr, span in zip(v_refs, krows, spans)])
            outs.append(o[:, mine] * r_ref[slot])
        o_ref[qrows, vsl] = jnp.where(lane < V_DIM, outs[0], outs[1]).astype(BF16)


def _attention(q, k, v, row0, rows, seq_len, tq, ctx=None, n_seq=1, zeros=(), cast=()):
    q_tiles = seq_len // tq
    n_seq = n_seq if q_tiles == 1 and ctx is None else 1
    q0, kv0 = row0 // (n_seq * tq), row0 // (n_seq * seq_len)
    kv_spec = lambda w: pl.BlockSpec((n_seq * seq_len, w), lambda b, t: (kv0 + b, 0))
    in_specs = [pl.BlockSpec((n_seq * tq, HEADS * HEAD_PAD), lambda b, t: (q0 + b * q_tiles + t, 0))]
    ks, vs = [kv_spec(HEADS * HEAD_PAD)], [kv_spec(HEADS * V_DIM)]
    kargs, vargs = [k], [v]
    n_keys = seq_len
    if ctx is not None:
        k_c, v_c = ctx
        past = k_c.shape[0] // (rows // seq_len)
        ks.insert(0, pl.BlockSpec((past, HEADS * HEAD_PAD), lambda b, t: (b, 0)))
        vs.insert(0, pl.BlockSpec((past, HEADS * V_DIM), lambda b, t: (b, 0)))
        kargs.insert(0, k_c)
        vargs.insert(0, v_c)
        n_keys += past
    grid = (rows // (n_seq * seq_len), q_tiles)
    out_shape = [jax.ShapeDtypeStruct((rows, HEADS * V_DIM), BF16)]
    out_specs = [pl.BlockSpec((n_seq * tq, HEADS * V_DIM), lambda b, t: (b * q_tiles + t, 0))]
    for shape in zeros:
        assert q_tiles == 1 and shape[0] % (SUBLANES * grid[0]) == 0
        out_shape.append(jax.ShapeDtypeStruct(shape, F32))
        out_specs.append(pl.BlockSpec((shape[0] // grid[0], shape[1]), lambda b, t: (b, 0)))
    in_specs, args = in_specs + ks + vs, [q, *kargs, *vargs]
    c_in, c_out, c_shape = _cast_job(cast, grid[0] * grid[1], lambda b, t: b * q_tiles + t)
    in_specs, args = in_specs + c_in, args + [w for w, _ in cast]
    out_specs, out_shape = out_specs + c_out, out_shape + c_shape
    return pl.pallas_call(
        functools.partial(_attn_body, len(kargs), n_seq, len(zeros), len(cast)),
        out_shape=out_shape, grid=grid, in_specs=in_specs, out_specs=out_specs,
        scratch_shapes=[pltpu.VMEM((2, tq, n_keys), F32), pltpu.VMEM((2, tq, n_keys), BF16),
                        pltpu.VMEM((2, tq, LANES), F32)],
        compiler_params=_params(2), name="attention",
    )(*args)


def _s5_project_in(u_ref, rows, bm_ref, d, q, bu_ref):
    qsl = slice(q * LANES, (q + 1) * LANES)
    lhs = jnp.concatenate([u_ref[0, rows, qsl], u_ref[1, rows, qsl]], axis=0).astype(BF16)
    bu = _dot(lhs, bm_ref[d, q])
    for b in range(2):
        r = b * S5_QUARTERS + q
        for j in range(2 * S5_SLABS):
            bu_ref[d, j, pl.ds(r * S5_PITCH, S5_TC), :] = bu[b * S5_TC:(b + 1) * S5_TC, j * LANES:(j + 1) * LANES]


def _s5_scan(bu_ref, st_ref, a_ref, carry_ref, steps):
    nrow = 2 * S5_QUARTERS
    order = [(d, j) for d in range(2) for j in range(S5_SLABS)]
    lanes = lambda j: slice(j * LANES, (j + 1) * LANES)
    carry = {(d, j): (carry_ref[d, 0, :, lanes(j)], carry_ref[d, 1, :, lanes(j)]) for d, j in order}
    for i in steps:
        for d, j in order:
            t = i if d == 0 else S5_TC - 1 - i
            rows = pl.ds(t, nrow, stride=S5_PITCH)
            sr, si = carry[d, j]
            ar, ai = a_ref[d, 0, :, lanes(j)], a_ref[d, 1, :, lanes(j)]
            nr = ar * sr - ai * si + bu_ref[d, j, rows, :]
            ni = ar * si + ai * sr + bu_ref[d, S5_SLABS + j, rows, :]
            st_ref[d, j, rows, :] = nr
            st_ref[d, S5_SLABS + j, rows, :] = ni
            carry[d, j] = (nr, ni)
    for d, j in order:
        carry_ref[d, 0, :, lanes(j)], carry_ref[d, 1, :, lanes(j)] = carry[d, j]


def _s5_project_out(st_ref, d, q, cm_ref, y_ref, rows):
    qsl = slice(q * LANES, (q + 1) * LANES)
    lhs = jnp.concatenate(
        [jnp.concatenate([st_ref[d, j, pl.ds((b * S5_QUARTERS + q) * S5_PITCH, S5_TC), :]
                          for j in range(2 * S5_SLABS)], axis=-1) for b in range(2)], axis=0).astype(BF16)
    y = _dot(lhs, cm_ref[d, q])
    y_ref[0, rows, qsl] = y[0:S5_TC]
    y_ref[1, rows, qsl] = y[S5_TC:2 * S5_TC]


def _s5_body(n_super, has_h0, want_fin, n_aliased, n_cast, *refs):
    refs = list(refs)
    uf_ref = refs.pop(0)
    ub_ref = refs.pop(0) if n_super > 1 else uf_ref
    bm_ref, cm_ref, a_ref = refs[:3]
    refs = refs[3:]
    h0_ref = refs.pop(0) if has_h0 else None
    refs = refs[n_aliased:]
    cast_in, refs = refs[:n_cast], refs[n_cast:]
    yf_ref, yb_ref = refs[:2]
    refs = refs[2:]
    fin_ref = refs.pop(0) if want_fin else None
    cast_out, refs = refs[:n_cast], refs[n_cast:]
    bu0_ref, bu1_ref, st0_ref, st1_ref, carry_ref = refs
    _cast_slabs(cast_in, cast_out)
    s = pl.program_id(0)
    lo, hi = slice(0, S5_TC), slice(S5_TC, 2 * S5_TC)
    n_seg = S5_SEGMENTS
    seg_steps = S5_TC // n_seg
    seg_pairs = 2 * S5_QUARTERS // n_seg

    @pl.when(s == 0)
    def _():
        bu1_ref[...] = jnp.zeros(bu1_ref.shape, F32)
        st0_ref[...] = jnp.zeros(st0_ref.shape, F32)
        carry_ref[...] = jnp.zeros(carry_ref.shape, F32)

    def half(bu_w, bu_r, st_w, st_r, rows_f, rows_b, tag):
        for k in range(n_seg):
            @pl.when(s > -(1 + tag * n_seg + k))
            def _():
                io = ((uf_ref, rows_f, yf_ref), (ub_ref, rows_b, yb_ref))
                sub = seg_steps // seg_pairs
                for m in range(k * seg_pairs, (k + 1) * seg_pairs):
                    d, q = divmod(m, S5_QUARTERS)
                    _s5_project_in(io[d][0], io[d][1], bm_ref, d, q, bu_w)
                    _s5_scan(bu_r, st_w, a_ref, carry_ref, range(m * sub, (m + 1) * sub))
                    _s5_project_out(st_r, d, q, cm_ref, io[d][2], io[d][1])

    half(bu0_ref, bu1_ref, st1_ref, st0_ref, lo, hi, 0)

    @pl.when(s > -(1 + 2 * n_seg))
    def _():
        if want_fin:
            fin_ref[...] = carry_ref[...]
        start = h0_ref[...] if has_h0 else jnp.zeros(carry_ref.shape, F32)
        if n_super > 1:
            start = jnp.where(s % n_super == 0, start, carry_ref[...])
        carry_ref[...] = start

    half(bu1_ref, bu0_ref, st0_ref, st1_ref, hi, lo, 1)


def _s5(u, wts, row0, rows, seq_len, h0=None, want_fin=False, y_prev=None, cast=()):
    batch = rows // seq_len
    n_super = seq_len // (2 * S5_TC)
    n_steps = (batch // 2) * n_super
    assert not (want_fin and n_super > 1) and row0 % (2 * seq_len) == 0
    pair0 = row0 // (2 * seq_len)
    u3 = u.reshape(u.shape[0] // seq_len, seq_len, S5_WIDTH)
    nrow = 2 * S5_QUARTERS
    blk = (2, 2 * S5_TC, S5_WIDTH)

    def chunk_spec(delay, reverse):
        def index(s):
            s = jnp.clip(s - delay, 0, n_steps - 1)
            c = s % n_super
            return (pair0 + s // n_super, n_super - 1 - c if reverse else c, 0)
        return pl.BlockSpec(blk, index)

    def state_spec(delay):
        return pl.BlockSpec((None, 2, 2, nrow, S5_QLANES),
                            lambda s: (jnp.clip(s - delay, 0, n_steps - 1) // n_super, 0, 0, 0, 0))

    in_specs, args = [chunk_spec(0, False)], [u3]
    if n_super > 1:
        in_specs.append(chunk_spec(0, True))
        args.append(u3)
    in_specs += [_const_spec((2, S5_QUARTERS, LANES, 2 * S5_QLANES), (0, 0, 0, 0)),
                 _const_spec((2, S5_QUARTERS, 2 * S5_QLANES, LANES), (0, 0, 0, 0)),
                 _const_spec((2, 2, nrow, S5_QLANES), (0, 0, 0, 0))]
    args += [wts["s5_b"], wts["s5_c"], wts["s5_a"]]
    if h0 is not None:
        in_specs.append(state_spec(0))
        args.append(h0)
    aliases = {}
    if y_prev is not None:
        for n, y in enumerate(y_prev):
            aliases[len(args)] = n
            in_specs.append(pl.BlockSpec(memory_space=pl.ANY))
            args.append(y.reshape(u3.shape))
    y_shape = jax.ShapeDtypeStruct(u3.shape, F32)
    out_shape, out_specs = [y_shape, y_shape], [chunk_spec(1, False), chunk_spec(1, True)]
    if want_fin:
        out_shape.append(jax.ShapeDtypeStruct((batch // 2, 2, 2, nrow, S5_QLANES), F32))
        out_specs.append(state_spec(1))
    c_in, c_out, c_shape = _cast_job(cast, n_steps, lambda s: s)
    in_specs, args = in_specs + c_in, args + [w for w, _ in cast]
    out_specs, out_shape = out_specs + c_out, out_shape + c_shape
    work = pltpu.VMEM((2, 2 * S5_SLABS, nrow * S5_PITCH, LANES), F32)
    return pl.pallas_call(
        functools.partial(_s5_body, n_super, h0 is not None, want_fin, len(aliases), len(cast)),
        out_shape=out_shape, grid=(n_steps + 1,), in_specs=in_specs, out_specs=out_specs,
        scratch_shapes=[work, work, work, work, pltpu.VMEM((2, 2, nrow, S5_QLANES), F32)],
        input_output_aliases=aliases,
        compiler_params=_params(1), name="s5",
    )(*args)


def _late_weights(hbm_refs, vmem_refs, sem):
    first = pl.program_id(0) == 0
    copies = [pltpu.make_async_copy(h, v, sem.at[n]) for n, (h, v) in enumerate(zip(hbm_refs, vmem_refs))]

    @pl.when(first)
    def _():
        for c in copies:
            c.start()

    def ready():
        @pl.when(first)
        def _():
            for c in copies:
                c.wait()

    return ready


def _mixout_body(n_prompt, x_ref, attn_p, attn_l, u_ref, yf_ref, yb_ref, mods0_ref, mods1_ref, g0_ref, g1_ref,
                 dsk_ref, wglu_ref, bglu_ref, wout_ref, wg0_ref, wu0_ref, wd0_ref, wg1_hbm, wu1_hbm, wd1_hbm, o_ref,
                 wg1_ref, wu1_ref, wd1_ref, sem):
    ready = _late_weights((wg1_hbm, wu1_hbm, wd1_hbm), (wg1_ref, wu1_ref, wd1_ref), sem)
    attn = jnp.where(pl.program_id(0) < n_prompt, attn_p[...], attn_l[...])
    mods = mods0_ref[...]
    g3 = g0_ref[...]
    y = dsk_ref[...] * u_ref[...] + yf_ref[...] + yb_ref[...]
    z = _gelu_tanh(y)
    s5o = z * _sigmoid(_dot(z.astype(BF16), wglu_ref[...]) + bglu_ref[...])
    half = HEADS * V_DIM
    mix = _dot(attn, wout_ref[0:half, :]) + _dot(s5o.astype(BF16), wout_ref[half:, :])
    x2 = x_ref[...] + mods[5:6] * mix
    x3 = _ffn(x2, mods[6:7], mods[7:8], mods[8:9], g3[2:3], wg0_ref, wu0_ref, wd0_ref)
    ready()
    mods = mods1_ref[...]
    o_ref[...] = _ffn(x3, mods[0:1], mods[1:2], mods[2:3], g1_ref[...][0:1], wg1_ref, wu1_ref, wd1_ref)


def _mixout(x, attn, u, yf, yb, mods, norm_g, wts, groups):
    in_specs = [_row_spec(D_MODEL), groups.spec(HEADS * V_DIM, 0), groups.spec(HEADS * V_DIM, 1)]
    in_specs += [_row_spec(S5_WIDTH)] * 3
    args = [x, *attn] + [a.reshape(groups.rows, S5_WIDTH) for a in (u, yf, yb)]
    in_specs += [_mods_spec(0, groups.cond), _mods_spec(1, groups.cond), _normg_spec(0), _normg_spec(1),
                 _const_spec((1, S5_WIDTH), (0, 0)), _const_spec((S5_WIDTH, S5_WIDTH), (0, 0)),
                 _const_spec((1, S5_WIDTH), (0, 0)), _const_spec((D_MODEL, D_MODEL), (0, 0))]
    late = wts["ffn"][1, 0]
    in_specs += _ffn_specs(wts["ffn"][0, 1], 0, 1) + [pl.BlockSpec(memory_space=pl.ANY)] * len(late)
    args += [mods, mods, norm_g, norm_g, wts["s5_d"], wts["w_glu"], wts["b_glu"], wts["w_out"],
             *wts["ffn"][0, 1], *late]
    return pl.pallas_call(
        functools.partial(_mixout_body, groups.prompt_tiles),
        out_shape=jax.ShapeDtypeStruct((groups.rows, D_MODEL), F32),
        grid=(groups.tiles,), in_specs=in_specs, out_specs=_row_spec(D_MODEL),
        scratch_shapes=[pltpu.VMEM(w.shape, w.dtype) for w in late] + [pltpu.SemaphoreType.DMA((len(late),))],
        compiler_params=_params(1), name="mixout_ffn",
    )(*args)


def _conv_body(n_prompt, seqs, x_ref, xp_ref, xn_ref, mods_ref, g_ref, cwin_ref, cw_ref, cwout_ref,
               wg_hbm, wu_hbm, wd_hbm, op_ref, ol_ref, wg_ref, wu_ref, wd_ref, sem):
    ready = _late_weights((wg_hbm, wu_hbm, wd_hbm), (wg_ref, wu_ref, wd_ref), sem)
    i = pl.program_id(0)
    is_prompt = i < n_prompt
    mods = mods_ref[...]
    g3 = g_ref[...]
    x = x_ref[...]
    n_ext = TM + 2 * SUBLANES
    xe = jnp.concatenate([xp_ref[...], x, xn_ref[...]], axis=0)
    hne = _modulate(xe, g3[1:2], mods[3:4], mods[4:5]).astype(BF16)
    pz = _dot(hne, cwin_ref[:, D_MODEL:3 * D_MODEL])
    z = pz[:, 0:D_MODEL] * pz[:, D_MODEL:2 * D_MODEL]
    gate_b = _dot(hne[SUBLANES:SUBLANES + TM], cwin_ref[:, 0:D_MODEL])
    main = slice(SUBLANES, SUBLANES + TM)
    row = i * TM + lax.broadcasted_iota(jnp.int32, (TM, 1), 0)
    pos = jnp.where(is_prompt, row % seqs[0], row % seqs[1])
    end = jnp.where(is_prompt, seqs[0] - 1, seqs[1] - 1)
    z_prev = jnp.where(pos == 0, 0.0, pltpu.roll(z, 1, 0)[main])
    z_next = jnp.where(pos == end, 0.0, pltpu.roll(z, n_ext - 1, 0)[main])
    cw = cw_ref[...]
    zc = z_prev * cw[0:1] + z[main] * cw[1:2] + z_next * cw[2:3]
    mix = _dot((gate_b * zc).astype(BF16), cwout_ref[...])
    x2 = x + mods[5:6] * mix
    ready()
    out = _ffn(x2, mods[6:7], mods[7:8], mods[8:9], g3[2:3], wg_ref, wu_ref, wd_ref)

    @pl.when(is_prompt)
    def _():
        op_ref[...] = out

    @pl.when(jnp.logical_not(is_prompt))
    def _():
        ol_ref[...] = out


def _conv_mixer(x, mods, norm_g, wts, groups):
    per = TM // SUBLANES
    last = groups.rows // SUBLANES - 1
    halo = (SUBLANES, D_MODEL)
    in_specs = [_row_spec(D_MODEL),
                pl.BlockSpec(halo, lambda i: (jnp.maximum(i * per - 1, 0), 0)),
                pl.BlockSpec(halo, lambda i: (jnp.minimum((i + 1) * per, last), 0)),
                _mods_spec(1, groups.cond), _normg_spec(1),
                _const_spec((D_MODEL, 3 * D_MODEL), (0, 0)), _const_spec((CONV_K, D_MODEL), (0, 0)),
                _const_spec((D_MODEL, D_MODEL), (0, 0))]
    late = wts["ffn"][1, 1]
    in_specs += [pl.BlockSpec(memory_space=pl.ANY)] * len(late)
    return pl.pallas_call(
        functools.partial(_conv_body, groups.prompt_tiles, groups.seq),
        out_shape=[jax.ShapeDtypeStruct((r, D_MODEL), F32) for r in groups.group_rows],
        grid=(groups.tiles,), in_specs=in_specs, out_specs=[groups.spec(D_MODEL, 0), groups.spec(D_MODEL, 1)],
        scratch_shapes=[pltpu.VMEM(w.shape, w.dtype) for w in late] + [pltpu.SemaphoreType.DMA((len(late),))],
        compiler_params=_params(1), name="conv_ffn",
    )(x, x, x, mods, norm_g, wts["conv_w_in"], wts["conv_w"], wts["conv_w_out"], *late)


ROPE_PARTNER = np.concatenate([np.arange(8, 16), np.arange(0, 8), np.arange(24, 32), np.arange(16, 24)])


def _rope_tables(seq_len, g_q, g_k):
    t = np.arange(seq_len)
    quarter = ROPE // 4
    inv_freq = ROPE_BASE ** (-np.arange(quarter, dtype=np.float64) / quarter)
    cos = np.ones((seq_len, LANES))
    sin = np.zeros((seq_len, LANES))
    for base, pos in ((NOPE, t // GRID_W), (NOPE + ROPE // 2, t % GRID_W)):
        ang = pos[:, None].astype(np.float64) * inv_freq[None, :]
        cos[:, base:base + quarter] = np.cos(ang)
        cos[:, base + quarter:base + 2 * quarter] = np.cos(ang)
        sin[:, base:base + quarter] = -np.sin(ang)
        sin[:, base + quarter:base + 2 * quarter] = np.sin(ang)
    cos, sin = jnp.asarray(cos, F32), jnp.asarray(sin, F32)
    out = []
    for g in (g_q, g_k):
        g = g.reshape(QK_DIM)
        g_pad = jnp.pad(g, (0, HEAD_PAD - QK_DIM))
        g_partner = jnp.pad(g[NOPE:][ROPE_PARTNER], (NOPE, HEAD_PAD - QK_DIM))
        out += [g_pad[None, :] * cos, g_partner[None, :] * sin]
    return tuple(out)


def _s5_params(lam_re, lam_im, log_dt, b_re, b_im, c_re, c_im):
    dt = jnp.exp(log_dt)[..., None]
    lr = jnp.minimum(lam_re, LAMBDA_RE_MAX)
    li = lam_im
    mag = jnp.exp(lr * dt)
    ang = li * dt
    ab_re = mag * jnp.cos(ang)
    ab_im = mag * jnp.sin(ang)
    den = lr * lr + li * li
    nr = ab_re - 1.0
    ni = ab_im
    co_re = (nr * lr + ni * li) / den
    co_im = (ni * lr - nr * li) / den
    bb_re = co_re[..., None] * b_re - co_im[..., None] * b_im
    bb_im = co_re[..., None] * b_im + co_im[..., None] * b_re
    per_q = S5_GROUPS // S5_QUARTERS
    eye = jnp.eye(per_q, dtype=F32)

    def in_mat(bb):
        bb = bb.reshape(2, S5_QUARTERS, per_q, S5_STATE, S5_GROUP)
        return jnp.einsum("dqgpi,gh->dqgihp", bb, eye).reshape(2, S5_QUARTERS, LANES, S5_QLANES)

    def out_mat(cc):
        cc = cc.reshape(2, S5_QUARTERS, per_q, S5_GROUP, S5_STATE)
        return jnp.einsum("dqgip,gh->dqgphi", cc, eye).reshape(2, S5_QUARTERS, S5_QLANES, LANES)

    bm = jnp.concatenate([in_mat(bb_re), in_mat(bb_im)], axis=-1).astype(BF16)
    cm = jnp.concatenate([out_mat(c_re), -out_mat(c_im)], axis=-2).astype(BF16)

    def rows(a):
        a = a.reshape(2, 1, S5_QUARTERS, S5_QLANES)
        return jnp.broadcast_to(a, (2, 2, S5_QUARTERS, S5_QLANES)).reshape(2, 2 * S5_QUARTERS, S5_QLANES)

    return bm, cm, jnp.stack([rows(ab_re), rows(ab_im)], axis=1)


def _state_rows(s):
    b = s.shape[0]
    s = s.reshape(b // 2, 2, 2, S5_QUARTERS, S5_QLANES)
    return s.transpose(0, 2, 1, 3, 4).reshape(b // 2, 2, 2 * S5_QUARTERS, S5_QLANES)


def _state_unrows(s, batch):
    s = s.reshape(batch // 2, 2, 2, S5_QUARTERS, S5_QLANES).transpose(0, 2, 1, 3, 4)
    return s.reshape(batch, 1, 2, S5_GROUPS, S5_STATE)


def kernel(x_prompt, x_sample, cache_ckv, cache_krope, state_ssm_re, state_ssm_im, c, c_ctx, w_ada, b_ada, norm_g, ffn_w_gate, ffn_w_up, ffn_w_down, ab_w_in, mla_g_q_lat, mla_g_kv_lat, mla_w_uq, mla_w_ukv, mla_g_qnorm, mla_g_knorm, s5_lam_re, s5_lam_im, s5_log_dt, s5_b_re, s5_b_im, s5_c_re, s5_c_im, s5_d, s5_w_glu, s5_b_glu, ab_w_out, conv_w_in, conv_w, conv_w_out):
    batch, seq, _ = x_prompt.shape
    dec_batch, dec_seq, _ = x_sample.shape
    past = cache_ckv.shape[2]

    w_in = ab_w_in[0]
    kr_cols = w_in[:, Q_LORA + KV_LORA:Q_LORA + KV_LORA + ROPE]
    zeros = jnp.zeros_like(kr_cols)
    w_uq = jnp.pad(mla_w_uq[0].reshape(Q_LORA, HEADS, QK_DIM), ((0, 0), (0, 0), (0, HEAD_PAD - QK_DIM)))
    w_ukv = mla_w_ukv[0].reshape(KV_LORA, HEADS, NOPE + V_DIM)
    w_uk = jnp.pad(w_ukv[:, :, :NOPE], ((0, 0), (0, 0), (0, HEAD_PAD - NOPE)))
    bm, cm, a_rows = _s5_params(s5_lam_re[0], s5_lam_im[0], s5_log_dt[0], s5_b_re[0], s5_b_im[0],
                                s5_c_re[0], s5_c_im[0])
    stacked = (ffn_w_gate, ffn_w_up, ffn_w_down)
    ffn = {(0, 0): tuple(w[0, 0].astype(BF16) for w in stacked)}
    w_in_cols = [w_in[:, :Q_LORA + KV_LORA], w_in[:, Q_LORA + KV_LORA + ROPE:], kr_cols, zeros, kr_cols, zeros]
    w_in_partner = [zeros, zeros, kr_cols[:, ROPE_PARTNER], zeros]
    w_uq3 = mla_w_uq[0].reshape(Q_LORA, HEADS, QK_DIM)
    w_uq_partner = jnp.pad(w_uq3[:, :, NOPE:][:, :, ROPE_PARTNER], ((0, 0), (0, 0), (NOPE, HEAD_PAD - QK_DIM)))
    wts = {
        "ffn": ffn,
        "w_in_rot": jnp.concatenate(w_in_cols + w_in_partner, axis=1).astype(BF16),
        "g_q_lat": mla_g_q_lat, "g_kv_lat": mla_g_kv_lat,
        "w_uq_rot": jnp.concatenate([w_uq, w_uq_partner], axis=1).reshape(Q_LORA, 2 * HEADS * HEAD_PAD).astype(BF16),
        "w_ukv": jnp.concatenate([w_uk.reshape(KV_LORA, HEADS * HEAD_PAD),
                                  w_ukv[:, :, NOPE:].reshape(KV_LORA, HEADS * V_DIM)], axis=1).astype(BF16),
        "g_qn": jnp.pad(mla_g_qnorm, ((0, 0), (0, HEAD_PAD - QK_DIM))),
        "g_kn": jnp.pad(mla_g_knorm, ((0, 0), (0, HEAD_PAD - QK_DIM))),
        "rope": _rope_tables(dec_seq, mla_g_qnorm, mla_g_knorm),
        "s5_b": bm, "s5_c": cm, "s5_a": a_rows,
        "s5_d": s5_d, "b_glu": s5_b_glu, "conv_w": conv_w[0].T,
    }

    cond = jnp.zeros((SUBLANES, D_MODEL), F32).at[0].set(c_ctx).at[1:1 + dec_batch].set(c)
    mods = _ada(cond, w_ada, b_ada).reshape(w_ada.shape[0], SUBLANES, N_MOD, D_MODEL)

    groups = _Groups(batch * seq, seq, dec_batch * dec_seq, dec_seq)
    rows_p, rows_l = groups.group_rows
    x1, q, k, v, u, ckv_p, kr_p = _mixin(
        x_prompt.reshape(rows_p, D_MODEL), x_sample.reshape(rows_l, D_MODEL), mods, norm_g, wts, groups)
    kr_pad = jnp.pad(cache_krope[:, 0].reshape(dec_batch * past, ROPE), ((0, 0), (NOPE, LANES - QK_DIM)))
    ctx = _ctx_kv(cache_ckv[:, 0].reshape(dec_batch * past, KV_LORA), kr_pad, wts)
    h0 = jnp.stack([_state_rows(state_ssm_re[:, 0]), _state_rows(state_ssm_im[:, 0])], axis=2)
    ffn_set = lambda l, s: [(w, (l, s)) for w in stacked]
    attn_p, yf, yb, down11, wts["conv_w_in"] = _attention(
        q, k, v, 0, rows_p, seq, seq, n_seq=2, zeros=[(groups.rows, S5_WIDTH)] * 2,
        cast=[(ffn_w_down, (1, 1)), (conv_w_in, (0,))])
    attn_l, gate01, up01, wts["w_out"], wts["w_glu"], wts["conv_w_out"] = _attention(
        q, k, v, rows_p, rows_l, dec_seq, 512, ctx,
        cast=[(ffn_w_gate, (0, 1)), (ffn_w_up, (0, 1)), (ab_w_out, (0,)), (s5_w_glu, (0,)), (conv_w_out, (0,))])
    attn = (attn_p, attn_l)
    yf, yb, fin, down01, *ffn[1, 0] = _s5(u, wts, 0, rows_p, seq, want_fin=True, y_prev=(yf, yb),
                                          cast=[(ffn_w_down, (0, 1))] + ffn_set(1, 0))
    ffn[0, 1] = (gate01, up01, down01)
    yf, yb, gate11, up11 = _s5(u, wts, rows_p, rows_l, dec_seq, h0=h0, y_prev=(yf, yb),
                               cast=[(ffn_w_gate, (1, 1)), (ffn_w_up, (1, 1))])
    ffn[1, 1] = (gate11, up11, down11)
    x3 = _mixout(x1, attn, u, yf, yb, mods, norm_g, wts, groups)
    y_p, y_s = _conv_mixer(x3, mods, norm_g, wts, groups)
    return (y_p.reshape(batch, seq, D_MODEL), y_s.reshape(dec_batch, dec_seq, D_MODEL),
            ckv_p.reshape(batch, 1, seq, KV_LORA), kr_p.reshape(batch, 1, seq, ROPE),
            _state_unrows(fin[:, :, 0], batch), _state_unrows(fin[:, :, 1], batch))
```

```python
import functools
import math

import jax
import jax.numpy as jnp
import numpy as np
from jax import lax
from jax.experimental import pallas as pl
from jax.experimental.pallas import tpu as pltpu

F32 = jnp.float32
BF16 = jnp.bfloat16

LANES = 128
SUBLANES = 8
VMEM_LIMIT_BYTES = 60 * 1024 * 1024

D_MODEL = 1024
D_FF = 2816
N_MOD = 9
EPS = 1e-6
HEADS = 8
Q_LORA = 384
KV_LORA = 256
NOPE = 64
ROPE = 32
V_DIM = 64
QK_DIM = NOPE + ROPE
HEAD_PAD = LANES
ROPE_BASE = 10000.0
GRID_W = 64
S5_WIDTH = 512
S5_GROUP = 16
S5_GROUPS = 32
S5_STATE = 64
S5_LANES = S5_GROUPS * S5_STATE
S5_QUARTERS = 4
S5_QLANES = S5_LANES // S5_QUARTERS
S5_SLABS = S5_QLANES // LANES
S5_TC = 128
S5_PITCH = S5_TC + 4
S5_SEGMENTS = 1
LAMBDA_RE_MAX = -1e-4
CONV_K = 3

ATT_ROWS = 2 * SUBLANES
TM = 512
FF_CHUNKS = tuple((c, min(512, D_FF - c)) for c in range(0, D_FF, 512))

O_CKV = Q_LORA
O_U = Q_LORA + KV_LORA
O_KR = O_U + S5_WIDTH


def _dot(a, b):
    return jnp.dot(a, b, preferred_element_type=F32)


def _sigmoid(x):
    return 1.0 / (1.0 + jnp.exp(-x))


def _rms(x, g, n):
    ms = jnp.sum(x * x, axis=-1, keepdims=True) * (1.0 / n)
    return x * lax.rsqrt(ms + EPS) * g


def _modulate(x, g, shift, scale):
    return _rms(x, g, D_MODEL) * (1.0 + scale) + shift


def _ffn(x, shift, scale, gate, g, wg_ref, wu_ref, wd_ref):
    h = _modulate(x, g, shift, scale).astype(BF16)
    acc = jnp.zeros(x.shape, F32)
    for c0, cs in FF_CHUNKS:
        gt = _dot(h, wg_ref[:, c0:c0 + cs])
        up = _dot(h, wu_ref[:, c0:c0 + cs])
        act = ((gt * _sigmoid(gt)) * up).astype(BF16)
        acc = acc + _dot(act, wd_ref[c0:c0 + cs, :])
    return x + (0.5 * gate) * acc


def _head_inv_rms(xh):
    return lax.rsqrt(jnp.sum(xh * xh, axis=-1, keepdims=True) * (1.0 / QK_DIM) + EPS)


def _head_norm(xh, g):
    return xh * _head_inv_rms(xh) * g


def _gelu_tanh(x):
    return x * (0.5 * (1.0 + jnp.tanh(math.sqrt(2.0 / math.pi) * (x + 0.044715 * (x * x * x)))))


def _const_spec(shape, index):
    return pl.BlockSpec(shape, lambda *_: index, pipeline_mode=pl.Buffered(1))


def _params(n_axes):
    return pltpu.CompilerParams(dimension_semantics=("arbitrary",) * n_axes, vmem_limit_bytes=VMEM_LIMIT_BYTES)


def _ffn_specs(weights, layer, sub):
    return [_const_spec(w.shape, (0, 0)) if w.ndim == 2 else
            _const_spec((None, None) + w.shape[2:], (layer, sub, 0, 0)) for w in weights]


def _mods_spec(layer, cond_of_tile):
    return pl.BlockSpec((None, None, N_MOD, D_MODEL), lambda i: (layer, cond_of_tile(i), 0, 0))


def _normg_spec(layer):
    return _const_spec((None, 3, D_MODEL), (layer, 0, 0))


def _row_spec(width, rows=TM):
    return pl.BlockSpec((rows, width), lambda i: (i, 0))


def _cast_job(items, n_steps, step_of):
    in_specs, out_specs, out_shape = [], [], []
    for w, lead in items:
        r, c = w.shape[len(lead):]
        assert r % (n_steps * 2 * SUBLANES) == 0
        slab = lambda *g: jnp.minimum(step_of(*g), n_steps - 1)
        in_specs.append(pl.BlockSpec((None,) * len(lead) + (r // n_steps, c),
                                     lambda *g, lead=lead, slab=slab: (*lead, slab(*g), 0)))
        out_specs.append(pl.BlockSpec((r // n_steps, c), lambda *g, slab=slab: (slab(*g), 0)))
        out_shape.append(jax.ShapeDtypeStruct((r, c), BF16))
    return in_specs, out_specs, out_shape


def _cast_slabs(src_refs, dst_refs):
    for src, dst in zip(src_refs, dst_refs):
        dst[...] = src[...].astype(BF16)


class _Groups:
    def __init__(self, prompt_rows, prompt_seq, latent_rows, latent_seq):
        assert prompt_rows % TM == 0 and latent_seq % TM == 0 and TM % prompt_seq == 0
        assert prompt_rows % latent_seq == 0
        self.group_rows = (prompt_rows, latent_rows)
        self.seq = (prompt_seq, latent_seq)
        self.rows = prompt_rows + latent_rows
        self.prompt_tiles = prompt_rows // TM
        self.tiles = self.rows // TM
        self.latent_tiles_per_seq = latent_seq // TM

    def cond(self, i):
        return jnp.where(i < self.prompt_tiles, 0,
                         1 + jnp.maximum(i - self.prompt_tiles, 0) // self.latent_tiles_per_seq)

    def spec(self, width, group):
        n = self.prompt_tiles
        if group == 0:
            return pl.BlockSpec((TM, width), lambda i: (jnp.minimum(i, n - 1), 0))
        return pl.BlockSpec((TM, width), lambda i: (jnp.maximum(i - n, 0), 0))


def _ada_body(c_ref, w_ref, b_ref, o_ref):
    c = c_ref[...]
    s = (c * _sigmoid(c)).astype(BF16)
    o_ref[...] = _dot(s, w_ref[...].astype(BF16)) + b_ref[...]


def _ada(cond, w_ada, b_ada):
    depth, _, n = w_ada.shape
    tn = 2304
    return pl.pallas_call(
        _ada_body,
        out_shape=jax.ShapeDtypeStruct((depth, SUBLANES, n), F32),
        grid=(depth, n // tn),
        in_specs=[pl.BlockSpec((SUBLANES, D_MODEL), lambda l, j: (0, 0)),
                  pl.BlockSpec((None, D_MODEL, tn), lambda l, j: (l, 0, j)),
                  pl.BlockSpec((None, 1, tn), lambda l, j: (l, 0, j))],
        out_specs=pl.BlockSpec((None, SUBLANES, tn), lambda l, j: (l, 0, j)),
        compiler_params=_params(2),
        name="ada",
    )(cond, w_ada, b_ada.reshape(depth, 1, n))


def _mixin_body(n_prompt, *refs):
    (xp_ref, xs_ref, mods_ref, g_ref, wg_ref, wu_ref, wd_ref, win_ref, gql_ref, gkv_ref, wuq_ref, wukv_ref,
     gq_ref, gk_ref, rqc_ref, rqs_ref, rkc_ref, rks_ref,
     x1_ref, q_ref, k_ref, v_ref, u_ref, ckv_ref, kr_ref) = refs
    is_prompt = pl.program_id(0) < n_prompt
    mods = mods_ref[...]
    g3 = g_ref[...]
    x = jnp.where(is_prompt, xp_ref[...], xs_ref[...])
    x1 = _ffn(x, mods[0:1], mods[1:2], mods[2:3], g3[0:1], wg_ref, wu_ref, wd_ref)
    x1_ref[...] = x1
    hn = _modulate(x1, g3[1:2], mods[3:4], mods[4:5]).astype(BF16)
    proj = _dot(hn, win_ref[:, 0:O_KR + LANES])
    u_ref[...] = proj[:, O_U:O_KR]
    ckv = _rms(proj[:, O_CKV:O_U], gkv_ref[...], KV_LORA)
    krg = proj[:, O_KR:O_KR + LANES]
    qn = _rms(proj[:, 0:Q_LORA], gql_ref[...], Q_LORA).astype(BF16)
    kv = _dot(ckv.astype(BF16), wukv_ref[...])
    v_ref[...] = kv[:, HEADS * HEAD_PAD:].astype(BF16)
    lane = lax.broadcasted_iota(jnp.int32, (1, LANES), 1)
    kr_only = jnp.where((lane >= NOPE) & (lane < QK_DIM), krg, 0.0)
    gk = gk_ref[...]
    head = lambda h: slice(h * HEAD_PAD, (h + 1) * HEAD_PAD)

    @pl.when(is_prompt)
    def _():
        ckv_ref[...] = ckv
        kr_ref[...] = krg[:, 0:ROPE]
        qraw = _dot(qn, wuq_ref[:, 0:HEADS * HEAD_PAD])
        gq = gq_ref[...]
        for h in range(HEADS):
            q_ref[:, head(h)] = _head_norm(qraw[:, head(h)], gq).astype(BF16)
            k_ref[:, head(h)] = _head_norm(kv[:, head(h)] + kr_only, gk).astype(BF16)

    @pl.when(jnp.logical_not(is_prompt))
    def _():
        qraw = _dot(qn, wuq_ref[...])
        kr_partner = _dot(hn, win_ref[:, O_KR + LANES:O_KR + 2 * LANES])
        k_rot = kr_only * rkc_ref[...] + kr_partner * rks_ref[...]
        for h in range(HEADS):
            qa = qraw[:, head(h)]
            qh = (qa * rqc_ref[...] + qraw[:, head(HEADS + h)] * rqs_ref[...]) * _head_inv_rms(qa)
            kh = (kv[:, head(h)] * gk + k_rot) * _head_inv_rms(kv[:, head(h)] + kr_only)
            q_ref[:, head(h)] = qh.astype(BF16)
            k_ref[:, head(h)] = kh.astype(BF16)


def _mixin(x_prompt, x_latent, mods, norm_g, wts, groups):
    n_prompt = groups.prompt_tiles
    w_in, w_uq = wts["w_in_rot"], wts["w_uq_rot"]
    in_specs = [groups.spec(D_MODEL, 0), groups.spec(D_MODEL, 1), _mods_spec(0, groups.cond), _normg_spec(0)]
    in_specs += _ffn_specs(wts["ffn"][0, 0], 0, 0) + [
        _const_spec(w_in.shape, (0, 0)),
        _const_spec((1, Q_LORA), (0, 0)),
        _const_spec((1, KV_LORA), (0, 0)),
        _const_spec(w_uq.shape, (0, 0)),
        _const_spec((KV_LORA, HEADS * HEAD_PAD + HEADS * V_DIM), (0, 0)),
        _const_spec((1, HEAD_PAD), (0, 0)),
        _const_spec((1, HEAD_PAD), (0, 0)),
    ] + [pl.BlockSpec((TM, LANES), lambda i: (jnp.maximum(i - n_prompt, 0) % groups.latent_tiles_per_seq, 0))] * 4
    args = [x_prompt, x_latent, mods, norm_g, *wts["ffn"][0, 0], w_in, wts["g_q_lat"],
            wts["g_kv_lat"], w_uq, wts["w_ukv"], wts["g_qn"], wts["g_kn"]] + list(wts["rope"])
    out_shape, out_specs = [], []
    for w, dt in ((D_MODEL, F32), (HEADS * HEAD_PAD, BF16), (HEADS * HEAD_PAD, BF16), (HEADS * V_DIM, BF16),
                  (S5_WIDTH, F32)):
        out_shape.append(jax.ShapeDtypeStruct((groups.rows, w), dt))
        out_specs.append(_row_spec(w))
    for w in (KV_LORA, ROPE):
        out_shape.append(jax.ShapeDtypeStruct((groups.group_rows[0], w), F32))
        out_specs.append(groups.spec(w, 0))
    return pl.pallas_call(
        functools.partial(_mixin_body, n_prompt),
        out_shape=out_shape, grid=(groups.tiles,), in_specs=in_specs, out_specs=out_specs,
        compiler_params=_params(1), name="ffn_mixin",
    )(*args)


def _ctx_body(ckv_ref, krp_ref, wukv_ref, gk_ref, k_ref, v_ref):
    kv = _dot(ckv_ref[...].astype(BF16), wukv_ref[...])
    v_ref[...] = kv[:, HEADS * HEAD_PAD:].astype(BF16)
    gk = gk_ref[...]
    krp = krp_ref[...]
    for h in range(HEADS):
        sl = slice(h * HEAD_PAD, (h + 1) * HEAD_PAD)
        k_ref[:, sl] = _head_norm(kv[:, sl] + krp, gk).astype(BF16)


def _ctx_kv(ckv, kr_padded, wts):
    rows = ckv.shape[0]
    return pl.pallas_call(
        _ctx_body,
        out_shape=[jax.ShapeDtypeStruct((rows, HEADS * HEAD_PAD), BF16),
                   jax.ShapeDtypeStruct((rows, HEADS * V_DIM), BF16)],
        grid=(rows // TM,),
        in_specs=[_row_spec(KV_LORA), _row_spec(LANES),
                  _const_spec((KV_LORA, HEADS * HEAD_PAD + HEADS * V_DIM), (0, 0)),
                  _const_spec((1, HEAD_PAD), (0, 0))],
        out_specs=[_row_spec(HEADS * HEAD_PAD), _row_spec(HEADS * V_DIM)],
        compiler_params=_params(1), name="ctx_kv",
    )(ckv, kr_padded, wts["w_ukv"], wts["g_kn"])


def _attn_body(n_kv, n_seq, n_zero, n_cast, *refs):
    refs = list(refs)
    q_ref = refs.pop(0)
    k_refs, v_refs, cast_in = refs[:n_kv], refs[n_kv:2 * n_kv], refs[2 * n_kv:2 * n_kv + n_cast]
    refs = refs[2 * n_kv + n_cast:]
    o_ref = refs.pop(0)
    zero_refs, cast_out = refs[:n_zero], refs[n_zero:n_zero + n_cast]
    s_ref, p_ref, r_ref = refs[n_zero + n_cast:]
    _cast_slabs(cast_in, cast_out)
    for z_ref in zero_refs:
        z_ref[...] = jnp.zeros(z_ref.shape, z_ref.dtype)
    tq = q_ref.shape[0] // n_seq
    c = (QK_DIM ** -0.5) * math.log2(math.e)
    lane = lax.broadcasted_iota(jnp.int32, (1, LANES), 1)
    n_keys = [k_ref.shape[0] // n_seq for k_ref in k_refs]
    spans = [slice(sum(n_keys[:i]), sum(n_keys[:i + 1])) for i in range(n_kv)]
    for sq, pair in [(sq, pair) for sq in range(n_seq) for pair in range(HEADS // 2)]:
        qrows = slice(sq * tq, (sq + 1) * tq)
        krows = [slice(sq * n, (sq + 1) * n) for n in n_keys]
        wide = slice(pair // 2 * 2 * LANES, (pair // 2 + 1) * 2 * LANES)
        mine = slice(pair % 2 * LANES, (pair % 2 + 1) * LANES)
        vsl = slice(pair * LANES, (pair + 1) * LANES)
        outs = []
        for h in (2 * pair, 2 * pair + 1):
            slot = h % 2
            sl = slice(h * HEAD_PAD, (h + 1) * HEAD_PAD)
            q = q_ref[qrows, sl]
            for k_ref, kr, span in zip(k_refs, krows, spans):
                s_ref[slot, :, span] = lax.dot_general(q, k_ref[kr, sl], (((1,), (1,)), ((), ())),
                                                       preferred_element_type=F32)
            for r0 in range(0, tq, ATT_ROWS):
                rows = slice(r0, r0 + ATT_ROWS)
                s = s_ref[slot, rows, :]
                e = jnp.exp2((s - jnp.max(s, axis=-1, keepdims=True)) * c)
                r_ref[slot, rows, :] = jnp.broadcast_to(1.0 / jnp.sum(e, axis=-1, keepdims=True), (ATT_ROWS, LANES))
                p_ref[slot, rows, :] = e.astype(BF16)
            o = functools.reduce(jnp.add, [_dot(p_ref[slot, :, span], v_ref[kr, wide])
                                           for v_ref, kr, span in zip(v_refs, krows, spans)])
            outs.append(o[:, mine] * r_ref[slot])
        o_ref[qrows, vsl] = jnp.where(lane < V_DIM, outs[0], outs[1]).astype(BF16)


def _attention(q, k, v, row0, rows, seq_len, tq, ctx=None, n_seq=1, zeros=(), cast=()):
    q_tiles = seq_len // tq
    n_seq = n_seq if q_tiles == 1 and ctx is None else 1
    q0, kv0 = row0 // (n_seq * tq), row0 // (n_seq * seq_len)
    kv_spec = lambda w: pl.BlockSpec((n_seq * seq_len, w), lambda b, t: (kv0 + b, 0))
    in_specs = [pl.BlockSpec((n_seq * tq, HEADS * HEAD_PAD), lambda b, t: (q0 + b * q_tiles + t, 0))]
    ks, vs = [kv_spec(HEADS * HEAD_PAD)], [kv_spec(HEADS * V_DIM)]
    kargs, vargs = [k], [v]
    n_keys = seq_len
    if ctx is not None:
        k_c, v_c = ctx
        past = k_c.shape[0] // (rows // seq_len)
        ks.insert(0, pl.BlockSpec((past, HEADS * HEAD_PAD), lambda b, t: (b, 0)))
        vs.insert(0, pl.BlockSpec((past, HEADS * V_DIM), lambda b, t: (b, 0)))
        kargs.insert(0, k_c)
        vargs.insert(0, v_c)
        n_keys += past
    grid = (rows // (n_seq * seq_len), q_tiles)
    out_shape = [jax.ShapeDtypeStruct((rows, HEADS * V_DIM), BF16)]
    out_specs = [pl.BlockSpec((n_seq * tq, HEADS * V_DIM), lambda b, t: (b * q_tiles + t, 0))]
    for shape in zeros:
        assert q_tiles == 1 and shape[0] % (SUBLANES * grid[0]) == 0
        out_shape.append(jax.ShapeDtypeStruct(shape, F32))
        out_specs.append(pl.BlockSpec((shape[0] // grid[0], shape[1]), lambda b, t: (b, 0)))
    in_specs, args = in_specs + ks + vs, [q, *kargs, *vargs]
    c_in, c_out, c_shape = _cast_job(cast, grid[0] * grid[1], lambda b, t: b * q_tiles + t)
    in_specs, args = in_specs + c_in, args + [w for w, _ in cast]
    out_specs, out_shape = out_specs + c_out, out_shape + c_shape
    return pl.pallas_call(
        functools.partial(_attn_body, len(kargs), n_seq, len(zeros), len(cast)),
        out_shape=out_shape, grid=grid, in_specs=in_specs, out_specs=out_specs,
        scratch_shapes=[pltpu.VMEM((2, tq, n_keys), F32), pltpu.VMEM((2, tq, n_keys), BF16),
                        pltpu.VMEM((2, tq, LANES), F32)],
        compiler_params=_params(2), name="attention",
    )(*args)


def _s5_project_in(u_ref, rows, bm_ref, d, q, bu_ref):
    qsl = slice(q * LANES, (q + 1) * LANES)
    lhs = jnp.concatenate([u_ref[0, rows, qsl], u_ref[1, rows, qsl]], axis=0).astype(BF16)
    bu = _dot(lhs, bm_ref[d, q])
    for b in range(2):
        r = b * S5_QUARTERS + q
        for j in range(2 * S5_SLABS):
            bu_ref[d, j, pl.ds(r * S5_PITCH, S5_TC), :] = bu[b * S5_TC:(b + 1) * S5_TC, j * LANES:(j + 1) * LANES]


def _s5_scan(bu_ref, st_ref, a_ref, carry_ref, steps):
    nrow = 2 * S5_QUARTERS
    order = [(d, j) for d in range(2) for j in range(S5_SLABS)]
    lanes = lambda j: slice(j * LANES, (j + 1) * LANES)
    carry = {(d, j): (carry_ref[d, 0, :, lanes(j)], carry_ref[d, 1, :, lanes(j)]) for d, j in order}
    for i in steps:
        for d, j in order:
            t = i if d == 0 else S5_TC - 1 - i
            rows = pl.ds(t, nrow, stride=S5_PITCH)
            sr, si = carry[d, j]
            ar, ai = a_ref[d, 0, :, lanes(j)], a_ref[d, 1, :, lanes(j)]
            nr = ar * sr - ai * si + bu_ref[d, j, rows, :]
            ni = ar * si + ai * sr + bu_ref[d, S5_SLABS + j, rows, :]
            st_ref[d, j, rows, :] = nr
            st_ref[d, S5_SLABS + j, rows, :] = ni
            carry[d, j] = (nr, ni)
    for d, j in order:
        carry_ref[d, 0, :, lanes(j)], carry_ref[d, 1, :, lanes(j)] = carry[d, j]


def _s5_project_out(st_ref, d, q, cm_ref, y_ref, rows):
    qsl = slice(q * LANES, (q + 1) * LANES)
    lhs = jnp.concatenate(
        [jnp.concatenate([st_ref[d, j, pl.ds((b * S5_QUARTERS + q) * S5_PITCH, S5_TC), :]
                          for j in range(2 * S5_SLABS)], axis=-1) for b in range(2)], axis=0).astype(BF16)
    y = _dot(lhs, cm_ref[d, q])
    y_ref[0, rows, qsl] = y[0:S5_TC]
    y_ref[1, rows, qsl] = y[S5_TC:2 * S5_TC]


def _s5_body(n_super, has_h0, want_fin, n_aliased, n_cast, *refs):
    refs = list(refs)
    uf_ref = refs.pop(0)
    ub_ref = refs.pop(0) if n_super > 1 else uf_ref
    bm_ref, cm_ref, a_ref = refs[:3]
    refs = refs[3:]
    h0_ref = refs.pop(0) if has_h0 else None
    refs = refs[n_aliased:]
    cast_in, refs = refs[:n_cast], refs[n_cast:]
    yf_ref, yb_ref = refs[:2]
    refs = refs[2:]
    fin_ref = refs.pop(0) if want_fin else None
    cast_out, refs = refs[:n_cast], refs[n_cast:]
    bu0_ref, bu1_ref, st0_ref, st1_ref, carry_ref = refs
    _cast_slabs(cast_in, cast_out)
    s = pl.program_id(0)
    lo, hi = slice(0, S5_TC), slice(S5_TC, 2 * S5_TC)
    n_seg = S5_SEGMENTS
    seg_steps = S5_TC // n_seg
    seg_pairs = 2 * S5_QUARTERS // n_seg

    @pl.when(s == 0)
    def _():
        bu1_ref[...] = jnp.zeros(bu1_ref.shape, F32)
        st0_ref[...] = jnp.zeros(st0_ref.shape, F32)
        carry_ref[...] = jnp.zeros(carry_ref.shape, F32)

    def half(bu_w, bu_r, st_w, st_r, rows_f, rows_b, tag):
        for k in range(n_seg):
            @pl.when(s > -(1 + tag * n_seg + k))
            def _():
                io = ((uf_ref, rows_f, yf_ref), (ub_ref, rows_b, yb_ref))
                sub = seg_steps // seg_pairs
                for m in range(k * seg_pairs, (k + 1) * seg_pairs):
                    d, q = divmod(m, S5_QUARTERS)
                    _s5_project_in(io[d][0], io[d][1], bm_ref, d, q, bu_w)
                    _s5_scan(bu_r, st_w, a_ref, carry_ref, range(m * sub, (m + 1) * sub))
                    _s5_project_out(st_r, d, q, cm_ref, io[d][2], io[d][1])

    half(bu0_ref, bu1_ref, st1_ref, st0_ref, lo, hi, 0)

    @pl.when(s > -(1 + 2 * n_seg))
    def _():
        if want_fin:
            fin_ref[...] = carry_ref[...]
        start = h0_ref[...] if has_h0 else jnp.zeros(carry_ref.shape, F32)
        if n_super > 1:
            start = jnp.where(s % n_super == 0, start, carry_ref[...])
        carry_ref[...] = start

    half(bu1_ref, bu0_ref, st0_ref, st1_ref, hi, lo, 1)


def _s5(u, wts, row0, rows, seq_len, h0=None, want_fin=False, y_prev=None, cast=()):
    batch = rows // seq_len
    n_super = seq_len // (2 * S5_TC)
    n_steps = (batch // 2) * n_super
    assert not (want_fin and n_super > 1) and row0 % (2 * seq_len) == 0
    pair0 = row0 // (2 * seq_len)
    u3 = u.reshape(u.shape[0] // seq_len, seq_len, S5_WIDTH)
    nrow = 2 * S5_QUARTERS
    blk = (2, 2 * S5_TC, S5_WIDTH)

    def chunk_spec(delay, reverse):
        def index(s):
            s = jnp.clip(s - delay, 0, n_steps - 1)
            c = s % n_super
            return (pair0 + s // n_super, n_super - 1 - c if reverse else c, 0)
        return pl.BlockSpec(blk, index)

    def state_spec(delay):
        return pl.BlockSpec((None, 2, 2, nrow, S5_QLANES),
                            lambda s: (jnp.clip(s - delay, 0, n_steps - 1) // n_super, 0, 0, 0, 0))

    in_specs, args = [chunk_spec(0, False)], [u3]
    if n_super > 1:
        in_specs.append(chunk_spec(0, True))
        args.append(u3)
    in_specs += [_const_spec((2, S5_QUARTERS, LANES, 2 * S5_QLANES), (0, 0, 0, 0)),
                 _const_spec((2, S5_QUARTERS, 2 * S5_QLANES, LANES), (0, 0, 0, 0)),
                 _const_spec((2, 2, nrow, S5_QLANES), (0, 0, 0, 0))]
    args += [wts["s5_b"], wts["s5_c"], wts["s5_a"]]
    if h0 is not None:
        in_specs.append(state_spec(0))
        args.append(h0)
    aliases = {}
    if y_prev is not None:
        for n, y in enumerate(y_prev):
            aliases[len(args)] = n
            in_specs.append(pl.BlockSpec(memory_space=pl.ANY))
            args.append(y.reshape(u3.shape))
    y_shape = jax.ShapeDtypeStruct(u3.shape, F32)
    out_shape, out_specs = [y_shape, y_shape], [chunk_spec(1, False), chunk_spec(1, True)]
    if want_fin:
        out_shape.append(jax.ShapeDtypeStruct((batch // 2, 2, 2, nrow, S5_QLANES), F32))
        out_specs.append(state_spec(1))
    c_in, c_out, c_shape = _cast_job(cast, n_steps, lambda s: s)
    in_specs, args = in_specs + c_in, args + [w for w, _ in cast]
    out_specs, out_shape = out_specs + c_out, out_shape + c_shape
    work = pltpu.VMEM((2, 2 * S5_SLABS, nrow * S5_PITCH, LANES), F32)
    return pl.pallas_call(
        functools.partial(_s5_body, n_super, h0 is not None, want_fin, len(aliases), len(cast)),
        out_shape=out_shape, grid=(n_steps + 1,), in_specs=in_specs, out_specs=out_specs,
        scratch_shapes=[work, work, work, work, pltpu.VMEM((2, 2, nrow, S5_QLANES), F32)],
        input_output_aliases=aliases,
        compiler_params=_params(1), name="s5",
    )(*args)


def _late_weights(hbm_refs, vmem_refs, sem):
    first = pl.program_id(0) == 0
    copies = [pltpu.make_async_copy(h, v, sem.at[n]) for n, (h, v) in enumerate(zip(hbm_refs, vmem_refs))]

    @pl.when(first)
    def _():
        for c in copies:
            c.start()

    def ready():
        @pl.when(first)
        def _():
            for c in copies:
                c.wait()

    return ready


def _mixout_body(n_prompt, x_ref, attn_p, attn_l, u_ref, yf_ref, yb_ref, mods0_ref, mods1_ref, g0_ref, g1_ref,
                 dsk_ref, wglu_ref, bglu_ref, wout_ref, wg0_ref, wu0_ref, wd0_ref, wg1_hbm, wu1_hbm, wd1_hbm, o_ref,
                 wg1_ref, wu1_ref, wd1_ref, sem):
    ready = _late_weights((wg1_hbm, wu1_hbm, wd1_hbm), (wg1_ref, wu1_ref, wd1_ref), sem)
    attn = jnp.where(pl.program_id(0) < n_prompt, attn_p[...], attn_l[...])
    mods = mods0_ref[...]
    g3 = g0_ref[...]
    y = dsk_ref[...] * u_ref[...] + yf_ref[...] + yb_ref[...]
    z = _gelu_tanh(y)
    s5o = z * _sigmoid(_dot(z.astype(BF16), wglu_ref[...]) + bglu_ref[...])
    half = HEADS * V_DIM
    mix = _dot(attn, wout_ref[0:half, :]) + _dot(s5o.astype(BF16), wout_ref[half:, :])
    x2 = x_ref[...] + mods[5:6] * mix
    x3 = _ffn(x2, mods[6:7], mods[7:8], mods[8:9], g3[2:3], wg0_ref, wu0_ref, wd0_ref)
    ready()
    mods = mods1_ref[...]
    o_ref[...] = _ffn(x3, mods[0:1], mods[1:2], mods[2:3], g1_ref[...][0:1], wg1_ref, wu1_ref, wd1_ref)


def _mixout(x, attn, u, yf, yb, mods, norm_g, wts, groups):
    in_specs = [_row_spec(D_MODEL), groups.spec(HEADS * V_DIM, 0), groups.spec(HEADS * V_DIM, 1)]
    in_specs += [_row_spec(S5_WIDTH)] * 3
    args = [x, *attn] + [a.reshape(groups.rows, S5_WIDTH) for a in (u, yf, yb)]
    in_specs += [_mods_spec(0, groups.cond), _mods_spec(1, groups.cond), _normg_spec(0), _normg_spec(1),
                 _const_spec((1, S5_WIDTH), (0, 0)), _const_spec((S5_WIDTH, S5_WIDTH), (0, 0)),
                 _const_spec((1, S5_WIDTH), (0, 0)), _const_spec((D_MODEL, D_MODEL), (0, 0))]
    late = wts["ffn"][1, 0]
    in_specs += _ffn_specs(wts["ffn"][0, 1], 0, 1) + [pl.BlockSpec(memory_space=pl.ANY)] * len(late)
    args += [mods, mods, norm_g, norm_g, wts["s5_d"], wts["w_glu"], wts["b_glu"], wts["w_out"],
             *wts["ffn"][0, 1], *late]
    return pl.pallas_call(
        functools.partial(_mixout_body, groups.prompt_tiles),
        out_shape=jax.ShapeDtypeStruct((groups.rows, D_MODEL), F32),
        grid=(groups.tiles,), in_specs=in_specs, out_specs=_row_spec(D_MODEL),
        scratch_shapes=[pltpu.VMEM(w.shape, w.dtype) for w in late] + [pltpu.SemaphoreType.DMA((len(late),))],
        compiler_params=_params(1), name="mixout_ffn",
    )(*args)


def _conv_body(n_prompt, seqs, x_ref, xp_ref, xn_ref, mods_ref, g_ref, cwin_ref, cw_ref, cwout_ref,
               wg_hbm, wu_hbm, wd_hbm, op_ref, ol_ref, wg_ref, wu_ref, wd_ref, sem):
    ready = _late_weights((wg_hbm, wu_hbm, wd_hbm), (wg_ref, wu_ref, wd_ref), sem)
    i = pl.program_id(0)
    is_prompt = i < n_prompt
    mods = mods_ref[...]
    g3 = g_ref[...]
    x = x_ref[...]
    n_ext = TM + 2 * SUBLANES
    xe = jnp.concatenate([xp_ref[...], x, xn_ref[...]], axis=0)
    hne = _modulate(xe, g3[1:2], mods[3:4], mods[4:5]).astype(BF16)
    pz = _dot(hne, cwin_ref[:, D_MODEL:3 * D_MODEL])
    z = pz[:, 0:D_MODEL] * pz[:, D_MODEL:2 * D_MODEL]
    gate_b = _dot(hne[SUBLANES:SUBLANES + TM], cwin_ref[:, 0:D_MODEL])
    main = slice(SUBLANES, SUBLANES + TM)
    row = i * TM + lax.broadcasted_iota(jnp.int32, (TM, 1), 0)
    pos = jnp.where(is_prompt, row % seqs[0], row % seqs[1])
    end = jnp.where(is_prompt, seqs[0] - 1, seqs[1] - 1)
    z_prev = jnp.where(pos == 0, 0.0, pltpu.roll(z, 1, 0)[main])
    z_next = jnp.where(pos == end, 0.0, pltpu.roll(z, n_ext - 1, 0)[main])
    cw = cw_ref[...]
    zc = z_prev * cw[0:1] + z[main] * cw[1:2] + z_next * cw[2:3]
    mix = _dot((gate_b * zc).astype(BF16), cwout_ref[...])
    x2 = x + mods[5:6] * mix
    ready()
    out = _ffn(x2, mods[6:7], mods[7:8], mods[8:9], g3[2:3], wg_ref, wu_ref, wd_ref)

    @pl.when(is_prompt)
    def _():
        op_ref[...] = out

    @pl.when(jnp.logical_not(is_prompt))
    def _():
        ol_ref[...] = out


def _conv_mixer(x, mods, norm_g, wts, groups):
    per = TM // SUBLANES
    last = groups.rows // SUBLANES - 1
    halo = (SUBLANES, D_MODEL)
    in_specs = [_row_spec(D_MODEL),
                pl.BlockSpec(halo, lambda i: (jnp.maximum(i * per - 1, 0), 0)),
                pl.BlockSpec(halo, lambda i: (jnp.minimum((i + 1) * per, last), 0)),
                _mods_spec(1, groups.cond), _normg_spec(1),
                _const_spec((D_MODEL, 3 * D_MODEL), (0, 0)), _const_spec((CONV_K, D_MODEL), (0, 0)),
                _const_spec((D_MODEL, D_MODEL), (0, 0))]
    late = wts["ffn"][1, 1]
    in_specs += [pl.BlockSpec(memory_space=pl.ANY)] * len(late)
    return pl.pallas_call(
        functools.partial(_conv_body, groups.prompt_tiles, groups.seq),
        out_shape=[jax.ShapeDtypeStruct((r, D_MODEL), F32) for r in groups.group_rows],
        grid=(groups.tiles,), in_specs=in_specs, out_specs=[groups.spec(D_MODEL, 0), groups.spec(D_MODEL, 1)],
        scratch_shapes=[pltpu.VMEM(w.shape, w.dtype) for w in late] + [pltpu.SemaphoreType.DMA((len(late),))],
        compiler_params=_params(1), name="conv_ffn",
    )(x, x, x, mods, norm_g, wts["conv_w_in"], wts["conv_w"], wts["conv_w_out"], *late)


ROPE_PARTNER = np.concatenate([np.arange(8, 16), np.arange(0, 8), np.arange(24, 32), np.arange(16, 24)])


def _rope_tables(seq_len, g_q, g_k):
    t = np.arange(seq_len)
    quarter = ROPE // 4
    inv_freq = ROPE_BASE ** (-np.arange(quarter, dtype=np.float64) / quarter)
    cos = np.ones((seq_len, LANES))
    sin = np.zeros((seq_len, LANES))
    for base, pos in ((NOPE, t // GRID_W), (NOPE + ROPE // 2, t % GRID_W)):
        ang = pos[:, None].astype(np.float64) * inv_freq[None, :]
        cos[:, base:base + quarter] = np.cos(ang)
        cos[:, base + quarter:base + 2 * quarter] = np.cos(ang)
        sin[:, base:base + quarter] = -np.sin(ang)
        sin[:, base + quarter:base + 2 * quarter] = np.sin(ang)
    cos, sin = jnp.asarray(cos, F32), jnp.asarray(sin, F32)
    out = []
    for g in (g_q, g_k):
        g = g.reshape(QK_DIM)
        g_pad = jnp.pad(g, (0, HEAD_PAD - QK_DIM))
        g_partner = jnp.pad(g[NOPE:][ROPE_PARTNER], (NOPE, HEAD_PAD - QK_DIM))
        out += [g_pad[None, :] * cos, g_partner[None, :] * sin]
    return tuple(out)


def _s5_params(lam_re, lam_im, log_dt, b_re, b_im, c_re, c_im):
    dt = jnp.exp(log_dt)[..., None]
    lr = jnp.minimum(lam_re, LAMBDA_RE_MAX)
    li = lam_im
    mag = jnp.exp(lr * dt)
    ang = li * dt
    ab_re = mag * jnp.cos(ang)
    ab_im = mag * jnp.sin(ang)
    den = lr * lr + li * li
    nr = ab_re - 1.0
    ni = ab_im
    co_re = (nr * lr + ni * li) / den
    co_im = (ni * lr - nr * li) / den
    bb_re = co_re[..., None] * b_re - co_im[..., None] * b_im
    bb_im = co_re[..., None] * b_im + co_im[..., None] * b_re
    per_q = S5_GROUPS // S5_QUARTERS
    eye = jnp.eye(per_q, dtype=F32)

    def in_mat(bb):
        bb = bb.reshape(2, S5_QUARTERS, per_q, S5_STATE, S5_GROUP)
        return jnp.einsum("dqgpi,gh->dqgihp", bb, eye).reshape(2, S5_QUARTERS, LANES, S5_QLANES)

    def out_mat(cc):
        cc = cc.reshape(2, S5_QUARTERS, per_q, S5_GROUP, S5_STATE)
        return jnp.einsum("dqgip,gh->dqgphi", cc, eye).reshape(2, S5_QUARTERS, S5_QLANES, LANES)

    bm = jnp.concatenate([in_mat(bb_re), in_mat(bb_im)], axis=-1).astype(BF16)
    cm = jnp.concatenate([out_mat(c_re), -out_mat(c_im)], axis=-2).astype(BF16)

    def rows(a):
        a = a.reshape(2, 1, S5_QUARTERS, S5_QLANES)
        return jnp.broadcast_to(a, (2, 2, S5_QUARTERS, S5_QLANES)).reshape(2, 2 * S5_QUARTERS, S5_QLANES)

    return bm, cm, jnp.stack([rows(ab_re), rows(ab_im)], axis=1)


def _state_rows(s):
    b = s.shape[0]
    s = s.reshape(b // 2, 2, 2, S5_QUARTERS, S5_QLANES)
    return s.transpose(0, 2, 1, 3, 4).reshape(b // 2, 2, 2 * S5_QUARTERS, S5_QLANES)


def _state_unrows(s, batch):
    s = s.reshape(batch // 2, 2, 2, S5_QUARTERS, S5_QLANES).transpose(0, 2, 1, 3, 4)
    return s.reshape(batch, 1, 2, S5_GROUPS, S5_STATE)


def kernel(x_prompt, x_sample, cache_ckv, cache_krope, state_ssm_re, state_ssm_im, c, c_ctx, w_ada, b_ada, norm_g, ffn_w_gate, ffn_w_up, ffn_w_down, ab_w_in, mla_g_q_lat, mla_g_kv_lat, mla_w_uq, mla_w_ukv, mla_g_qnorm, mla_g_knorm, s5_lam_re, s5_lam_im, s5_log_dt, s5_b_re, s5_b_im, s5_c_re, s5_c_im, s5_d, s5_w_glu, s5_b_glu, ab_w_out, conv_w_in, conv_w, conv_w_out):
    batch, seq, _ = x_prompt.shape
    dec_batch, dec_seq, _ = x_sample.shape
    past = cache_ckv.shape[2]

    w_in = ab_w_in[0]
    kr_cols = w_in[:, Q_LORA + KV_LORA:Q_LORA + KV_LORA + ROPE]
    zeros = jnp.zeros_like(kr_cols)
    w_uq = jnp.pad(mla_w_uq[0].reshape(Q_LORA, HEADS, QK_DIM), ((0, 0), (0, 0), (0, HEAD_PAD - QK_DIM)))
    w_ukv = mla_w_ukv[0].reshape(KV_LORA, HEADS, NOPE + V_DIM)
    w_uk = jnp.pad(w_ukv[:, :, :NOPE], ((0, 0), (0, 0), (0, HEAD_PAD - NOPE)))
    bm, cm, a_rows = _s5_params(s5_lam_re[0], s5_lam_im[0], s5_log_dt[0], s5_b_re[0], s5_b_im[0],
                                s5_c_re[0], s5_c_im[0])
    stacked = (ffn_w_gate, ffn_w_up, ffn_w_down)
    ffn = {(0, 0): tuple(w[0, 0].astype(BF16) for w in stacked)}
    w_in_cols = [w_in[:, :Q_LORA + KV_LORA], w_in[:, Q_LORA + KV_LORA + ROPE:], kr_cols, zeros, kr_cols, zeros]
    w_in_partner = [zeros, zeros, kr_cols[:, ROPE_PARTNER], zeros]
    w_uq3 = mla_w_uq[0].reshape(Q_LORA, HEADS, QK_DIM)
    w_uq_partner = jnp.pad(w_uq3[:, :, NOPE:][:, :, ROPE_PARTNER], ((0, 0), (0, 0), (NOPE, HEAD_PAD - QK_DIM)))
    wts = {
        "ffn": ffn,
        "w_in_rot": jnp.concatenate(w_in_cols + w_in_partner, axis=1).astype(BF16),
        "g_q_lat": mla_g_q_lat, "g_kv_lat": mla_g_kv_lat,
        "w_uq_rot": jnp.concatenate([w_uq, w_uq_partner], axis=1).reshape(Q_LORA, 2 * HEADS * HEAD_PAD).astype(BF16),
        "w_ukv": jnp.concatenate([w_uk.reshape(KV_LORA, HEADS * HEAD_PAD),
                                  w_ukv[:, :, NOPE:].reshape(KV_LORA, HEADS * V_DIM)], axis=1).astype(BF16),
        "g_qn": jnp.pad(mla_g_qnorm, ((0, 0), (0, HEAD_PAD - QK_DIM))),
        "g_kn": jnp.pad(mla_g_knorm, ((0, 0), (0, HEAD_PAD - QK_DIM))),
        "rope": _rope_tables(dec_seq, mla_g_qnorm, mla_g_knorm),
        "s5_b": bm, "s5_c": cm, "s5_a": a_rows,
        "s5_d": s5_d, "b_glu": s5_b_glu, "conv_w": conv_w[0].T,
    }

    cond = jnp.zeros((SUBLANES, D_MODEL), F32).at[0].set(c_ctx).at[1:1 + dec_batch].set(c)
    mods = _ada(cond, w_ada, b_ada).reshape(w_ada.shape[0], SUBLANES, N_MOD, D_MODEL)

    groups = _Groups(batch * seq, seq, dec_batch * dec_seq, dec_seq)
    rows_p, rows_l = groups.group_rows
    x1, q, k, v, u, ckv_p, kr_p = _mixin(
        x_prompt.reshape(rows_p, D_MODEL), x_sample.reshape(rows_l, D_MODEL), mods, norm_g, wts, groups)
    kr_pad = jnp.pad(cache_krope[:, 0].reshape(dec_batch * past, ROPE), ((0, 0), (NOPE, LANES - QK_DIM)))
    ctx = _ctx_kv(cache_ckv[:, 0].reshape(dec_batch * past, KV_LORA), kr_pad, wts)
    h0 = jnp.stack([_state_rows(state_ssm_re[:, 0]), _state_rows(state_ssm_im[:, 0])], axis=2)
    ffn_set = lambda l, s: [(w, (l, s)) for w in stacked]
    attn_p, yf, yb, down11, wts["conv_w_in"] = _attention(
        q, k, v, 0, rows_p, seq, seq, n_seq=2, zeros=[(groups.rows, S5_WIDTH)] * 2,
        cast=[(ffn_w_down, (1, 1)), (conv_w_in, (0,))])
    attn_l, gate01, up01, wts["w_out"], wts["w_glu"], wts["conv_w_out"] = _attention(
        q, k, v, rows_p, rows_l, dec_seq, 256, ctx,
        cast=[(ffn_w_gate, (0, 1)), (ffn_w_up, (0, 1)), (ab_w_out, (0,)), (s5_w_glu, (0,)), (conv_w_out, (0,))])
    attn = (attn_p, attn_l)
    yf, yb, fin, down01, *ffn[1, 0] = _s5(u, wts, 0, rows_p, seq, want_fin=True, y_prev=(yf, yb),
                                          cast=[(ffn_w_down, (0, 1))] + ffn_set(1, 0))
    ffn[0, 1] = (gate01, up01, down01)
    yf, yb, gate11, up11 = _s5(u, wts, rows_p, rows_l, dec_seq, h0=h0, y_prev=(yf, yb),
                               cast=[(ffn_w_gate, (1, 1)), (ffn_w_up, (1, 1))])
    ffn[1, 1] = (gate11, up11, down11)
    x3 = _mixout(x1, attn, u, yf, yb, mods, norm_g, wts, groups)
    y_p, y_s = _conv_mixer(x3, mods, norm_g, wts, groups)
    return (y_p.reshape(batch, seq, D_MODEL), y_s.reshape(dec_batch, dec_seq, D_MODEL),
            ckv_p.reshape(batch, 1, seq, KV_LORA), kr_p.reshape(batch, 1, seq, ROPE),
            _state_unrows(fin[:, :, 0], batch), _state_unrows(fin[:, :, 1], batch))
```

```python
import functools
import math

import jax
import jax.numpy as jnp
import numpy as np
from jax import lax
from jax.experimental import pallas as pl
from jax.experimental.pallas import tpu as pltpu

F32 = jnp.float32
BF16 = jnp.bfloat16

LANES = 128
SUBLANES = 8
VMEM_LIMIT_BYTES = 60 * 1024 * 1024

D_MODEL = 1024
D_FF = 2816
N_MOD = 9
EPS = 1e-6
HEADS = 8
Q_LORA = 384
KV_LORA = 256
NOPE = 64
ROPE = 32
V_DIM = 64
QK_DIM = NOPE + ROPE
HEAD_PAD = LANES
ROPE_BASE = 10000.0
GRID_W = 64
S5_WIDTH = 512
S5_GROUP = 16
S5_GROUPS = 32
S5_STATE = 64
S5_LANES = S5_GROUPS * S5_STATE
S5_QUARTERS = 4
S5_QLANES = S5_LANES // S5_QUARTERS
S5_SLABS = S5_QLANES // LANES
S5_TC = 128
S5_PITCH = S5_TC + 4
S5_SEGMENTS = 1
LAMBDA_RE_MAX = -1e-4
CONV_K = 3

ATT_ROWS = 2 * SUBLANES
TM = 512
FF_CHUNKS = tuple((c, min(512, D_FF - c)) for c in range(0, D_FF, 512))

O_CKV = Q_LORA
O_U = Q_LORA + KV_LORA
O_KR = O_U + S5_WIDTH


def _dot(a, b):
    return jnp.dot(a, b, preferred_element_type=F32)


def _sigmoid(x):
    return 1.0 / (1.0 + jnp.exp(-x))


def _rms(x, g, n):
    ms = jnp.sum(x * x, axis=-1, keepdims=True) * (1.0 / n)
    return x * lax.rsqrt(ms + EPS) * g


def _modulate(x, g, shift, scale):
    return _rms(x, g, D_MODEL) * (1.0 + scale) + shift


def _ffn(x, shift, scale, gate, g, wg_ref, wu_ref, wd_ref):
    h = _modulate(x, g, shift, scale).astype(BF16)
    acc = jnp.zeros(x.shape, F32)
    for c0, cs in FF_CHUNKS:
        gt = _dot(h, wg_ref[:, c0:c0 + cs])
        up = _dot(h, wu_ref[:, c0:c0 + cs])
        act = ((gt * _sigmoid(gt)) * up).astype(BF16)
        acc = acc + _dot(act, wd_ref[c0:c0 + cs, :])
    return x + (0.5 * gate) * acc


def _head_inv_rms(xh):
    return lax.rsqrt(jnp.sum(xh * xh, axis=-1, keepdims=True) * (1.0 / QK_DIM) + EPS)


def _head_norm(xh, g):
    return xh * _head_inv_rms(xh) * g


def _gelu_tanh(x):
    return x * (0.5 * (1.0 + jnp.tanh(math.sqrt(2.0 / math.pi) * (x + 0.044715 * (x * x * x)))))


def _const_spec(shape, index):
    return pl.BlockSpec(shape, lambda *_: index, pipeline_mode=pl.Buffered(1))


def _params(n_axes):
    return pltpu.CompilerParams(dimension_semantics=("arbitrary",) * n_axes, vmem_limit_bytes=VMEM_LIMIT_BYTES)


def _ffn_specs(weights, layer, sub):
    return [_const_spec(w.shape, (0, 0)) if w.ndim == 2 else
            _const_spec((None, None) + w.shape[2:], (layer, sub, 0, 0)) for w in weights]


def _mods_spec(layer, cond_of_tile):
    return pl.BlockSpec((None, None, N_MOD, D_MODEL), lambda i: (layer, cond_of_tile(i), 0, 0))


def _normg_spec(layer):
    return _const_spec((None, 3, D_MODEL), (layer, 0, 0))


def _row_spec(width, rows=TM):
    return pl.BlockSpec((rows, width), lambda i: (i, 0))


def _cast_job(items, n_steps, step_of):
    in_specs, out_specs, out_shape = [], [], []
    for w, lead in items:
        r, c = w.shape[len(lead):]
        assert r % (n_steps * 2 * SUBLANES) == 0
        slab = lambda *g: jnp.minimum(step_of(*g), n_steps - 1)
        in_specs.append(pl.BlockSpec((None,) * len(lead) + (r // n_steps, c),
                                     lambda *g, lead=lead, slab=slab: (*lead, slab(*g), 0)))
        out_specs.append(pl.BlockSpec((r // n_steps, c), lambda *g, slab=slab: (slab(*g), 0)))
        out_shape.append(jax.ShapeDtypeStruct((r, c), BF16))
    return in_specs, out_specs, out_shape


def _cast_slabs(src_refs, dst_refs):
    for src, dst in zip(src_refs, dst_refs):
        dst[...] = src[...].astype(BF16)


class _Groups:
    def __init__(self, prompt_rows, prompt_seq, latent_rows, latent_seq):
        assert prompt_rows % TM == 0 and latent_seq % TM == 0 and TM % prompt_seq == 0
        assert prompt_rows % latent_seq == 0
        self.group_rows = (prompt_rows, latent_rows)
        self.seq = (prompt_seq, latent_seq)
        self.rows = prompt_rows + latent_rows
        self.prompt_tiles = prompt_rows // TM
        self.tiles = self.rows // TM
        self.latent_tiles_per_seq = latent_seq // TM

    def cond(self, i):
        return jnp.where(i < self.prompt_tiles, 0,
                         1 + jnp.maximum(i - self.prompt_tiles, 0) // self.latent_tiles_per_seq)

    def spec(self, width, group):
        n = self.prompt_tiles
        if group == 0:
            return pl.BlockSpec((TM, width), lambda i: (jnp.minimum(i, n - 1), 0))
        return pl.BlockSpec((TM, width), lambda i: (jnp.maximum(i - n, 0), 0))


def _ada_body(c_ref, w_ref, b_ref, o_ref):
    c = c_ref[...]
    s = (c * _sigmoid(c)).astype(BF16)
    o_ref[...] = _dot(s, w_ref[...].astype(BF16)) + b_ref[...]


def _ada(cond, w_ada, b_ada):
    depth, _, n = w_ada.shape
    tn = 2304
    return pl.pallas_call(
        _ada_body,
        out_shape=jax.ShapeDtypeStruct((depth, SUBLANES, n), F32),
        grid=(depth, n // tn),
        in_specs=[pl.BlockSpec((SUBLANES, D_MODEL), lambda l, j: (0, 0)),
                  pl.BlockSpec((None, D_MODEL, tn), lambda l, j: (l, 0, j)),
                  pl.BlockSpec((None, 1, tn), lambda l, j: (l, 0, j))],
        out_specs=pl.BlockSpec((None, SUBLANES, tn), lambda l, j: (l, 0, j)),
        compiler_params=_params(2),
        name="ada",
    )(cond, w_ada, b_ada.reshape(depth, 1, n))


def _mixin_body(n_prompt, *refs):
    (xp_ref, xs_ref, mods_ref, g_ref, wg_ref, wu_ref, wd_ref, win_ref, gql_ref, gkv_ref, wuq_ref, wukv_ref,
     gq_ref, gk_ref, rqc_ref, rqs_ref, rkc_ref, rks_ref,
     x1_ref, q_ref, k_ref, v_ref, u_ref, ckv_ref, kr_ref) = refs
    is_prompt = pl.program_id(0) < n_prompt
    mods = mods_ref[...]
    g3 = g_ref[...]
    x = jnp.where(is_prompt, xp_ref[...], xs_ref[...])
    x1 = _ffn(x, mods[0:1], mods[1:2], mods[2:3], g3[0:1], wg_ref, wu_ref, wd_ref)
    x1_ref[...] = x1
    hn = _modulate(x1, g3[1:2], mods[3:4], mods[4:5]).astype(BF16)
    proj = _dot(hn, win_ref[:, 0:O_KR + LANES])
    u_ref[...] = proj[:, O_U:O_KR]
    ckv = _rms(proj[:, O_CKV:O_U], gkv_ref[...], KV_LORA)
    krg = proj[:, O_KR:O_KR + LANES]
    qn = _rms(proj[:, 0:Q_LORA], gql_ref[...], Q_LORA).astype(BF16)
    kv = _dot(ckv.astype(BF16), wukv_ref[...])
    v_ref[...] = kv[:, HEADS * HEAD_PAD:].astype(BF16)
    lane = lax.broadcasted_iota(jnp.int32, (1, LANES), 1)
    kr_only = jnp.where((lane >= NOPE) & (lane < QK_DIM), krg, 0.0)
    gk = gk_ref[...]
    head = lambda h: slice(h * HEAD_PAD, (h + 1) * HEAD_PAD)

    @pl.when(is_prompt)
    def _():
        ckv_ref[...] = ckv
        kr_ref[...] = krg[:, 0:ROPE]
        qraw = _dot(qn, wuq_ref[:, 0:HEADS * HEAD_PAD])
        gq = gq_ref[...]
        for h in range(HEADS):
            q_ref[:, head(h)] = _head_norm(qraw[:, head(h)], gq).astype(BF16)
            k_ref[:, head(h)] = _head_norm(kv[:, head(h)] + kr_only, gk).astype(BF16)

    @pl.when(jnp.logical_not(is_prompt))
    def _():
        qraw = _dot(qn, wuq_ref[...])
        kr_partner = _dot(hn, win_ref[:, O_KR + LANES:O_KR + 2 * LANES])
        k_rot = kr_only * rkc_ref[...] + kr_partner * rks_ref[...]
        for h in range(HEADS):
            qa = qraw[:, head(h)]
            qh = (qa * rqc_ref[...] + qraw[:, head(HEADS + h)] * rqs_ref[...]) * _head_inv_rms(qa)
            kh = (kv[:, head(h)] * gk + k_rot) * _head_inv_rms(kv[:, head(h)] + kr_only)
            q_ref[:, head(h)] = qh.astype(BF16)
            k_ref[:, head(h)] = kh.astype(BF16)


def _mixin(x_prompt, x_latent, mods, norm_g, wts, groups):
    n_prompt = groups.prompt_tiles
    w_in, w_uq = wts["w_in_rot"], wts["w_uq_rot"]
    in_specs = [groups.spec(D_MODEL, 0), groups.spec(D_MODEL, 1), _mods_spec(0, groups.cond), _normg_spec(0)]
    in_specs += _ffn_specs(wts["ffn"][0, 0], 0, 0) + [
        _const_spec(w_in.shape, (0, 0)),
        _const_spec((1, Q_LORA), (0, 0)),
        _const_spec((1, KV_LORA), (0, 0)),
        _const_spec(w_uq.shape, (0, 0)),
        _const_spec((KV_LORA, HEADS * HEAD_PAD + HEADS * V_DIM), (0, 0)),
        _const_spec((1, HEAD_PAD), (0, 0)),
        _const_spec((1, HEAD_PAD), (0, 0)),
    ] + [pl.BlockSpec((TM, LANES), lambda i: (jnp.maximum(i - n_prompt, 0) % groups.latent_tiles_per_seq, 0))] * 4
    args = [x_prompt, x_latent, mods, norm_g, *wts["ffn"][0, 0], w_in, wts["g_q_lat"],
            wts["g_kv_lat"], w_uq, wts["w_ukv"], wts["g_qn"], wts["g_kn"]] + list(wts["rope"])
    out_shape, out_specs = [], []
    for w, dt in ((D_MODEL, F32), (HEADS * HEAD_PAD, BF16), (HEADS * HEAD_PAD, BF16), (HEADS * V_DIM, BF16),
                  (S5_WIDTH, F32)):
        out_shape.append(jax.ShapeDtypeStruct((groups.rows, w), dt))
        out_specs.append(_row_spec(w))
    for w in (KV_LORA, ROPE):
        out_shape.append(jax.ShapeDtypeStruct((groups.group_rows[0], w), F32))
        out_specs.append(groups.spec(w, 0))
    return pl.pallas_call(
        functools.partial(_mixin_body, n_prompt),
        out_shape=out_shape, grid=(groups.tiles,), in_specs=in_specs, out_specs=out_specs,
        compiler_params=_params(1), name="ffn_mixin",
    )(*args)


def _ctx_body(ckv_ref, krp_ref, wukv_ref, gk_ref, k_ref, v_ref):
    kv = _dot(ckv_ref[...].astype(BF16), wukv_ref[...])
    v_ref[...] = kv[:, HEADS * HEAD_PAD:].astype(BF16)
    gk = gk_ref[...]
    krp = krp_ref[...]
    for h in range(HEADS):
        sl = slice(h * HEAD_PAD, (h + 1) * HEAD_PAD)
        k_ref[:, sl] = _head_norm(kv[:, sl] + krp, gk).astype(BF16)


def _ctx_kv(ckv, kr_padded, wts):
    rows = ckv.shape[0]
    return pl.pallas_call(
        _ctx_body,
        out_shape=[jax.ShapeDtypeStruct((rows, HEADS * HEAD_PAD), BF16),
                   jax.ShapeDtypeStruct((rows, HEADS * V_DIM), BF16)],
        grid=(rows // TM,),
        in_specs=[_row_spec(KV_LORA), _row_spec(LANES),
                  _const_spec((KV_LORA, HEADS * HEAD_PAD + HEADS * V_DIM), (0, 0)),
                  _const_spec((1, HEAD_PAD), (0, 0))],
        out_specs=[_row_spec(HEADS * HEAD_PAD), _row_spec(HEADS * V_DIM)],
        compiler_params=_params(1), name="ctx_kv",
    )(ckv, kr_padded, wts["w_ukv"], wts["g_kn"])


def _attn_body(n_kv, n_seq, n_zero, n_cast, n_after, *refs):
    refs = list(refs)
    q_ref = refs.pop(0)
    k_refs, v_refs, cast_in = refs[:n_kv], refs[n_kv:2 * n_kv], refs[2 * n_kv:2 * n_kv + n_cast]
    refs = refs[2 * n_kv + n_cast + n_after:]
    o_ref = refs.pop(0)
    zero_refs, cast_out = refs[:n_zero], refs[n_zero:n_zero + n_cast]
    s_ref, p_ref, r_ref = refs[n_zero + n_cast:]
    _cast_slabs(cast_in, cast_out)
    for z_ref in zero_refs:
        z_ref[...] = jnp.zeros(z_ref.shape, z_ref.dtype)
    tq = q_ref.shape[0] // n_seq
    c = (QK_DIM ** -0.5) * math.log2(math.e)
    lane = lax.broadcasted_iota(jnp.int32, (1, LANES), 1)
    n_keys = [k_ref.shape[0] // n_seq for k_ref in k_refs]
    spans = [slice(sum(n_keys[:i]), sum(n_keys[:i + 1])) for i in range(n_kv)]
    for sq, pair in [(sq, pair) for sq in range(n_seq) for pair in range(HEADS // 2)]:
        qrows = slice(sq * tq, (sq + 1) * tq)
        krows = [slice(sq * n, (sq + 1) * n) for n in n_keys]
        wide = slice(pair // 2 * 2 * LANES, (pair // 2 + 1) * 2 * LANES)
        mine = slice(pair % 2 * LANES, (pair % 2 + 1) * LANES)
        vsl = slice(pair * LANES, (pair + 1) * LANES)
        outs = []
        for h in (2 * pair, 2 * pair + 1):
            slot = h % 2
            sl = slice(h * HEAD_PAD, (h + 1) * HEAD_PAD)
            q = q_ref[qrows, sl]
            for k_ref, kr, span in zip(k_refs, krows, spans):
                s_ref[slot, :, span] = lax.dot_general(q, k_ref[kr, sl], (((1,), (1,)), ((), ())),
                                                       preferred_element_type=F32)
            for r0 in range(0, tq, ATT_ROWS):
                rows = slice(r0, r0 + ATT_ROWS)
                s = s_ref[slot, rows, :]
                e = jnp.exp2((s - jnp.max(s, axis=-1, keepdims=True)) * c)
                r_ref[slot, rows, :] = jnp.broadcast_to(1.0 / jnp.sum(e, axis=-1, keepdims=True), (ATT_ROWS, LANES))
                p_ref[slot, rows, :] = e.astype(BF16)
            o = functools.reduce(jnp.add, [_dot(p_ref[slot, :, span], v_ref[kr, wide])
                                           for v_ref, kr, span in zip(v_refs, krows, spans)])
            outs.append(o[:, mine] * r_ref[slot])
        o_ref[qrows, vsl] = jnp.where(lane < V_DIM, outs[0], outs[1]).astype(BF16)


def _attention(q, k, v, row0, rows, seq_len, tq, ctx=None, n_seq=1, zeros=(), cast=(), after=()):
    q_tiles = seq_len // tq
    n_seq = n_seq if q_tiles == 1 and ctx is None else 1
    q0, kv0 = row0 // (n_seq * tq), row0 // (n_seq * seq_len)
    kv_spec = lambda w: pl.BlockSpec((n_seq * seq_len, w), lambda b, t: (kv0 + b, 0))
    in_specs = [pl.BlockSpec((n_seq * tq, HEADS * HEAD_PAD), lambda b, t: (q0 + b * q_tiles + t, 0))]
    ks, vs = [kv_spec(HEADS * HEAD_PAD)], [kv_spec(HEADS * V_DIM)]
    kargs, vargs = [k], [v]
    n_keys = seq_len
    if ctx is not None:
        k_c, v_c = ctx
        past = k_c.shape[0] // (rows // seq_len)
        ks.insert(0, pl.BlockSpec((past, HEADS * HEAD_PAD), lambda b, t: (b, 0)))
        vs.insert(0, pl.BlockSpec((past, HEADS * V_DIM), lambda b, t: (b, 0)))
        kargs.insert(0, k_c)
        vargs.insert(0, v_c)
        n_keys += past
    grid = (rows // (n_seq * seq_len), q_tiles)
    out_shape = [jax.ShapeDtypeStruct((rows, HEADS * V_DIM), BF16)]
    out_specs = [pl.BlockSpec((n_seq * tq, HEADS * V_DIM), lambda b, t: (b * q_tiles + t, 0))]
    for shape in zeros:
        assert q_tiles == 1 and shape[0] % (SUBLANES * grid[0]) == 0
        out_shape.append(jax.ShapeDtypeStruct(shape, F32))
        out_specs.append(pl.BlockSpec((shape[0] // grid[0], shape[1]), lambda b, t: (b, 0)))
    in_specs, args = in_specs + ks + vs, [q, *kargs, *vargs]
    c_in, c_out, c_shape = _cast_job(cast, grid[0] * grid[1], lambda b, t: b * q_tiles + t)
    in_specs, args = in_specs + c_in, args + [w for w, _ in cast]
    out_specs, out_shape = out_specs + c_out, out_shape + c_shape
    in_specs, args = in_specs + [pl.BlockSpec(memory_space=pl.ANY)] * len(after), args + list(after)
    return pl.pallas_call(
        functools.partial(_attn_body, len(kargs), n_seq, len(zeros), len(cast), len(after)),
        out_shape=out_shape, grid=grid, in_specs=in_specs, out_specs=out_specs,
        scratch_shapes=[pltpu.VMEM((2, tq, n_keys), F32), pltpu.VMEM((2, tq, n_keys), BF16),
                        pltpu.VMEM((2, tq, LANES), F32)],
        compiler_params=_params(2), name="attention",
    )(*args)


def _s5_project_in(u_ref, rows, bm_ref, d, q, bu_ref):
    qsl = slice(q * LANES, (q + 1) * LANES)
    lhs = jnp.concatenate([u_ref[0, rows, qsl], u_ref[1, rows, qsl]], axis=0).astype(BF16)
    bu = _dot(lhs, bm_ref[d, q])
    for b in range(2):
        r = b * S5_QUARTERS + q
        for j in range(2 * S5_SLABS):
            bu_ref[d, j, pl.ds(r * S5_PITCH, S5_TC), :] = bu[b * S5_TC:(b + 1) * S5_TC, j * LANES:(j + 1) * LANES]


def _s5_scan(bu_ref, st_ref, a_ref, carry_ref, steps):
    nrow = 2 * S5_QUARTERS
    order = [(d, j) for d in range(2) for j in range(S5_SLABS)]
    lanes = lambda j: slice(j * LANES, (j + 1) * LANES)
    carry = {(d, j): (carry_ref[d, 0, :, lanes(j)], carry_ref[d, 1, :, lanes(j)]) for d, j in order}
    for i in steps:
        for d, j in order:
            t = i if d == 0 else S5_TC - 1 - i
            rows = pl.ds(t, nrow, stride=S5_PITCH)
            sr, si = carry[d, j]
            ar, ai = a_ref[d, 0, :, lanes(j)], a_ref[d, 1, :, lanes(j)]
            nr = ar * sr - ai * si + bu_ref[d, j, rows, :]
            ni = ar * si + ai * sr + bu_ref[d, S5_SLABS + j, rows, :]
            st_ref[d, j, rows, :] = nr
            st_ref[d, S5_SLABS + j, rows, :] = ni
            carry[d, j] = (nr, ni)
    for d, j in order:
        carry_ref[d, 0, :, lanes(j)], carry_ref[d, 1, :, lanes(j)] = carry[d, j]


def _s5_project_out(st_ref, d, q, cm_ref, y_ref, rows):
    qsl = slice(q * LANES, (q + 1) * LANES)
    lhs = jnp.concatenate(
        [jnp.concatenate([st_ref[d, j, pl.ds((b * S5_QUARTERS + q) * S5_PITCH, S5_TC), :]
                          for j in range(2 * S5_SLABS)], axis=-1) for b in range(2)], axis=0).astype(BF16)
    y = _dot(lhs, cm_ref[d, q])
    y_ref[0, rows, qsl] = y[0:S5_TC]
    y_ref[1, rows, qsl] = y[S5_TC:2 * S5_TC]


def _s5_body(n_super, has_h0, want_fin, n_aliased, n_cast, *refs):
    refs = list(refs)
    uf_ref = refs.pop(0)
    ub_ref = refs.pop(0) if n_super > 1 else uf_ref
    bm_ref, cm_ref, a_ref = refs[:3]
    refs = refs[3:]
    h0_ref = refs.pop(0) if has_h0 else None
    refs = refs[n_aliased:]
    cast_in, refs = refs[:n_cast], refs[n_cast:]
    yf_ref, yb_ref = refs[:2]
    refs = refs[2:]
    fin_ref = refs.pop(0) if want_fin else None
    cast_out, refs = refs[:n_cast], refs[n_cast:]
    bu0_ref, bu1_ref, st0_ref, st1_ref, carry_ref = refs
    _cast_slabs(cast_in, cast_out)
    s = pl.program_id(0)
    lo, hi = slice(0, S5_TC), slice(S5_TC, 2 * S5_TC)
    n_seg = S5_SEGMENTS
    seg_steps = S5_TC // n_seg
    seg_pairs = 2 * S5_QUARTERS // n_seg

    @pl.when(s == 0)
    def _():
        bu1_ref[...] = jnp.zeros(bu1_ref.shape, F32)
        st0_ref[...] = jnp.zeros(st0_ref.shape, F32)
        carry_ref[...] = jnp.zeros(carry_ref.shape, F32)

    def half(bu_w, bu_r, st_w, st_r, rows_f, rows_b, tag):
        for k in range(n_seg):
            @pl.when(s > -(1 + tag * n_seg + k))
            def _():
                io = ((uf_ref, rows_f, yf_ref), (ub_ref, rows_b, yb_ref))
                sub = seg_steps // seg_pairs
                for m in range(k * seg_pairs, (k + 1) * seg_pairs):
                    d, q = divmod(m, S5_QUARTERS)
                    _s5_project_in(io[d][0], io[d][1], bm_ref, d, q, bu_w)
                    _s5_scan(bu_r, st_w, a_ref, carry_ref, range(m * sub, (m + 1) * sub))
                    _s5_project_out(st_r, d, q, cm_ref, io[d][2], io[d][1])

    half(bu0_ref, bu1_ref, st1_ref, st0_ref, lo, hi, 0)

    @pl.when(s > -(1 + 2 * n_seg))
    def _():
        if want_fin:
            fin_ref[...] = carry_ref[...]
        start = h0_ref[...] if has_h0 else jnp.zeros(carry_ref.shape, F32)
        if n_super > 1:
            start = jnp.where(s % n_super == 0, start, carry_ref[...])
        carry_ref[...] = start

    half(bu1_ref, bu0_ref, st0_ref, st1_ref, hi, lo, 1)


def _s5(u, wts, row0, rows, seq_len, h0=None, want_fin=False, y_prev=None, cast=()):
    batch = rows // seq_len
    n_super = seq_len // (2 * S5_TC)
    n_steps = (batch // 2) * n_super
    assert not (want_fin and n_super > 1) and row0 % (2 * seq_len) == 0
    pair0 = row0 // (2 * seq_len)
    u3 = u.reshape(u.shape[0] // seq_len, seq_len, S5_WIDTH)
    nrow = 2 * S5_QUARTERS
    blk = (2, 2 * S5_TC, S5_WIDTH)

    def chunk_spec(delay, reverse):
        def index(s):
            s = jnp.clip(s - delay, 0, n_steps - 1)
            c = s % n_super
            return (pair0 + s // n_super, n_super - 1 - c if reverse else c, 0)
        return pl.BlockSpec(blk, index)

    def state_spec(delay):
        return pl.BlockSpec((None, 2, 2, nrow, S5_QLANES),
                            lambda s: (jnp.clip(s - delay, 0, n_steps - 1) // n_super, 0, 0, 0, 0))

    in_specs, args = [chunk_spec(0, False)], [u3]
    if n_super > 1:
        in_specs.append(chunk_spec(0, True))
        args.append(u3)
    in_specs += [_const_spec((2, S5_QUARTERS, LANES, 2 * S5_QLANES), (0, 0, 0, 0)),
                 _const_spec((2, S5_QUARTERS, 2 * S5_QLANES, LANES), (0, 0, 0, 0)),
                 _const_spec((2, 2, nrow, S5_QLANES), (0, 0, 0, 0))]
    args += [wts["s5_b"], wts["s5_c"], wts["s5_a"]]
    if h0 is not None:
        in_specs.append(state_spec(0))
        args.append(h0)
    aliases = {}
    if y_prev is not None:
        for n, y in enumerate(y_prev):
            aliases[len(args)] = n
            in_specs.append(pl.BlockSpec(memory_space=pl.ANY))
            args.append(y.reshape(u3.shape))
    y_shape = jax.ShapeDtypeStruct(u3.shape, F32)
    out_shape, out_specs = [y_shape, y_shape], [chunk_spec(1, False), chunk_spec(1, True)]
    if want_fin:
        out_shape.append(jax.ShapeDtypeStruct((batch // 2, 2, 2, nrow, S5_QLANES), F32))
        out_specs.append(state_spec(1))
    c_in, c_out, c_shape = _cast_job(cast, n_steps, lambda s: s)
    in_specs, args = in_specs + c_in, args + [w for w, _ in cast]
    out_specs, out_shape = out_specs + c_out, out_shape + c_shape
    work = pltpu.VMEM((2, 2 * S5_SLABS, nrow * S5_PITCH, LANES), F32)
    return pl.pallas_call(
        functools.partial(_s5_body, n_super, h0 is not None, want_fin, len(aliases), len(cast)),
        out_shape=out_shape, grid=(n_steps + 1,), in_specs=in_specs, out_specs=out_specs,
        scratch_shapes=[work, work, work, work, pltpu.VMEM((2, 2, nrow, S5_QLANES), F32)],
        input_output_aliases=aliases,
        compiler_params=_params(1), name="s5",
    )(*args)


def _late_weights(hbm_refs, vmem_refs, sem):
    first = pl.program_id(0) == 0
    copies = [pltpu.make_async_copy(h, v, sem.at[n]) for n, (h, v) in enumerate(zip(hbm_refs, vmem_refs))]

    @pl.when(first)
    def _():
        for c in copies:
            c.start()

    def ready():
        @pl.when(first)
        def _():
            for c in copies:
                c.wait()

    return ready


def _mixout_body(n_prompt, x_ref, attn_p, attn_l, u_ref, yf_ref, yb_ref, mods0_ref, mods1_ref, g0_ref, g1_ref,
                 dsk_ref, wglu_ref, bglu_ref, wout_ref, wg0_ref, wu0_ref, wd0_ref, wg1_hbm, wu1_hbm, wd1_hbm, o_ref,
                 wg1_ref, wu1_ref, wd1_ref, sem):
    ready = _late_weights((wg1_hbm, wu1_hbm, wd1_hbm), (wg1_ref, wu1_ref, wd1_ref), sem)
    attn = jnp.where(pl.program_id(0) < n_prompt, attn_p[...], attn_l[...])
    mods = mods0_ref[...]
    g3 = g0_ref[...]
    y = dsk_ref[...] * u_ref[...] + yf_ref[...] + yb_ref[...]
    z = _gelu_tanh(y)
    s5o = z * _sigmoid(_dot(z.astype(BF16), wglu_ref[...]) + bglu_ref[...])
    half = HEADS * V_DIM
    mix = _dot(attn, wout_ref[0:half, :]) + _dot(s5o.astype(BF16), wout_ref[half:, :])
    x2 = x_ref[...] + mods[5:6] * mix
    x3 = _ffn(x2, mods[6:7], mods[7:8], mods[8:9], g3[2:3], wg0_ref, wu0_ref, wd0_ref)
    ready()
    mods = mods1_ref[...]
    o_ref[...] = _ffn(x3, mods[0:1], mods[1:2], mods[2:3], g1_ref[...][0:1], wg1_ref, wu1_ref, wd1_ref)


def _mixout(x, attn, u, yf, yb, mods, norm_g, wts, groups):
    in_specs = [_row_spec(D_MODEL), groups.spec(HEADS * V_DIM, 0), groups.spec(HEADS * V_DIM, 1)]
    in_specs += [_row_spec(S5_WIDTH)] * 3
    args = [x, *attn] + [a.reshape(groups.rows, S5_WIDTH) for a in (u, yf, yb)]
    in_specs += [_mods_spec(0, groups.cond), _mods_spec(1, groups.cond), _normg_spec(0), _normg_spec(1),
                 _const_spec((1, S5_WIDTH), (0, 0)), _const_spec((S5_WIDTH, S5_WIDTH), (0, 0)),
                 _const_spec((1, S5_WIDTH), (0, 0)), _const_spec((D_MODEL, D_MODEL), (0, 0))]
    late = wts["ffn"][1, 0]
    in_specs += _ffn_specs(wts["ffn"][0, 1], 0, 1) + [pl.BlockSpec(memory_space=pl.ANY)] * len(late)
    args += [mods, mods, norm_g, norm_g, wts["s5_d"], wts["w_glu"], wts["b_glu"], wts["w_out"],
             *wts["ffn"][0, 1], *late]
    return pl.pallas_call(
        functools.partial(_mixout_body, groups.prompt_tiles),
        out_shape=jax.ShapeDtypeStruct((groups.rows, D_MODEL), F32),
        grid=(groups.tiles,), in_specs=in_specs, out_specs=_row_spec(D_MODEL),
        scratch_shapes=[pltpu.VMEM(w.shape, w.dtype) for w in late] + [pltpu.SemaphoreType.DMA((len(late),))],
        compiler_params=_params(1), name="mixout_ffn",
    )(*args)


def _conv_body(n_prompt, seqs, x_ref, xp_ref, xn_ref, mods_ref, g_ref, cwin_ref, cw_ref, cwout_ref,
               wg_hbm, wu_hbm, wd_hbm, op_ref, ol_ref, wg_ref, wu_ref, wd_ref, sem):
    ready = _late_weights((wg_hbm, wu_hbm, wd_hbm), (wg_ref, wu_ref, wd_ref), sem)
    i = pl.program_id(0)
    is_prompt = i < n_prompt
    mods = mods_ref[...]
    g3 = g_ref[...]
    x = x_ref[...]
    n_ext = TM + 2 * SUBLANES
    xe = jnp.concatenate([xp_ref[...], x, xn_ref[...]], axis=0)
    hne = _modulate(xe, g3[1:2], mods[3:4], mods[4:5]).astype(BF16)
    pz = _dot(hne, cwin_ref[:, D_MODEL:3 * D_MODEL])
    z = pz[:, 0:D_MODEL] * pz[:, D_MODEL:2 * D_MODEL]
    gate_b = _dot(hne[SUBLANES:SUBLANES + TM], cwin_ref[:, 0:D_MODEL])
    main = slice(SUBLANES, SUBLANES + TM)
    row = i * TM + lax.broadcasted_iota(jnp.int32, (TM, 1), 0)
    pos = jnp.where(is_prompt, row % seqs[0], row % seqs[1])
    end = jnp.where(is_prompt, seqs[0] - 1, seqs[1] - 1)
    z_prev = jnp.where(pos == 0, 0.0, pltpu.roll(z, 1, 0)[main])
    z_next = jnp.where(pos == end, 0.0, pltpu.roll(z, n_ext - 1, 0)[main])
    cw = cw_ref[...]
    zc = z_prev * cw[0:1] + z[main] * cw[1:2] + z_next * cw[2:3]
    mix = _dot((gate_b * zc).astype(BF16), cwout_ref[...])
    x2 = x + mods[5:6] * mix
    ready()
    out = _ffn(x2, mods[6:7], mods[7:8], mods[8:9], g3[2:3], wg_ref, wu_ref, wd_ref)

    @pl.when(is_prompt)
    def _():
        op_ref[...] = out

    @pl.when(jnp.logical_not(is_prompt))
    def _():
        ol_ref[...] = out


def _conv_mixer(x, mods, norm_g, wts, groups):
    per = TM // SUBLANES
    last = groups.rows // SUBLANES - 1
    halo = (SUBLANES, D_MODEL)
    in_specs = [_row_spec(D_MODEL),
                pl.BlockSpec(halo, lambda i: (jnp.maximum(i * per - 1, 0), 0)),
                pl.BlockSpec(halo, lambda i: (jnp.minimum((i + 1) * per, last), 0)),
                _mods_spec(1, groups.cond), _normg_spec(1),
                _const_spec((D_MODEL, 3 * D_MODEL), (0, 0)), _const_spec((CONV_K, D_MODEL), (0, 0)),
                _const_spec((D_MODEL, D_MODEL), (0, 0))]
    late = wts["ffn"][1, 1]
    in_specs += [pl.BlockSpec(memory_space=pl.ANY)] * len(late)
    return pl.pallas_call(
        functools.partial(_conv_body, groups.prompt_tiles, groups.seq),
        out_shape=[jax.ShapeDtypeStruct((r, D_MODEL), F32) for r in groups.group_rows],
        grid=(groups.tiles,), in_specs=in_specs, out_specs=[groups.spec(D_MODEL, 0), groups.spec(D_MODEL, 1)],
        scratch_shapes=[pltpu.VMEM(w.shape, w.dtype) for w in late] + [pltpu.SemaphoreType.DMA((len(late),))],
        compiler_params=_params(1), name="conv_ffn",
    )(x, x, x, mods, norm_g, wts["conv_w_in"], wts["conv_w"], wts["conv_w_out"], *late)


ROPE_PARTNER = np.concatenate([np.arange(8, 16), np.arange(0, 8), np.arange(24, 32), np.arange(16, 24)])


def _rope_tables(seq_len, g_q, g_k):
    t = np.arange(seq_len)
    quarter = ROPE // 4
    inv_freq = ROPE_BASE ** (-np.arange(quarter, dtype=np.float64) / quarter)
    cos = np.ones((seq_len, LANES))
    sin = np.zeros((seq_len, LANES))
    for base, pos in ((NOPE, t // GRID_W), (NOPE + ROPE // 2, t % GRID_W)):
        ang = pos[:, None].astype(np.float64) * inv_freq[None, :]
        cos[:, base:base + quarter] = np.cos(ang)
        cos[:, base + quarter:base + 2 * quarter] = np.cos(ang)
        sin[:, base:base + quarter] = -np.sin(ang)
        sin[:, base + quarter:base + 2 * quarter] = np.sin(ang)
    cos, sin = jnp.asarray(cos, F32), jnp.asarray(sin, F32)
    out = []
    for g in (g_q, g_k):
        g = g.reshape(QK_DIM)
        g_pad = jnp.pad(g, (0, HEAD_PAD - QK_DIM))
        g_partner = jnp.pad(g[NOPE:][ROPE_PARTNER], (NOPE, HEAD_PAD - QK_DIM))
        out += [g_pad[None, :] * cos, g_partner[None, :] * sin]
    return tuple(out)


def _s5_params(lam_re, lam_im, log_dt, b_re, b_im, c_re, c_im):
    dt = jnp.exp(log_dt)[..., None]
    lr = jnp.minimum(lam_re, LAMBDA_RE_MAX)
    li = lam_im
    mag = jnp.exp(lr * dt)
    ang = li * dt
    ab_re = mag * jnp.cos(ang)
    ab_im = mag * jnp.sin(ang)
    den = lr * lr + li * li
    nr = ab_re - 1.0
    ni = ab_im
    co_re = (nr * lr + ni * li) / den
    co_im = (ni * lr - nr * li) / den
    bb_re = co_re[..., None] * b_re - co_im[..., None] * b_im
    bb_im = co_re[..., None] * b_im + co_im[..., None] * b_re
    per_q = S5_GROUPS // S5_QUARTERS
    eye = jnp.eye(per_q, dtype=F32)

    def in_mat(bb):
        bb = bb.reshape(2, S5_QUARTERS, per_q, S5_STATE, S5_GROUP)
        return jnp.einsum("dqgpi,gh->dqgihp", bb, eye).reshape(2, S5_QUARTERS, LANES, S5_QLANES)

    def out_mat(cc):
        cc = cc.reshape(2, S5_QUARTERS, per_q, S5_GROUP, S5_STATE)
        return jnp.einsum("dqgip,gh->dqgphi", cc, eye).reshape(2, S5_QUARTERS, S5_QLANES, LANES)

    bm = jnp.concatenate([in_mat(bb_re), in_mat(bb_im)], axis=-1).astype(BF16)
    cm = jnp.concatenate([out_mat(c_re), -out_mat(c_im)], axis=-2).astype(BF16)

    def rows(a):
        a = a.reshape(2, 1, S5_QUARTERS, S5_QLANES)
        return jnp.broadcast_to(a, (2, 2, S5_QUARTERS, S5_QLANES)).reshape(2, 2 * S5_QUARTERS, S5_QLANES)

    return bm, cm, jnp.stack([rows(ab_re), rows(ab_im)], axis=1)


def _state_rows(s):
    b = s.shape[0]
    s = s.reshape(b // 2, 2, 2, S5_QUARTERS, S5_QLANES)
    return s.transpose(0, 2, 1, 3, 4).reshape(b // 2, 2, 2 * S5_QUARTERS, S5_QLANES)


def _state_unrows(s, batch):
    s = s.reshape(batch // 2, 2, 2, S5_QUARTERS, S5_QLANES).transpose(0, 2, 1, 3, 4)
    return s.reshape(batch, 1, 2, S5_GROUPS, S5_STATE)


def kernel(x_prompt, x_sample, cache_ckv, cache_krope, state_ssm_re, state_ssm_im, c, c_ctx, w_ada, b_ada, norm_g, ffn_w_gate, ffn_w_up, ffn_w_down, ab_w_in, mla_g_q_lat, mla_g_kv_lat, mla_w_uq, mla_w_ukv, mla_g_qnorm, mla_g_knorm, s5_lam_re, s5_lam_im, s5_log_dt, s5_b_re, s5_b_im, s5_c_re, s5_c_im, s5_d, s5_w_glu, s5_b_glu, ab_w_out, conv_w_in, conv_w, conv_w_out):
    batch, seq, _ = x_prompt.shape
    dec_batch, dec_seq, _ = x_sample.shape
    past = cache_ckv.shape[2]

    w_in = ab_w_in[0]
    kr_cols = w_in[:, Q_LORA + KV_LORA:Q_LORA + KV_LORA + ROPE]
    zeros = jnp.zeros_like(kr_cols)
    w_uq = jnp.pad(mla_w_uq[0].reshape(Q_LORA, HEADS, QK_DIM), ((0, 0), (0, 0), (0, HEAD_PAD - QK_DIM)))
    w_ukv = mla_w_ukv[0].reshape(KV_LORA, HEADS, NOPE + V_DIM)
    w_uk = jnp.pad(w_ukv[:, :, :NOPE], ((0, 0), (0, 0), (0, HEAD_PAD - NOPE)))
    bm, cm, a_rows = _s5_params(s5_lam_re[0], s5_lam_im[0], s5_log_dt[0], s5_b_re[0], s5_b_im[0],
                                s5_c_re[0], s5_c_im[0])
    stacked = (ffn_w_gate, ffn_w_up, ffn_w_down)
    ffn = {(0, 0): tuple(w[0, 0].astype(BF16) for w in stacked)}
    w_in_cols = [w_in[:, :Q_LORA + KV_LORA], w_in[:, Q_LORA + KV_LORA + ROPE:], kr_cols, zeros, kr_cols, zeros]
    w_in_partner = [zeros, zeros, kr_cols[:, ROPE_PARTNER], zeros]
    w_uq3 = mla_w_uq[0].reshape(Q_LORA, HEADS, QK_DIM)
    w_uq_partner = jnp.pad(w_uq3[:, :, NOPE:][:, :, ROPE_PARTNER], ((0, 0), (0, 0), (NOPE, HEAD_PAD - QK_DIM)))
    wts = {
        "ffn": ffn,
        "w_in_rot": jnp.concatenate(w_in_cols + w_in_partner, axis=1).astype(BF16),
        "g_q_lat": mla_g_q_lat, "g_kv_lat": mla_g_kv_lat,
        "w_uq_rot": jnp.concatenate([w_uq, w_uq_partner], axis=1).reshape(Q_LORA, 2 * HEADS * HEAD_PAD).astype(BF16),
        "w_ukv": jnp.concatenate([w_uk.reshape(KV_LORA, HEADS * HEAD_PAD),
                                  w_ukv[:, :, NOPE:].reshape(KV_LORA, HEADS * V_DIM)], axis=1).astype(BF16),
        "g_qn": jnp.pad(mla_g_qnorm, ((0, 0), (0, HEAD_PAD - QK_DIM))),
        "g_kn": jnp.pad(mla_g_knorm, ((0, 0), (0, HEAD_PAD - QK_DIM))),
        "rope": _rope_tables(dec_seq, mla_g_qnorm, mla_g_knorm),
        "s5_b": bm, "s5_c": cm, "s5_a": a_rows,
        "s5_d": s5_d, "w_glu": s5_w_glu[0].astype(BF16), "b_glu": s5_b_glu, "w_out": ab_w_out[0].astype(BF16),
        "conv_w": conv_w[0].T, "conv_w_out": conv_w_out[0].astype(BF16),
    }

    cond = jnp.zeros((SUBLANES, D_MODEL), F32).at[0].set(c_ctx).at[1:1 + dec_batch].set(c)
    mods = _ada(cond, w_ada, b_ada).reshape(w_ada.shape[0], SUBLANES, N_MOD, D_MODEL)

    groups = _Groups(batch * seq, seq, dec_batch * dec_seq, dec_seq)
    rows_p, rows_l = groups.group_rows
    x1, q, k, v, u, ckv_p, kr_p = _mixin(
        x_prompt.reshape(rows_p, D_MODEL), x_sample.reshape(rows_l, D_MODEL), mods, norm_g, wts, groups)
    kr_pad = jnp.pad(cache_krope[:, 0].reshape(dec_batch * past, ROPE), ((0, 0), (NOPE, LANES - QK_DIM)))
    ctx = _ctx_kv(cache_ckv[:, 0].reshape(dec_batch * past, KV_LORA), kr_pad, wts)
    h0 = jnp.stack([_state_rows(state_ssm_re[:, 0]), _state_rows(state_ssm_im[:, 0])], axis=2)
    ffn_set = lambda l, s: [(w, (l, s)) for w in stacked]
    attn_l, *ffn[0, 1] = _attention(q, k, v, rows_p, rows_l, dec_seq, 256, ctx, cast=ffn_set(0, 1))
    attn_p, yf, yb, down11, wts["conv_w_in"] = _attention(
        q, k, v, 0, rows_p, seq, seq, n_seq=2, zeros=[(groups.rows, S5_WIDTH)] * 2,
        cast=[(ffn_w_down, (1, 1)), (conv_w_in, (0,))], after=[attn_l])
    attn = (attn_p, attn_l)
    yf, yb, fin, *ffn[1, 0] = _s5(u, wts, 0, rows_p, seq, want_fin=True, y_prev=(yf, yb), cast=ffn_set(1, 0))
    yf, yb, gate11, up11 = _s5(u, wts, rows_p, rows_l, dec_seq, h0=h0, y_prev=(yf, yb),
                               cast=[(ffn_w_gate, (1, 1)), (ffn_w_up, (1, 1))])
    ffn[1, 1] = (gate11, up11, down11)
    x3 = _mixout(x1, attn, u, yf, yb, mods, norm_g, wts, groups)
    y_p, y_s = _conv_mixer(x3, mods, norm_g, wts, groups)
    return (y_p.reshape(batch, seq, D_MODEL), y_s.reshape(dec_batch, dec_seq, D_MODEL),
            ckv_p.reshape(batch, 1, seq, KV_LORA), kr_p.reshape(batch, 1, seq, ROPE),
            _state_unrows(fin[:, :, 0], batch), _state_unrows(fin[:, :, 1], batch))
```
